```python
import math
import jax, jax.numpy as jnp
from jax import lax
import numpy as np

D_MODEL = 2048
BATCH = 2
SEQ = 8192
DEPTH = 2
DEC_BATCH = 16
DEC_SEQ = 32
PAST_LEN = 1024

CHUNK = 64
N_MEM = 256
HG_HEADS = 8
HG_DK = 128
HG_DV = 128
D_A = HG_HEADS * HG_DK
D_B = 1024
LRU_BLOCKS = 8
LRU_BW = D_B // LRU_BLOCKS
CONV_W = 4
LRU_C = 8.0
MEM_HEADS = 4
MEM_HEAD_DIM = 256
D_C = MEM_HEADS * MEM_HEAD_DIM
N_BRANCH = 3
D_FF = -(-(8 * D_MODEL) // (3 * 256)) * 256
EPS = 1e-6
SPLIT_SIZES = (D_A, D_A, D_A, D_A, D_B, D_B, D_C, N_BRANCH * D_MODEL)
N_IN = D_A * 4 + D_B * 2 + D_C + N_BRANCH * D_MODEL

kernel_name = 'hybrid_hgrn2_rglru_memxattn_stream_step'


def _split_points():
    pts, acc = [], 0
    for s in SPLIT_SIZES[:-1]:
        acc += s
        pts.append(acc)
    return pts


def rmsnorm(x, g):
    xf = x.astype(jnp.float32)
    y = xf * lax.rsqrt(jnp.mean(xf * xf, axis=-1, keepdims=True) + EPS)
    return (y * g.astype(jnp.float32)).astype(x.dtype)


def hgrn2_chunkwise(q, log_f, k, v, s0):
    bsz, L, H, dk = q.shape
    dv = v.shape[-1]
    c = min(CHUNK, L)
    n = L // c

    def to_blocks(t):
        return t.reshape(bsz, n, c, H, t.shape[-1]).transpose(1, 0, 3, 2, 4)

    mask = jnp.tril(jnp.ones((c, c), dtype=bool))[None, None, :, :, None]

    def step(S, inp):
        qc, lfc, kc, vc = inp
        b = jnp.cumsum(lfc, axis=2)
        b_last = b[:, :, -1:, :]
        inter = jnp.einsum('bhtk,bhkv->bhtv', qc * jnp.exp(b), S)
        diff = jnp.where(mask, b[:, :, :, None, :] - b[:, :, None, :, :], -jnp.inf)
        attn = jnp.einsum('bhtk,bhsk,bhtsk->bhts', qc, kc, jnp.exp(diff))
        o = inter + jnp.einsum('bhts,bhsv->bhtv', attn, vc)
        S_new = jnp.exp(b_last[:, :, 0, :])[..., None] * S + jnp.einsum(
            'bhsk,bhsv->bhkv', kc * jnp.exp(b_last - b), vc)
        return S_new, o

    S_fin, o = lax.scan(step, s0, (to_blocks(q), to_blocks(log_f), to_blocks(k), to_blocks(v)))
    o = o.transpose(1, 0, 3, 2, 4).reshape(bsz, L, H, dv)
    return o, S_fin


def causal_conv(x, buf, w, b):
    L = x.shape[1]
    xp = jnp.concatenate([buf.astype(x.dtype), x], axis=1)
    y = xp[:, 0:L] * w[0]
    for j in range(1, CONV_W):
        y = y + xp[:, j:j + L] * w[j]
    return y + b, xp[:, xp.shape[1] - (CONV_W - 1):]


def linear_scan(a, bx, h0):
    bx = bx.at[:, 0].add(a[:, 0] * h0)

    def comb(l, r):
        al, bl = l
        ar, br = r
        return al * ar, ar * bl + br

    _, h = lax.associative_scan(comb, (a, bx), axis=1)
    return h, h[:, -1]


def memory_kv(mem, g, w_kv):
    hm = rmsnorm(mem, g)
    kv = hm @ w_kv
    k, v = jnp.split(kv, 2, axis=-1)
    bsz, m = mem.shape[:2]
    return (k.reshape(bsz, m, MEM_HEADS, MEM_HEAD_DIM), v.reshape(bsz, m, MEM_HEADS, MEM_HEAD_DIM))


def trunk_layer(x, mem_k, mem_v, s_hg, h_lru, conv_buf, lb,
                norm_pre_mix, w_in, b_gate, hgrn_out_norm, conv_w, conv_b,
                lru_wa, lru_ba, lru_wx, lru_bx, lru_lambda,
                w_branch_a, w_branch_b, w_branch_c, w_out, norm_post_mix,
                norm_pre_ffn, ffn_w_gu, ffn_w_down, norm_post_ffn):
    f32 = jnp.float32
    bsz, L = x.shape[:2]
    h = rmsnorm(x, norm_pre_mix)
    proj = h @ w_in
    hq, hf, hi, hg, rx, ry, cq, gates = jnp.split(proj, _split_points(), axis=-1)

    q = jax.nn.silu(hq.astype(f32)).reshape(bsz, L, HG_HEADS, HG_DK)
    f = lb + (1.0 - lb) * jax.nn.sigmoid(hf.astype(f32))
    log_f = jnp.log(f).reshape(bsz, L, HG_HEADS, HG_DK)
    k = (1.0 - f).reshape(bsz, L, HG_HEADS, HG_DK)
    v = hi.astype(f32).reshape(bsz, L, HG_HEADS, HG_DV)
    o_a, s_hg_new = hgrn2_chunkwise(q, log_f, k, v, s_hg.astype(f32))
    o_a = o_a * lax.rsqrt(jnp.mean(o_a * o_a, axis=-1, keepdims=True) + EPS)
    o_a = o_a.reshape(bsz, L, D_A) * hgrn_out_norm.astype(f32) * jax.nn.silu(hg.astype(f32))
    o_a = o_a.astype(x.dtype)

    xc, conv_new = causal_conv(rx, conv_buf, conv_w, conv_b)
    xb = xc.reshape(bsz, L, LRU_BLOCKS, LRU_BW)
    r = jax.nn.sigmoid(jnp.einsum('blnc,ncd->blnd', xb, lru_wa).reshape(bsz, L, D_B) + lru_ba)
    ig = jax.nn.sigmoid(jnp.einsum('blnc,ncd->blnd', xb, lru_wx).reshape(bsz, L, D_B) + lru_bx)
    log_a = -LRU_C * r.astype(f32) * jax.nn.softplus(-lru_lambda.astype(f32))
    a = jnp.exp(log_a)
    mult = jnp.sqrt(jnp.maximum(-jnp.expm1(2.0 * log_a), 0.0))
    h_seq, h_last = linear_scan(a, mult * (ig * xc).astype(f32), h_lru.astype(f32))
    o_b = h_seq.astype(x.dtype) * jax.nn.gelu(ry)

    qm = cq.reshape(bsz, L, MEM_HEADS, MEM_HEAD_DIM).astype(f32)
    scores = jnp.einsum('blhd,bmhd->bhlm', qm, mem_k.astype(f32)) * (1.0 / math.sqrt(MEM_HEAD_DIM))
    probs = jax.nn.softmax(scores, axis=-1)
    o_c = jnp.einsum('bhlm,bmhd->blhd', probs, mem_v.astype(f32)).reshape(bsz, L, D_C).astype(x.dtype)

    g = jax.nn.sigmoid(gates + b_gate).reshape(bsz, L, N_BRANCH, D_MODEL)
    m = (g[:, :, 0] * (o_a @ w_branch_a) + g[:, :, 1] * (o_b @ w_branch_b)
         + g[:, :, 2] * (o_c @ w_branch_c))
    x = x + rmsnorm(m @ w_out, norm_post_mix)

    h2 = rmsnorm(x, norm_pre_ffn)
    gt, up = jnp.split(h2 @ ffn_w_gu, 2, axis=-1)
    x = x + rmsnorm((jax.nn.silu(gt) * up) @ ffn_w_down, norm_post_ffn)
    return x, s_hg_new, h_last, conv_new


def setup_inputs(seed: int = 0) -> dict:
    key = jax.random.key(seed)
    ks = iter(jax.random.split(key, 40))

    def nrm(shape, scale):
        return jax.random.normal(next(ks), shape, jnp.float32) * scale

    def gain(shape):
        return 1.0 + nrm(shape, 0.05)

    u = jax.random.uniform(next(ks), (DEPTH, D_B), jnp.float32, minval=0.9, maxval=0.999)
    s = u ** (1.0 / LRU_C)
    lam = jnp.log(s) - jnp.log1p(-s)
    inp = {
        'x_prompt': nrm((BATCH, SEQ, D_MODEL), 1.0),
        'x_sample': nrm((DEC_BATCH, DEC_SEQ, D_MODEL), 1.0),
        'state_hgrn': nrm((DEPTH, DEC_BATCH, HG_HEADS, HG_DK, HG_DV), 0.3),
        'state_lru': nrm((DEPTH, DEC_BATCH, D_B), 0.5),
        'state_conv': nrm((DEPTH, DEC_BATCH, CONV_W - 1, D_B), 1.0),
        'cache_mem_k': nrm((DEPTH, DEC_BATCH, N_MEM, MEM_HEADS, MEM_HEAD_DIM), 1.0),
        'cache_mem_v': nrm((DEPTH, DEC_BATCH, N_MEM, MEM_HEADS, MEM_HEAD_DIM), 1.0),
        'mem_prompt': nrm((BATCH, N_MEM, D_MODEL), 1.0),
        'norm_mem': gain((DEPTH, D_MODEL)),
        'mem_w_kv': nrm((DEPTH, D_MODEL, 2 * D_C), D_MODEL ** -0.5),
        'hgrn_lower_bound': nrm((DEPTH, D_A), 0.5),
        'norm_pre_mix': gain((DEPTH, D_MODEL)),
        'w_in': nrm((DEPTH, D_MODEL, N_IN), D_MODEL ** -0.5),
        'b_gate': nrm((DEPTH, N_BRANCH * D_MODEL), 0.02),
        'hgrn_out_norm': gain((DEPTH, D_A)),
        'conv_w': nrm((DEPTH, CONV_W, D_B), CONV_W ** -0.5),
        'conv_b': nrm((DEPTH, D_B), 0.02),
        'lru_wa': nrm((DEPTH, LRU_BLOCKS, LRU_BW, LRU_BW), LRU_BW ** -0.5),
        'lru_ba': nrm((DEPTH, D_B), 0.02),
        'lru_wx': nrm((DEPTH, LRU_BLOCKS, LRU_BW, LRU_BW), LRU_BW ** -0.5),
        'lru_bx': nrm((DEPTH, D_B), 0.02),
        'lru_lambda': lam,
        'w_branch_a': nrm((DEPTH, D_A, D_MODEL), D_A ** -0.5),
        'w_branch_b': nrm((DEPTH, D_B, D_MODEL), D_B ** -0.5),
        'w_branch_c': nrm((DEPTH, D_C, D_MODEL), D_C ** -0.5),
        'w_out': nrm((DEPTH, D_MODEL, D_MODEL), D_MODEL ** -0.5),
        'norm_post_mix': gain((DEPTH, D_MODEL)),
        'norm_pre_ffn': gain((DEPTH, D_MODEL)),
        'ffn_w_gu': nrm((DEPTH, D_MODEL, 2 * D_FF), D_MODEL ** -0.5),
        'ffn_w_down': nrm((DEPTH, D_FF, D_MODEL), D_FF ** -0.5),
        'norm_post_ffn': gain((DEPTH, D_MODEL)),
    }
    return inp


def reference(x_prompt, x_sample, state_hgrn, state_lru, state_conv, cache_mem_k, cache_mem_v,
              mem_prompt, norm_mem, mem_w_kv, hgrn_lower_bound, norm_pre_mix, w_in, b_gate,
              hgrn_out_norm, conv_w, conv_b, lru_wa, lru_ba, lru_wx, lru_bx, lru_lambda,
              w_branch_a, w_branch_b, w_branch_c, w_out, norm_post_mix, norm_pre_ffn,
              ffn_w_gu, ffn_w_down, norm_post_ffn):
    f32 = jnp.float32
    sm = jax.nn.softmax(hgrn_lower_bound.astype(f32), axis=0)
    lbs = jnp.cumsum(sm, axis=0) - sm[0:1]

    bp = x_prompt.shape[0]
    xp, xs = x_prompt, x_sample
    hg_p, lru_p, conv_p, mk_p, mv_p = [], [], [], [], []
    hg_s, lru_s, conv_s = [], [], []
    for l in range(DEPTH):
        lp = (norm_pre_mix[l], w_in[l], b_gate[l], hgrn_out_norm[l], conv_w[l], conv_b[l],
              lru_wa[l], lru_ba[l], lru_wx[l], lru_bx[l], lru_lambda[l],
              w_branch_a[l], w_branch_b[l], w_branch_c[l], w_out[l], norm_post_mix[l],
              norm_pre_ffn[l], ffn_w_gu[l], ffn_w_down[l], norm_post_ffn[l])
        mk, mv = memory_kv(mem_prompt, norm_mem[l], mem_w_kv[l])
        xp, s1, h1, c1 = trunk_layer(
            xp, mk, mv,
            jnp.zeros((bp, HG_HEADS, HG_DK, HG_DV), f32),
            jnp.zeros((bp, D_B), f32),
            jnp.zeros((bp, CONV_W - 1, D_B), xp.dtype),
            lbs[l], *lp)
        hg_p.append(s1); lru_p.append(h1); conv_p.append(c1); mk_p.append(mk); mv_p.append(mv)
        xs, s2, h2, c2 = trunk_layer(
            xs, cache_mem_k[l], cache_mem_v[l], state_hgrn[l], state_lru[l], state_conv[l],
            lbs[l], *lp)
        hg_s.append(s2); lru_s.append(h2); conv_s.append(c2)

    new_state_hgrn_p = jnp.stack(hg_p)
    new_state_lru_p = jnp.stack(lru_p)
    new_state_conv_p = jnp.stack(conv_p)
    new_cache_mem_k_p = jnp.stack(mk_p)
    new_cache_mem_v_p = jnp.stack(mv_p)
    new_state_hgrn_s = jnp.stack(hg_s)
    new_state_lru_s = jnp.stack(lru_s)
    new_state_conv_s = jnp.stack(conv_s)
    return (xp, xs, new_state_hgrn_p, new_state_lru_p, new_state_conv_p, new_cache_mem_k_p,
            new_cache_mem_v_p, new_state_hgrn_s, new_state_lru_s, new_state_conv_s)
```

```python
import functools
import math

import jax
import jax.numpy as jnp
from jax import lax
from jax.experimental import pallas as pl
from jax.experimental.pallas import tpu as pltpu

F32 = jnp.float32
BF16 = jnp.bfloat16

EPS = 1e-6
LRU_C = 8.0
CHUNK = 64
SUB = 16
CONV_W = 4
CONV_PAD = 8

V7X_VMEM_BYTES = 64 * 1024 * 1024
VMEM_LIMIT_CAP = 56 * 1024 * 1024


def _vmem_limit(estimate_bytes):
    return int(min(VMEM_LIMIT_CAP, max(16 * 1024 * 1024, estimate_bytes * 5 // 4)))


def _params(sem, vmem_estimate):
    return pltpu.CompilerParams(dimension_semantics=sem, vmem_limit_bytes=_vmem_limit(vmem_estimate))


def _rms(x, g):
    ms = jnp.mean(x * x, axis=-1, keepdims=True)
    return x * lax.rsqrt(ms + EPS) * g


def _dot(a, b):
    return jnp.dot(a, b, preferred_element_type=F32)


def _dot_nt(a, b):
    return lax.dot_general(a, b, (((1,), (1,)), ((), ())), preferred_element_type=F32)


def _dot_tn(a, b):
    return lax.dot_general(a, b, (((0,), (0,)), ((), ())), preferred_element_type=F32)


def _norm_matmul_kernel(x_ref, g_ref, w_ref, o_ref, h_ref):
    @pl.when(pl.program_id(1) == 0)
    def _():
        h_ref[...] = _rms(x_ref[...], g_ref[...]).astype(BF16)

    o_ref[...] = _dot(h_ref[...], w_ref[...]).astype(o_ref.dtype)


def norm_matmul(x, g, w, *, tm, tn, out_dtype=F32):
    t, d = x.shape
    n = w.shape[1]
    assert t % tm == 0 and n % tn == 0
    est = 2 * tm * d * 4 + 2 * d * tn * 2 + 2 * tm * tn * 4 + tm * d * 2
    return pl.pallas_call(
        _norm_matmul_kernel,
        grid=(t // tm, n // tn),
        in_specs=[
            pl.BlockSpec((tm, d), lambda i, j: (i, 0)),
            pl.BlockSpec((1, d), lambda i, j: (0, 0)),
            pl.BlockSpec((d, tn), lambda i, j: (0, j)),
        ],
        out_specs=pl.BlockSpec((tm, tn), lambda i, j: (i, j)),
        out_shape=jax.ShapeDtypeStruct((t, n), out_dtype),
        scratch_shapes=[pltpu.VMEM((tm, d), BF16)],
        compiler_params=_params(("parallel", "arbitrary"), est),
        name="norm_matmul",
    )(x, g, w)


def _hgrn_kernel(hq_ref, hf_ref, hi_ref, hg_ref, lb_ref, gn_ref, s0_ref,
                 o_ref, sfin_ref, st_ref, b_ref, k_ref, *, chunk, n_chunks):
    l = pl.program_id(2)
    n_sub = chunk // SUB

    @pl.when(l == 0)
    def _():
        st_ref[...] = s0_ref[0, 0].T

    lb = lb_ref[...]
    gn = gn_ref[...]
    row = lax.broadcasted_iota(jnp.int32, (chunk, chunk), 0)
    col = lax.broadcasted_iota(jnp.int32, (chunk, chunk), 1)
    tril_incl = (col <= row)
    tril_bf = tril_incl.astype(F32).astype(BF16)
    row_blk = row // SUB
    col_blk = col // SUB
    lane_sub = lax.broadcasted_iota(jnp.int32, (SUB, chunk), 1)

    def chunk_body(c, carry):
        r0 = pl.multiple_of(c * chunk, chunk)
        rows = pl.ds(r0, chunk)
        q = jax.nn.silu(hq_ref[rows, :])
        f = lb + (1.0 - lb) * jax.nn.sigmoid(hf_ref[rows, :])
        lf = jnp.log(f)
        kk = 1.0 - f
        v = hi_ref[rows, :]
        vb = v.astype(BF16)

        p0 = lf.astype(BF16)
        r1 = lf - p0.astype(F32)
        p1 = r1.astype(BF16)
        p2 = (r1 - p1.astype(F32)).astype(BF16)
        b = _dot(tril_bf, p0) + _dot(tril_bf, p1) + _dot(tril_bf, p2)
        b_ref[...] = b
        k_ref[...] = kk
        b_last = b[chunk - 1:chunk, :]

        st = st_ref[...]
        inter = _dot_nt((q * jnp.exp(b)).astype(BF16), st.astype(BF16))

        b_end = jnp.concatenate(
            [jnp.broadcast_to(b[(j + 1) * SUB - 1:(j + 1) * SUB, :], (SUB, b.shape[1]))
             for j in range(n_sub)], axis=0)
        k_dec = (kk * jnp.exp(b_end - b)).astype(BF16)
        attn = jnp.zeros((chunk, chunk), F32)
        for j in range(n_sub - 1):
            ref_j = b[(j + 1) * SUB - 1:(j + 1) * SUB, :]
            q_dec = (q * jnp.exp(jnp.minimum(b - ref_j, 0.0))).astype(BF16)
            m_j = _dot_nt(q_dec, k_dec)
            attn = jnp.where((col_blk == j) & (row_blk > j), m_j, attn)

        diag_rows = []
        for j in range(n_sub):
            blk = slice(j * SUB, (j + 1) * SUB)
            q_j = q[blk, :]
            b_j = b[blk, :]
            a_j = jnp.zeros((SUB, chunk), F32)
            for u in range(SUB):
                s = j * SUB + u
                b_s = b_ref[pl.ds(s, 1), :]
                k_s = k_ref[pl.ds(s, 1), :]
                w = q_j * jnp.exp(b_j - b_s) * k_s
                c_s = jnp.sum(w, axis=-1, keepdims=True)
                a_j = jnp.where(lane_sub == s, c_s, a_j)
            diag_rows.append(a_j)
        attn_diag = jnp.concatenate(diag_rows, axis=0)
        attn = jnp.where((col_blk == row_blk) & tril_incl, attn_diag, attn)

        o = inter + _dot(attn.astype(BF16), vb)

        k_end = (kk * jnp.exp(b_last - b)).astype(BF16)
        st_ref[...] = st * jnp.exp(b_last) + _dot_tn(vb, k_end)

        ms = jnp.mean(o * o, axis=-1, keepdims=True)
        o = o * lax.rsqrt(ms + EPS) * gn * jax.nn.silu(hg_ref[rows, :])
        o_ref[rows, :] = o.astype(o_ref.dtype)
        return carry

    lax.fori_loop(0, n_chunks, chunk_body, 0)

    @pl.when(l == pl.num_programs(2) - 1)
    def _():
        sfin_ref[0, 0] = st_ref[...].T


def hgrn2(proj, lb, gn, s0, *, bsz, seq, heads, dk, tl):
    chunk = min(CHUNK, seq)
    assert seq % tl == 0 and tl % chunk == 0 and chunk % SUB == 0
    nl = seq // tl
    d_a = heads * dk

    def sec(k):
        return pl.BlockSpec((tl, dk), lambda b, h, l, k=k: (b * nl + l, k * heads + h))

    est = 2 * 4 * tl * dk * 4 + 2 * tl * dk * 2 + 6 * dk * dk * 4 + 2 * chunk * dk * 4
    kern = functools.partial(_hgrn_kernel, chunk=chunk, n_chunks=tl // chunk)
    return pl.pallas_call(
        kern,
        grid=(bsz, heads, nl),
        in_specs=[
            sec(0), sec(1), sec(2), sec(3),
            pl.BlockSpec((1, dk), lambda b, h, l: (0, h)),
            pl.BlockSpec((1, dk), lambda b, h, l: (0, h)),
            pl.BlockSpec((1, 1, dk, dk), lambda b, h, l: (b, h, 0, 0)),
        ],
        out_specs=[
            pl.BlockSpec((tl, dk), lambda b, h, l: (b * nl + l, h)),
            pl.BlockSpec((1, 1, dk, dk), lambda b, h, l: (b, h, 0, 0)),
        ],
        out_shape=[
            jax.ShapeDtypeStruct((bsz * seq, d_a), BF16),
            jax.ShapeDtypeStruct((bsz, heads, dk, dk), F32),
        ],
        scratch_shapes=[
            pltpu.VMEM((dk, dk), F32),
            pltpu.VMEM((chunk, dk), F32),
            pltpu.VMEM((chunk, dk), F32),
        ],
        compiler_params=_params(("parallel", "parallel", "arbitrary"), est),
        name="hgrn2",
    )(proj, proj, proj, proj, lb, gn, s0)


def _lru_kernel(rx_ref, ry_ref, cw_ref, cb_ref, wax_ref, ba_ref, bx_ref, lam_ref, h0_ref, buf_ref,
                o_ref, hlast_ref, xp_ref, a_ref, u_ref, h_ref, *, tl, n_blocks, bw):
    l = pl.program_id(1)
    keep = CONV_W - 1

    @pl.when(l == 0)
    def _():
        xp_ref[CONV_PAD - keep:CONV_PAD, :] = buf_ref[0]
        h_ref[...] = h0_ref[0]

    x = rx_ref[...]
    xp_ref[CONV_PAD:CONV_PAD + tl, :] = x
    cw = cw_ref[...]
    xc = xp_ref[CONV_PAD - keep:CONV_PAD - keep + tl, :] * cw[0:1, :]
    for j in range(1, CONV_W):
        xc = xc + xp_ref[CONV_PAD - keep + j:CONV_PAD - keep + j + tl, :] * cw[j:j + 1, :]
    xc = xc + cb_ref[...]
    xp_ref[CONV_PAD - keep:CONV_PAD, :] = xp_ref[CONV_PAD + tl - keep:CONV_PAD + tl, :]

    xcb = xc.astype(BF16)
    pre = [_dot(xcb[:, n * bw:(n + 1) * bw], wax_ref[n]) for n in range(n_blocks)]
    r = jax.nn.sigmoid(jnp.concatenate([pn[:, :bw] for pn in pre], axis=-1) + ba_ref[...])
    ig = jax.nn.sigmoid(jnp.concatenate([pn[:, bw:] for pn in pre], axis=-1) + bx_ref[...])
    lam = lam_ref[...]
    softplus_neg = jnp.maximum(-lam, 0.0) + jnp.log1p(jnp.exp(-jnp.abs(lam)))
    log_a = -LRU_C * r * softplus_neg
    a = jnp.exp(log_a)
    mult = jnp.sqrt(jnp.maximum(-jnp.tanh(log_a) * (a * a + 1.0), 0.0))
    a_ref[...] = a
    u_ref[...] = mult * (ig * xc)

    def step(t, h):
        h = a_ref[pl.ds(t, 1), :] * h + u_ref[pl.ds(t, 1), :]
        u_ref[pl.ds(t, 1), :] = h
        return h

    h = lax.fori_loop(0, tl, step, h_ref[...], unroll=8)
    h_ref[...] = h
    o_ref[...] = (u_ref[...] * jax.nn.gelu(ry_ref[...])).astype(o_ref.dtype)

    @pl.when(l == pl.num_programs(1) - 1)
    def _():
        hlast_ref[0] = h


def conv_lru(proj, cw, cb, wax, ba, bx, lam, h0, buf, *, bsz, seq, d, col_rx, col_ry, tl):
    assert seq % tl == 0 and tl >= CONV_W - 1
    nl = seq // tl
    n_blocks, bw = wax.shape[0], wax.shape[1]
    vec = pl.BlockSpec((1, d), lambda b, l: (0, 0))
    est = 2 * 2 * tl * d * 4 + 2 * tl * d * 2 + (3 * tl + CONV_PAD) * d * 4 + 8 * tl * d * 4
    kern = functools.partial(_lru_kernel, tl=tl, n_blocks=n_blocks, bw=bw)
    return pl.pallas_call(
        kern,
        grid=(bsz, nl),
        in_specs=[
            pl.BlockSpec((tl, d), lambda b, l: (b * nl + l, col_rx)),
            pl.BlockSpec((tl, d), lambda b, l: (b * nl + l, col_ry)),
            pl.BlockSpec((CONV_W, d), lambda b, l: (0, 0)),
            vec,
            pl.BlockSpec((n_blocks, bw, 2 * bw), lambda b, l: (0, 0, 0)),
            vec, vec, vec,
            pl.BlockSpec((1, 1, d), lambda b, l: (b, 0, 0)),
            pl.BlockSpec((1, CONV_W - 1, d), lambda b, l: (b, 0, 0)),
        ],
        out_specs=[
            pl.BlockSpec((tl, d), lambda b, l: (b * nl + l, 0)),
            pl.BlockSpec((1, 1, d), lambda b, l: (b, 0, 0)),
        ],
        out_shape=[
            jax.ShapeDtypeStruct((bsz * seq, d), BF16),
            jax.ShapeDtypeStruct((bsz, 1, d), F32),
        ],
        scratch_shapes=[
            pltpu.VMEM((CONV_PAD + tl, d), F32),
            pltpu.VMEM((tl, d), F32),
            pltpu.VMEM((tl, d), F32),
            pltpu.VMEM((1, d), F32),
        ],
        compiler_params=_params(("parallel", "arbitrary"), est),
        name="conv_lru",
    )(proj, proj, cw, cb, wax, ba, bx, lam, h0, buf)


def _mem_attn_kernel(q_ref, k_ref, v_ref, o_ref, *, scale):
    q = q_ref[...].astype(BF16)
    k = k_ref[0].astype(BF16)
    v = v_ref[0].astype(BF16)
    s = _dot_nt(q, k) * scale
    s = s - jnp.max(s, axis=-1, keepdims=True)
    p = jnp.exp(s)
    denom = jnp.sum(p, axis=-1, keepdims=True)
    o_ref[...] = (_dot(p.astype(BF16), v) / denom).astype(o_ref.dtype)


def mem_attn(proj, mem_k, mem_v, *, bsz, seq, heads, hd, col0, tl):
    assert seq % tl == 0
    nl = seq // tl
    n_mem = mem_k.shape[1]
    est = 2 * tl * hd * 4 + 4 * n_mem * hd * 4 + 2 * tl * hd * 2 + 4 * tl * n_mem * 4
    kern = functools.partial(_mem_attn_kernel, scale=1.0 / math.sqrt(hd))
    return pl.pallas_call(
        kern,
        grid=(bsz, nl, heads),
        in_specs=[
            pl.BlockSpec((tl, hd), lambda b, l, h: (b * nl + l, col0 + h)),
            pl.BlockSpec((1, n_mem, hd), lambda b, l, h: (b, 0, h)),
            pl.BlockSpec((1, n_mem, hd), lambda b, l, h: (b, 0, h)),
        ],
        out_specs=pl.BlockSpec((tl, hd), lambda b, l, h: (b * nl + l, h)),
        out_shape=jax.ShapeDtypeStruct((bsz * seq, heads * hd), BF16),
        compiler_params=_params(("parallel", "parallel", "parallel"), est),
        name="mem_attn",
    )(proj, mem_k, mem_v)


def _merge_kernel(x_ref, oa_ref, ob_ref, oc_ref, g0a, g0b, g1a, g1b, g2a, g2b, bg_ref,
                  wa_ref, wb_ref, wc_ref, wo_ref, gn_ref, y_ref, *, d):
    half = d // 2
    bg = bg_ref[...]

    def gate(lo_ref, hi_ref, n):
        pre = jnp.concatenate([lo_ref[...], hi_ref[...]], axis=-1)
        return jax.nn.sigmoid(pre + bg[:, n * d:(n + 1) * d])

    m = gate(g0a, g0b, 0) * _dot(oa_ref[...], wa_ref[...])
    m = m + gate(g1a, g1b, 1) * _dot(ob_ref[...], wb_ref[...])
    m = m + gate(g2a, g2b, 2) * _dot(oc_ref[...], wc_ref[...])
    z = _dot(m.astype(BF16), wo_ref[...])
    y_ref[...] = x_ref[...] + _rms(z, gn_ref[...])


def merge(x, oa, ob, oc, proj, bg, wa, wb, wc, wo, gn, *, col_gates, tm):
    t, d = x.shape
    db = oa.shape[1]
    assert t % tm == 0 and d == 2 * db
    row = lambda i: (i, 0)
    const = lambda i: (0, 0)
    single = dict(pipeline_mode=pl.Buffered(1))
    gate_specs = [pl.BlockSpec((tm, db), lambda i, k=k: (i, col_gates + k)) for k in range(6)]
    est = (4 * tm * d * 4 + 6 * tm * db * 2 + 12 * tm * db * 4
           + 3 * db * d * 2 + d * d * 2 + 6 * tm * d * 4)
    return pl.pallas_call(
        functools.partial(_merge_kernel, d=d),
        grid=(t // tm,),
        in_specs=[
            pl.BlockSpec((tm, d), row),
            pl.BlockSpec((tm, db), row), pl.BlockSpec((tm, db), row), pl.BlockSpec((tm, db), row),
            *gate_specs,
            pl.BlockSpec((1, 3 * d), const),
            pl.BlockSpec((db, d), const, **single),
            pl.BlockSpec((db, d), const, **single),
            pl.BlockSpec((db, d), const, **single),
            pl.BlockSpec((d, d), const, **single),
            pl.BlockSpec((1, d), const),
        ],
        out_specs=pl.BlockSpec((tm, d), row),
        out_shape=jax.ShapeDtypeStruct((t, d), F32),
        compiler_params=_params(("parallel",), est),
        name="merge",
    )(x, oa, ob, oc, proj, proj, proj, proj, proj, proj, bg, wa, wb, wc, wo, gn)


def _ffn_kernel(x_ref, gpre_ref, wg_ref, wu_ref, wd_ref, gpost_ref, y_ref, h_ref, acc_ref):
    j = pl.program_id(1)

    @pl.when(j == 0)
    def _():
        h_ref[...] = _rms(x_ref[...], gpre_ref[...]).astype(BF16)
        acc_ref[...] = jnp.zeros_like(acc_ref)

    h = h_ref[...]
    gt = _dot(h, wg_ref[...])
    up = _dot(h, wu_ref[...])
    act = (jax.nn.silu(gt) * up).astype(BF16)
    acc_ref[...] += _dot(act, wd_ref[...])

    @pl.when(j == pl.num_programs(1) - 1)
    def _():
        y_ref[...] = x_ref[...] + _rms(acc_ref[...], gpost_ref[...])


def ffn(x, gpre, w_gu, w_down, gpost, *, tm, tf):
    t, d = x.shape
    d_ff = w_down.shape[0]
    assert t % tm == 0 and d_ff % tf == 0
    nf = d_ff // tf
    est = 4 * tm * d * 4 + 2 * 3 * d * tf * 2 + tm * d * 2 + tm * d * 4 + 4 * tm * tf * 4
    return pl.pallas_call(
        _ffn_kernel,
        grid=(t // tm, nf),
        in_specs=[
            pl.BlockSpec((tm, d), lambda i, j: (i, 0)),
            pl.BlockSpec((1, d), lambda i, j: (0, 0)),
            pl.BlockSpec((d, tf), lambda i, j: (0, j)),
            pl.BlockSpec((d, tf), lambda i, j: (0, nf + j)),
            pl.BlockSpec((tf, d), lambda i, j: (j, 0)),
            pl.BlockSpec((1, d), lambda i, j: (0, 0)),
        ],
        out_specs=pl.BlockSpec((tm, d), lambda i, j: (i, 0)),
        out_shape=jax.ShapeDtypeStruct((t, d), F32),
        scratch_shapes=[pltpu.VMEM((tm, d), BF16), pltpu.VMEM((tm, d), F32)],
        compiler_params=_params(("parallel", "arbitrary"), est),
        name="ffn",
    )(x, gpre, w_gu, w_gu, w_down, gpost)


def _row_tile(n, target):
    t = min(n, target)
    while n % t:
        t //= 2
    return t


def _trunk_layer(x2, bsz, seq, mem_k, mem_v, s_hg, h_lru, conv_buf, lb, p):
    t, d = x2.shape
    heads, dk = s_hg.shape[1], s_hg.shape[2]
    d_a = heads * dk
    d_b = h_lru.shape[-1]
    mem_heads, hd = p["mem_heads"], p["mem_hd"]
    d_c = mem_heads * hd

    proj = norm_matmul(x2, p["norm_pre_mix"], p["w_in"], tm=_row_tile(t, 512), tn=1024)

    o_a, s_new = hgrn2(proj, lb, p["hgrn_out_norm"], s_hg, bsz=bsz, seq=seq, heads=heads, dk=dk,
                       tl=_row_tile(seq, 512))
    o_b, h_last = conv_lru(proj, p["conv_w"], p["conv_b"], p["lru_wax"], p["lru_ba"], p["lru_bx"],
                           p["lru_lambda"], h_lru.reshape(bsz, 1, d_b), conv_buf,
                           bsz=bsz, seq=seq, d=d_b, col_rx=4 * d_a // d_b, col_ry=4 * d_a // d_b + 1,
                           tl=_row_tile(seq, 256))
    o_c = mem_attn(proj, mem_k, mem_v, bsz=bsz, seq=seq, heads=mem_heads, hd=hd,
                   col0=(4 * d_a + 2 * d_b) // hd, tl=_row_tile(seq, 512))

    x2 = merge(x2, o_a, o_b, o_c, proj, p["b_gate"], p["w_branch_a"], p["w_branch_b"], p["w_branch_c"],
               p["w_out"], p["norm_post_mix"], col_gates=(4 * d_a + 2 * d_b + d_c) // d_a,
               tm=_row_tile(t, 256))
    x2 = ffn(x2, p["norm_pre_ffn"], p["ffn_w_gu"], p["ffn_w_down"], p["norm_post_ffn"],
             tm=_row_tile(t, 512), tf=512)

    col_rx0 = 4 * d_a
    rx_tail = proj.reshape(bsz, seq, -1)[:, seq - (CONV_W - 1):, col_rx0:col_rx0 + d_b]
    if seq < CONV_W - 1:
        raise NotImplementedError("sequence shorter than the conv history")
    return x2, s_new, h_last.reshape(bsz, d_b), rx_tail


def kernel(x_prompt, x_sample, state_hgrn, state_lru, state_conv, cache_mem_k, cache_mem_v, mem_prompt, norm_mem, mem_w_kv, hgrn_lower_bound, norm_pre_mix, w_in, b_gate, hgrn_out_norm, conv_w, conv_b, lru_wa, lru_ba, lru_wx, lru_bx, lru_lambda, w_branch_a, w_branch_b, w_branch_c, w_out, norm_post_mix, norm_pre_ffn, ffn_w_gu, ffn_w_down, norm_post_ffn):
    depth = w_in.shape[0]
    bp, sp, d = x_prompt.shape
    bs, ss, _ = x_sample.shape
    _, _, heads, dk, dv = state_hgrn.shape
    d_b = state_lru.shape[-1]
    n_mem, mem_heads, hd = cache_mem_k.shape[2:]
    d_c = mem_heads * hd
    assert dk == dv and d_b == heads * dk and d_c == d_b and d == 2 * d_b

    sm = jax.nn.softmax(hgrn_lower_bound.astype(F32), axis=0)
    lbs = jnp.cumsum(sm, axis=0) - sm[0:1]

    xp = x_prompt.reshape(bp * sp, d)
    xs = x_sample.reshape(bs * ss, d)
    mem2 = mem_prompt.reshape(bp * n_mem, d)
    zero_hg = jnp.zeros((bp, heads, dk, dv), F32)
    zero_lru = jnp.zeros((bp, d_b), F32)
    zero_conv = jnp.zeros((bp, CONV_W - 1, d_b), F32)

    outs = {k: [] for k in ("hg_p", "lru_p", "conv_p", "mk_p", "mv_p", "hg_s", "lru_s", "conv_s")}
    for l in range(depth):
        row = lambda a: a[l].reshape(1, -1)
        p = dict(
            mem_heads=mem_heads, mem_hd=hd,
            norm_pre_mix=row(norm_pre_mix), w_in=w_in[l].astype(BF16), b_gate=row(b_gate),
            hgrn_out_norm=row(hgrn_out_norm), conv_w=conv_w[l], conv_b=row(conv_b),
            lru_wax=jnp.concatenate([lru_wa[l], lru_wx[l]], axis=-1).astype(BF16),
            lru_ba=row(lru_ba), lru_bx=row(lru_bx), lru_lambda=row(lru_lambda),
            w_branch_a=w_branch_a[l].astype(BF16), w_branch_b=w_branch_b[l].astype(BF16),
            w_branch_c=w_branch_c[l].astype(BF16), w_out=w_out[l].astype(BF16),
            norm_post_mix=row(norm_post_mix), norm_pre_ffn=row(norm_pre_ffn),
            ffn_w_gu=ffn_w_gu[l].astype(BF16), ffn_w_down=ffn_w_down[l].astype(BF16),
            norm_post_ffn=row(norm_post_ffn),
        )
        lb = lbs[l].reshape(1, -1)

        kv = norm_matmul(mem2, row(norm_mem), mem_w_kv[l].astype(BF16),
                         tm=_row_tile(bp * n_mem, 512), tn=1024)
        mk = kv[:, :d_c].reshape(bp, n_mem, d_c)
        mv = kv[:, d_c:].reshape(bp, n_mem, d_c)
        xp, s1, h1, c1 = _trunk_layer(xp, bp, sp, mk, mv, zero_hg, zero_lru, zero_conv, lb, p)
        outs["hg_p"].append(s1); outs["lru_p"].append(h1); outs["conv_p"].append(c1)
        outs["mk_p"].append(mk.reshape(bp, n_mem, mem_heads, hd))
        outs["mv_p"].append(mv.reshape(bp, n_mem, mem_heads, hd))

        xs, s2, h2, c2 = _trunk_layer(
            xs, bs, ss, cache_mem_k[l].reshape(bs, n_mem, d_c), cache_mem_v[l].reshape(bs, n_mem, d_c),
            state_hgrn[l], state_lru[l], state_conv[l], lb, p)
        outs["hg_s"].append(s2); outs["lru_s"].append(h2); outs["conv_s"].append(c2)

    st = {k: jnp.stack(v) for k, v in outs.items()}
    return (xp.reshape(bp, sp, d), xs.reshape(bs, ss, d), st["hg_p"], st["lru_p"], st["conv_p"],
            st["mk_p"], st["mv_p"], st["hg_s"], st["lru_s"], st["conv_s"])
```

```python
import functools
import math

import numpy as np
import jax
import jax.numpy as jnp
from jax import lax
from jax.experimental import pallas as pl
from jax.experimental.pallas import tpu as pltpu

F32 = jnp.float32
BF16 = jnp.bfloat16

EPS = 1e-6
LRU_C = 8.0
CHUNK = 64
SUB = 16
CONV_W = 4
CONV_PAD = 8

V7X_VMEM_BYTES = 64 * 1024 * 1024
VMEM_LIMIT_CAP = 56 * 1024 * 1024


def _vmem_limit(estimate_bytes):
    return int(min(VMEM_LIMIT_CAP, max(16 * 1024 * 1024, estimate_bytes * 5 // 4)))


def _params(sem, vmem_estimate):
    return pltpu.CompilerParams(dimension_semantics=sem, vmem_limit_bytes=_vmem_limit(vmem_estimate))


def _rms(x, g):
    ms = jnp.mean(x * x, axis=-1, keepdims=True)
    return x * lax.rsqrt(ms + EPS) * g


def _dot(a, b):
    return jnp.dot(a, b, preferred_element_type=F32)


def _dot_nt(a, b):
    return lax.dot_general(a, b, (((1,), (1,)), ((), ())), preferred_element_type=F32)


def _dot_tn(a, b):
    return lax.dot_general(a, b, (((0,), (0,)), ((), ())), preferred_element_type=F32)


def _norm_matmul_kernel(x_ref, g_ref, w_ref, o_ref, h_ref):
    @pl.when(pl.program_id(1) == 0)
    def _():
        h_ref[...] = _rms(x_ref[...], g_ref[...]).astype(BF16)

    o_ref[...] = _dot(h_ref[...], w_ref[...]).astype(o_ref.dtype)


def norm_matmul(x, g, w, *, tm, tn, out_dtype=F32):
    t, d = x.shape
    n = w.shape[1]
    assert t % tm == 0 and n % tn == 0
    est = 2 * tm * d * 4 + 2 * d * tn * 2 + 2 * tm * tn * 4 + tm * d * 2
    return pl.pallas_call(
        _norm_matmul_kernel,
        grid=(t // tm, n // tn),
        in_specs=[
            pl.BlockSpec((tm, d), lambda i, j: (i, 0)),
            pl.BlockSpec((1, d), lambda i, j: (0, 0)),
            pl.BlockSpec((d, tn), lambda i, j: (0, j)),
        ],
        out_specs=pl.BlockSpec((tm, tn), lambda i, j: (i, j)),
        out_shape=jax.ShapeDtypeStruct((t, n), out_dtype),
        scratch_shapes=[pltpu.VMEM((tm, d), BF16)],
        compiler_params=_params(("parallel", "arbitrary"), est),
        name="norm_matmul",
    )(x, g, w)


def _hgrn_consts(chunk, dk):
    n_sub = chunk // SUB
    t = np.arange(chunk)[:, None]
    s = np.arange(chunk)[None, :]
    parts = [s <= t, s <= (t // SUB) * SUB + SUB - 1, s <= chunk - 1 + 0 * t]
    parts += [s <= (j + 1) * SUB - 1 + 0 * t for j in range(n_sub - 1)]
    cum = np.concatenate(parts, axis=0).astype(np.float32)
    lane_blk = np.arange(SUB * dk)[:, None] // dk
    sel = (lane_blk == (np.arange(chunk)[None, :] % SUB)).astype(np.float32)
    return jnp.asarray(cum, BF16), jnp.asarray(sel, BF16)


def _hgrn_kernel(hq_ref, hf_ref, hi_ref, hg_ref, lb_ref, gn_ref, s0_ref, cum_ref, sel_ref,
                 o_ref, sfin_ref, st_ref, b_ref, k_ref, u_ref, sb_ref, *, chunk, n_chunks):
    l = pl.program_id(2)
    n_sub = chunk // SUB
    dk = hq_ref.shape[1]

    @pl.when(l == 0)
    def _():
        st_ref[...] = s0_ref[0, 0].T

    lb = lb_ref[...]
    q = jax.nn.silu(hq_ref[...])
    f = lb + (1.0 - lb) * jax.nn.sigmoid(hf_ref[...])
    lf = jnp.log(f)
    kk = 1.0 - f
    vb = hi_ref[...].astype(BF16)
    k_ref[...] = kk

    p0 = lf.astype(BF16)
    r1 = lf - p0.astype(F32)
    p1 = r1.astype(BF16)
    p2 = (r1 - p1.astype(F32)).astype(BF16)
    pieces = jnp.concatenate([p0, p1, p2], axis=1)
    cum = cum_ref[...]
    sums = []
    for c in range(n_chunks):
        r = _dot(cum, pieces[c * chunk:(c + 1) * chunk, :])
        sums.append(r[:, :dk] + r[:, dk:2 * dk] + r[:, 2 * dk:])

    def gather(i):
        return jnp.concatenate([s_[i * chunk:(i + 1) * chunk, :] for s_ in sums], axis=0)

    b = gather(0)
    b_end = gather(1)
    b_last = gather(2)
    b_ref[...] = b

    qe = (q * jnp.exp(b)).astype(BF16)
    k_dec = (kk * jnp.exp(b_end - b)).astype(BF16)
    k_end = (kk * jnp.exp(b_last - b)).astype(BF16)
    decay = jnp.exp(b_last)
    q_dec = [(q * jnp.exp(jnp.minimum(b - gather(3 + j), 0.0))).astype(BF16) for j in range(n_sub - 1)]

    row = lax.broadcasted_iota(jnp.int32, (chunk, chunk), 0)
    col = lax.broadcasted_iota(jnp.int32, (chunk, chunk), 1)
    row_blk = row // SUB
    col_blk = col // SUB
    diag_mask = (col_blk == row_blk) & (col <= row)
    sel = sel_ref[...]

    intra = []
    for c in range(n_chunks):
        rows = slice(c * chunk, (c + 1) * chunk)
        attn = jnp.zeros((chunk, chunk), F32)
        if n_sub > 1:
            qd = jnp.concatenate([qj[rows, :] for qj in q_dec], axis=0)
            m = _dot_nt(qd, k_dec[rows, :])
            for j in range(n_sub - 1):
                attn = jnp.where((col_blk == j) & (row_blk > j), m[j * chunk:(j + 1) * chunk, :], attn)

        q_c = q[rows, :]
        b_c = b[rows, :]
        ws = []
        for u in range(SUB):
            b_s = jnp.concatenate(
                [jnp.broadcast_to(b_ref[pl.ds(c * chunk + j * SUB + u, 1), :], (SUB, dk))
                 for j in range(n_sub)], axis=0)
            k_s = jnp.concatenate(
                [jnp.broadcast_to(k_ref[pl.ds(c * chunk + j * SUB + u, 1), :], (SUB, dk))
                 for j in range(n_sub)], axis=0)
            ws.append((q_c * jnp.exp(jnp.minimum(b_c - b_s, 0.0)) * k_s).astype(BF16))
        d = _dot(jnp.concatenate(ws, axis=1), sel)
        attn = jnp.where(diag_mask, d, attn)

        intra.append(_dot(attn.astype(BF16), vb[rows, :]))
        u_ref[c] = _dot_tn(vb[rows, :], k_end[rows, :])

    st = st_ref[...]
    for c in range(n_chunks):
        sb_ref[c] = st.astype(BF16)
        st = st * decay[c * chunk:c * chunk + 1, :] + u_ref[c]
    st_ref[...] = st

    inter = [_dot_nt(qe[c * chunk:(c + 1) * chunk, :], sb_ref[c]) for c in range(n_chunks)]
    o = jnp.concatenate(intra, axis=0) + jnp.concatenate(inter, axis=0)
    ms = jnp.mean(o * o, axis=-1, keepdims=True)
    o = o * lax.rsqrt(ms + EPS) * gn_ref[...] * jax.nn.silu(hg_ref[...])
    o_ref[...] = o.astype(o_ref.dtype)

    @pl.when(l == pl.num_programs(2) - 1)
    def _():
        sfin_ref[0, 0] = st.T


def hgrn2(proj, lb, gn, s0, *, bsz, seq, heads, dk, tl):
    chunk = min(CHUNK, seq)
    assert seq % tl == 0 and tl % chunk == 0 and chunk % SUB == 0
    nl = seq // tl
    d_a = heads * dk

    def sec(k):
        return pl.BlockSpec((tl, dk), lambda b, h, l, k=k: (b * nl + l, k * heads + h))

    n_chunks = tl // chunk
    cum, sel = _hgrn_consts(chunk, dk)
    est = (2 * 4 * tl * dk * 4 + 2 * tl * dk * 2 + 6 * dk * dk * 4 + 2 * tl * dk * 4
           + n_chunks * dk * dk * 6 + 2 * (cum.size + sel.size) * 2 + 24 * tl * dk * 4)
    kern = functools.partial(_hgrn_kernel, chunk=chunk, n_chunks=n_chunks)
    return pl.pallas_call(
        kern,
        grid=(bsz, heads, nl),
        in_specs=[
            sec(0), sec(1), sec(2), sec(3),
            pl.BlockSpec((1, dk), lambda b, h, l: (0, h)),
            pl.BlockSpec((1, dk), lambda b, h, l: (0, h)),
            pl.BlockSpec((1, 1, dk, dk), lambda b, h, l: (b, h, 0, 0)),
            pl.BlockSpec(cum.shape, lambda b, h, l: (0, 0)),
            pl.BlockSpec(sel.shape, lambda b, h, l: (0, 0)),
        ],
        out_specs=[
            pl.BlockSpec((tl, dk), lambda b, h, l: (b * nl + l, h)),
            pl.BlockSpec((1, 1, dk, dk), lambda b, h, l: (b, h, 0, 0)),
        ],
        out_shape=[
            jax.ShapeDtypeStruct((bsz * seq, d_a), BF16),
            jax.ShapeDtypeStruct((bsz, heads, dk, dk), F32),
        ],
        scratch_shapes=[
            pltpu.VMEM((dk, dk), F32),
            pltpu.VMEM((tl, dk), F32),
            pltpu.VMEM((tl, dk), F32),
            pltpu.VMEM((n_chunks, dk, dk), F32),
            pltpu.VMEM((n_chunks, dk, dk), BF16),
        ],
        compiler_params=_params(("parallel", "parallel", "arbitrary"), est),
        name="hgrn2",
    )(proj, proj, proj, proj, lb, gn, s0, cum, sel)


def _lru_kernel(rx_ref, ry_ref, cw_ref, cb_ref, wax_ref, ba_ref, bx_ref, lam_ref, h0_ref, buf_ref,
                o_ref, hlast_ref, xp_ref, a_ref, u_ref, h_ref, *, tl, n_blocks, bw):
    l = pl.program_id(1)
    keep = CONV_W - 1

    @pl.when(l == 0)
    def _():
        xp_ref[CONV_PAD - keep:CONV_PAD, :] = buf_ref[0]
        h_ref[...] = h0_ref[0]

    x = rx_ref[...]
    xp_ref[CONV_PAD:CONV_PAD + tl, :] = x
    cw = cw_ref[...]
    xc = xp_ref[CONV_PAD - keep:CONV_PAD - keep + tl, :] * cw[0:1, :]
    for j in range(1, CONV_W):
        xc = xc + xp_ref[CONV_PAD - keep + j:CONV_PAD - keep + j + tl, :] * cw[j:j + 1, :]
    xc = xc + cb_ref[...]
    xp_ref[CONV_PAD - keep:CONV_PAD, :] = xp_ref[CONV_PAD + tl - keep:CONV_PAD + tl, :]

    xcb = xc.astype(BF16)
    pre = [_dot(xcb[:, n * bw:(n + 1) * bw], wax_ref[n]) for n in range(n_blocks)]
    r = jax.nn.sigmoid(jnp.concatenate([pn[:, :bw] for pn in pre], axis=-1) + ba_ref[...])
    ig = jax.nn.sigmoid(jnp.concatenate([pn[:, bw:] for pn in pre], axis=-1) + bx_ref[...])
    lam = lam_ref[...]
    softplus_neg = jnp.maximum(-lam, 0.0) + jnp.log1p(jnp.exp(-jnp.abs(lam)))
    log_a = -LRU_C * r * softplus_neg
    a = jnp.exp(log_a)
    mult = jnp.sqrt(jnp.maximum(-jnp.tanh(log_a) * (a * a + 1.0), 0.0))
    a_ref[...] = a
    u_ref[...] = mult * (ig * xc)

    def step(t, h):
        h = a_ref[pl.ds(t, 1), :] * h + u_ref[pl.ds(t, 1), :]
        u_ref[pl.ds(t, 1), :] = h
        return h

    h = lax.fori_loop(0, tl, step, h_ref[...], unroll=8)
    h_ref[...] = h
    o_ref[...] = (u_ref[...] * jax.nn.gelu(ry_ref[...])).astype(o_ref.dtype)

    @pl.when(l == pl.num_programs(1) - 1)
    def _():
        hlast_ref[0] = h


def conv_lru(proj, cw, cb, wax, ba, bx, lam, h0, buf, *, bsz, seq, d, col_rx, col_ry, tl):
    assert seq % tl == 0 and tl >= CONV_W - 1
    nl = seq // tl
    n_blocks, bw = wax.shape[0], wax.shape[1]
    vec = pl.BlockSpec((1, d), lambda b, l: (0, 0))
    est = 2 * 2 * tl * d * 4 + 2 * tl * d * 2 + (3 * tl + CONV_PAD) * d * 4 + 8 * tl * d * 4
    kern = functools.partial(_lru_kernel, tl=tl, n_blocks=n_blocks, bw=bw)
    return pl.pallas_call(
        kern,
        grid=(bsz, nl),
        in_specs=[
            pl.BlockSpec((tl, d), lambda b, l: (b * nl + l, col_rx)),
            pl.BlockSpec((tl, d), lambda b, l: (b * nl + l, col_ry)),
            pl.BlockSpec((CONV_W, d), lambda b, l: (0, 0)),
            vec,
            pl.BlockSpec((n_blocks, bw, 2 * bw), lambda b, l: (0, 0, 0)),
            vec, vec, vec,
            pl.BlockSpec((1, 1, d), lambda b, l: (b, 0, 0)),
            pl.BlockSpec((1, CONV_W - 1, d), lambda b, l: (b, 0, 0)),
        ],
        out_specs=[
            pl.BlockSpec((tl, d), lambda b, l: (b * nl + l, 0)),
            pl.BlockSpec((1, 1, d), lambda b, l: (b, 0, 0)),
        ],
        out_shape=[
            jax.ShapeDtypeStruct((bsz * seq, d), BF16),
            jax.ShapeDtypeStruct((bsz, 1, d), F32),
        ],
        scratch_shapes=[
            pltpu.VMEM((CONV_PAD + tl, d), F32),
            pltpu.VMEM((tl, d), F32),
            pltpu.VMEM((tl, d), F32),
            pltpu.VMEM((1, d), F32),
        ],
        compiler_params=_params(("parallel", "arbitrary"), est),
        name="conv_lru",
    )(proj, proj, cw, cb, wax, ba, bx, lam, h0, buf)


def _mem_attn_kernel(q_ref, k_ref, v_ref, o_ref, *, scale):
    q = q_ref[...].astype(BF16)
    k = k_ref[0].astype(BF16)
    v = v_ref[0].astype(BF16)
    s = _dot_nt(q, k) * scale
    s = s - jnp.max(s, axis=-1, keepdims=True)
    p = jnp.exp(s)
    denom = jnp.sum(p, axis=-1, keepdims=True)
    o_ref[...] = (_dot(p.astype(BF16), v) / denom).astype(o_ref.dtype)


def mem_attn(proj, mem_k, mem_v, *, bsz, seq, heads, hd, col0, tl):
    assert seq % tl == 0
    nl = seq // tl
    n_mem = mem_k.shape[1]
    est = 2 * tl * hd * 4 + 4 * n_mem * hd * 4 + 2 * tl * hd * 2 + 4 * tl * n_mem * 4
    kern = functools.partial(_mem_attn_kernel, scale=1.0 / math.sqrt(hd))
    return pl.pallas_call(
        kern,
        grid=(bsz, nl, heads),
        in_specs=[
            pl.BlockSpec((tl, hd), lambda b, l, h: (b * nl + l, col0 + h)),
            pl.BlockSpec((1, n_mem, hd), lambda b, l, h: (b, 0, h)),
            pl.BlockSpec((1, n_mem, hd), lambda b, l, h: (b, 0, h)),
        ],
        out_specs=pl.BlockSpec((tl, hd), lambda b, l, h: (b * nl + l, h)),
        out_shape=jax.ShapeDtypeStruct((bsz * seq, heads * hd), BF16),
        compiler_params=_params(("parallel", "parallel", "parallel"), est),
        name="mem_attn",
    )(proj, mem_k, mem_v)


def _merge_kernel(x_ref, oa_ref, ob_ref, oc_ref, g0a, g0b, g1a, g1b, g2a, g2b, bg_ref,
                  wa_ref, wb_ref, wc_ref, wo_ref, gn_ref, y_ref, *, d):
    half = d // 2
    bg = bg_ref[...]

    def gate(lo_ref, hi_ref, n):
        pre = jnp.concatenate([lo_ref[...], hi_ref[...]], axis=-1)
        return jax.nn.sigmoid(pre + bg[:, n * d:(n + 1) * d])

    m = gate(g0a, g0b, 0) * _dot(oa_ref[...], wa_ref[...])
    m = m + gate(g1a, g1b, 1) * _dot(ob_ref[...], wb_ref[...])
    m = m + gate(g2a, g2b, 2) * _dot(oc_ref[...], wc_ref[...])
    z = _dot(m.astype(BF16), wo_ref[...])
    y_ref[...] = x_ref[...] + _rms(z, gn_ref[...])


def merge(x, oa, ob, oc, proj, bg, wa, wb, wc, wo, gn, *, col_gates, tm):
    t, d = x.shape
    db = oa.shape[1]
    assert t % tm == 0 and d == 2 * db
    row = lambda i: (i, 0)
    const = lambda i: (0, 0)
    single = dict(pipeline_mode=pl.Buffered(1))
    gate_specs = [pl.BlockSpec((tm, db), lambda i, k=k: (i, col_gates + k)) for k in range(6)]
    est = (4 * tm * d * 4 + 6 * tm * db * 2 + 12 * tm * db * 4
           + 3 * db * d * 2 + d * d * 2 + 6 * tm * d * 4)
    return pl.pallas_call(
        functools.partial(_merge_kernel, d=d),
        grid=(t // tm,),
        in_specs=[
            pl.BlockSpec((tm, d), row),
            pl.BlockSpec((tm, db), row), pl.BlockSpec((tm, db), row), pl.BlockSpec((tm, db), row),
            *gate_specs,
            pl.BlockSpec((1, 3 * d), const),
            pl.BlockSpec((db, d), const, **single),
            pl.BlockSpec((db, d), const, **single),
            pl.BlockSpec((db, d), const, **single),
            pl.BlockSpec((d, d), const, **single),
            pl.BlockSpec((1, d), const),
        ],
        out_specs=pl.BlockSpec((tm, d), row),
        out_shape=jax.ShapeDtypeStruct((t, d), F32),
        compiler_params=_params(("parallel",), est),
        name="merge",
    )(x, oa, ob, oc, proj, proj, proj, proj, proj, proj, bg, wa, wb, wc, wo, gn)


def _ffn_kernel(x_ref, gpre_ref, wg_ref, wu_ref, wd_ref, gpost_ref, y_ref, h_ref, acc_ref):
    j = pl.program_id(1)

    @pl.when(j == 0)
    def _():
        h_ref[...] = _rms(x_ref[...], gpre_ref[...]).astype(BF16)
        acc_ref[...] = jnp.zeros_like(acc_ref)

    h = h_ref[...]
    gt = _dot(h, wg_ref[...])
    up = _dot(h, wu_ref[...])
    act = (jax.nn.silu(gt) * up).astype(BF16)
    acc_ref[...] += _dot(act, wd_ref[...])

    @pl.when(j == pl.num_programs(1) - 1)
    def _():
        y_ref[...] = x_ref[...] + _rms(acc_ref[...], gpost_ref[...])


def ffn(x, gpre, w_gu, w_down, gpost, *, tm, tf):
    t, d = x.shape
    d_ff = w_down.shape[0]
    assert t % tm == 0 and d_ff % tf == 0
    nf = d_ff // tf
    est = 4 * tm * d * 4 + 2 * 3 * d * tf * 2 + tm * d * 2 + tm * d * 4 + 4 * tm * tf * 4
    return pl.pallas_call(
        _ffn_kernel,
        grid=(t // tm, nf),
        in_specs=[
            pl.BlockSpec((tm, d), lambda i, j: (i, 0)),
            pl.BlockSpec((1, d), lambda i, j: (0, 0)),
            pl.BlockSpec((d, tf), lambda i, j: (0, j)),
            pl.BlockSpec((d, tf), lambda i, j: (0, nf + j)),
            pl.BlockSpec((tf, d), lambda i, j: (j, 0)),
            pl.BlockSpec((1, d), lambda i, j: (0, 0)),
        ],
        out_specs=pl.BlockSpec((tm, d), lambda i, j: (i, 0)),
        out_shape=jax.ShapeDtypeStruct((t, d), F32),
        scratch_shapes=[pltpu.VMEM((tm, d), BF16), pltpu.VMEM((tm, d), F32)],
        compiler_params=_params(("parallel", "arbitrary"), est),
        name="ffn",
    )(x, gpre, w_gu, w_gu, w_down, gpost)


def _row_tile(n, target):
    t = min(n, target)
    while n % t:
        t //= 2
    return t


def _trunk_layer(x2, bsz, seq, mem_k, mem_v, s_hg, h_lru, conv_buf, lb, p):
    t, d = x2.shape
    heads, dk = s_hg.shape[1], s_hg.shape[2]
    d_a = heads * dk
    d_b = h_lru.shape[-1]
    mem_heads, hd = p["mem_heads"], p["mem_hd"]
    d_c = mem_heads * hd

    proj = norm_matmul(x2, p["norm_pre_mix"], p["w_in"], tm=_row_tile(t, 512), tn=1024)

    o_a, s_new = hgrn2(proj, lb, p["hgrn_out_norm"], s_hg, bsz=bsz, seq=seq, heads=heads, dk=dk,
                       tl=_row_tile(seq, 512))
    o_b, h_last = conv_lru(proj, p["conv_w"], p["conv_b"], p["lru_wax"], p["lru_ba"], p["lru_bx"],
                           p["lru_lambda"], h_lru.reshape(bsz, 1, d_b), conv_buf,
                           bsz=bsz, seq=seq, d=d_b, col_rx=4 * d_a // d_b, col_ry=4 * d_a // d_b + 1,
                           tl=_row_tile(seq, 256))
    o_c = mem_attn(proj, mem_k, mem_v, bsz=bsz, seq=seq, heads=mem_heads, hd=hd,
                   col0=(4 * d_a + 2 * d_b) // hd, tl=_row_tile(seq, 512))

    x2 = merge(x2, o_a, o_b, o_c, proj, p["b_gate"], p["w_branch_a"], p["w_branch_b"], p["w_branch_c"],
               p["w_out"], p["norm_post_mix"], col_gates=(4 * d_a + 2 * d_b + d_c) // d_a,
               tm=_row_tile(t, 256))
    x2 = ffn(x2, p["norm_pre_ffn"], p["ffn_w_gu"], p["ffn_w_down"], p["norm_post_ffn"],
             tm=_row_tile(t, 512), tf=512)

    col_rx0 = 4 * d_a
    rx_tail = proj.reshape(bsz, seq, -1)[:, seq - (CONV_W - 1):, col_rx0:col_rx0 + d_b]
    if seq < CONV_W - 1:
        raise NotImplementedError("sequence shorter than the conv history")
    return x2, s_new, h_last.reshape(bsz, d_b), rx_tail


def kernel(x_prompt, x_sample, state_hgrn, state_lru, state_conv, cache_mem_k, cache_mem_v, mem_prompt, norm_mem, mem_w_kv, hgrn_lower_bound, norm_pre_mix, w_in, b_gate, hgrn_out_norm, conv_w, conv_b, lru_wa, lru_ba, lru_wx, lru_bx, lru_lambda, w_branch_a, w_branch_b, w_branch_c, w_out, norm_post_mix, norm_pre_ffn, ffn_w_gu, ffn_w_down, norm_post_ffn):
    depth = w_in.shape[0]
    bp, sp, d = x_prompt.shape
    bs, ss, _ = x_sample.shape
    _, _, heads, dk, dv = state_hgrn.shape
    d_b = state_lru.shape[-1]
    n_mem, mem_heads, hd = cache_mem_k.shape[2:]
    d_c = mem_heads * hd
    assert dk == dv and d_b == heads * dk and d_c == d_b and d == 2 * d_b

    sm = jax.nn.softmax(hgrn_lower_bound.astype(F32), axis=0)
    lbs = jnp.cumsum(sm, axis=0) - sm[0:1]

    xp = x_prompt.reshape(bp * sp, d)
    xs = x_sample.reshape(bs * ss, d)
    mem2 = mem_prompt.reshape(bp * n_mem, d)
    zero_hg = jnp.zeros((bp, heads, dk, dv), F32)
    zero_lru = jnp.zeros((bp, d_b), F32)
    zero_conv = jnp.zeros((bp, CONV_W - 1, d_b), F32)

    outs = {k: [] for k in ("hg_p", "lru_p", "conv_p", "mk_p", "mv_p", "hg_s", "lru_s", "conv_s")}
    for l in range(depth):
        row = lambda a: a[l].reshape(1, -1)
        p = dict(
            mem_heads=mem_heads, mem_hd=hd,
            norm_pre_mix=row(norm_pre_mix), w_in=w_in[l].astype(BF16), b_gate=row(b_gate),
            hgrn_out_norm=row(hgrn_out_norm), conv_w=conv_w[l], conv_b=row(conv_b),
            lru_wax=jnp.concatenate([lru_wa[l], lru_wx[l]], axis=-1).astype(BF16),
            lru_ba=row(lru_ba), lru_bx=row(lru_bx), lru_lambda=row(lru_lambda),
            w_branch_a=w_branch_a[l].astype(BF16), w_branch_b=w_branch_b[l].astype(BF16),
            w_branch_c=w_branch_c[l].astype(BF16), w_out=w_out[l].astype(BF16),
            norm_post_mix=row(norm_post_mix), norm_pre_ffn=row(norm_pre_ffn),
            ffn_w_gu=ffn_w_gu[l].astype(BF16), ffn_w_down=ffn_w_down[l].astype(BF16),
            norm_post_ffn=row(norm_post_ffn),
        )
        lb = lbs[l].reshape(1, -1)

        kv = norm_matmul(mem2, row(norm_mem), mem_w_kv[l].astype(BF16),
                         tm=_row_tile(bp * n_mem, 512), tn=1024)
        mk = kv[:, :d_c].reshape(bp, n_mem, d_c)
        mv = kv[:, d_c:].reshape(bp, n_mem, d_c)
        xp, s1, h1, c1 = _trunk_layer(xp, bp, sp, mk, mv, zero_hg, zero_lru, zero_conv, lb, p)
        outs["hg_p"].append(s1); outs["lru_p"].append(h1); outs["conv_p"].append(c1)
        outs["mk_p"].append(mk.reshape(bp, n_mem, mem_heads, hd))
        outs["mv_p"].append(mv.reshape(bp, n_mem, mem_heads, hd))

        xs, s2, h2, c2 = _trunk_layer(
            xs, bs, ss, cache_mem_k[l].reshape(bs, n_mem, d_c), cache_mem_v[l].reshape(bs, n_mem, d_c),
            state_hgrn[l], state_lru[l], state_conv[l], lb, p)
        outs["hg_s"].append(s2); outs["lru_s"].append(h2); outs["conv_s"].append(c2)

    st = {k: jnp.stack(v) for k, v in outs.items()}
    return (xp.reshape(bp, sp, d), xs.reshape(bs, ss, d), st["hg_p"], st["lru_p"], st["conv_p"],
            st["mk_p"], st["mv_p"], st["hg_s"], st["lru_s"], st["conv_s"])
```

```python
import functools
import math

import numpy as np
import jax
import jax.numpy as jnp
from jax import lax
from jax.experimental import pallas as pl
from jax.experimental.pallas import tpu as pltpu

F32 = jnp.float32
BF16 = jnp.bfloat16

EPS = 1e-6
LRU_C = 8.0
CHUNK = 64
SUB = 16
CONV_W = 4
CONV_PAD = 8
N_GATES = 3

V7X_VMEM_BYTES = 64 * 1024 * 1024
VMEM_LIMIT_CAP = 56 * 1024 * 1024


def _vmem_limit(estimate_bytes):
    return int(min(VMEM_LIMIT_CAP, max(16 * 1024 * 1024, estimate_bytes * 5 // 4)))


def _params(sem, vmem_estimate):
    return pltpu.CompilerParams(dimension_semantics=sem, vmem_limit_bytes=_vmem_limit(vmem_estimate))


def _rms(x, g):
    ms = jnp.mean(x * x, axis=-1, keepdims=True)
    return x * lax.rsqrt(ms + EPS) * g


def _dot(a, b):
    return jnp.dot(a, b, preferred_element_type=F32)


def _dot_nt(a, b):
    return lax.dot_general(a, b, (((1,), (1,)), ((), ())), preferred_element_type=F32)


def _dot_tn(a, b):
    return lax.dot_general(a, b, (((0,), (0,)), ((), ())), preferred_element_type=F32)


def _norm_matmul_kernel(x_ref, g_ref, w_ref, o_ref, h_ref):
    @pl.when(pl.program_id(1) == 0)
    def _():
        h_ref[...] = _rms(x_ref[...], g_ref[...]).astype(BF16)

    o_ref[...] = _dot(h_ref[...], w_ref[...]).astype(o_ref.dtype)


def norm_matmul(x, g, w, *, tm, tn, out_dtype=F32):
    t, d = x.shape
    n = w.shape[1]
    assert t % tm == 0 and n % tn == 0
    est = 2 * tm * d * 4 + 2 * d * tn * 2 + 2 * tm * tn * 4 + tm * d * 2
    return pl.pallas_call(
        _norm_matmul_kernel,
        grid=(t // tm, n // tn),
        in_specs=[
            pl.BlockSpec((tm, d), lambda i, j: (i, 0)),
            pl.BlockSpec((1, d), lambda i, j: (0, 0)),
            pl.BlockSpec((d, tn), lambda i, j: (0, j)),
        ],
        out_specs=pl.BlockSpec((tm, tn), lambda i, j: (i, j)),
        out_shape=jax.ShapeDtypeStruct((t, n), out_dtype),
        scratch_shapes=[pltpu.VMEM((tm, d), BF16)],
        compiler_params=_params(("parallel", "arbitrary"), est),
        name="norm_matmul",
    )(x, g, w)


IN_F32_SECTIONS = 2
IN_SEC_LOGF, IN_SEC_RX, IN_SEC_Q, IN_SEC_V, IN_SEC_OG, IN_SEC_RY, IN_SEC_GATES = range(7)


def _in_proj_kernel(x_ref, g_ref, w_ref, lb_ref, bg_ref, of_ref, ob_ref, h_ref, *, tps, n_gate_secs, sub_rows):
    j = pl.program_id(1)
    sec = j // tps

    @pl.when(j == 0)
    def _():
        h_ref[...] = _rms(x_ref[...], g_ref[...]).astype(BF16)

    tm = h_ref.shape[0]

    def run(out_ref, act):
        for r in range(tm // sub_rows):
            rs = slice(r * sub_rows, (r + 1) * sub_rows)
            out_ref[rs, :] = act(_dot(h_ref[rs, :], w_ref[...])).astype(out_ref.dtype)

    def log_forget(a):
        lb = lb_ref[...]
        return jnp.log(lb + (1.0 - lb) * jax.nn.sigmoid(a))

    sec_cq = IN_SEC_GATES + n_gate_secs
    pl.when(sec == IN_SEC_LOGF)(lambda: run(of_ref, log_forget))
    pl.when(sec == IN_SEC_RX)(lambda: run(of_ref, lambda a: a))
    pl.when((sec == IN_SEC_Q) | (sec == IN_SEC_OG))(lambda: run(ob_ref, jax.nn.silu))
    pl.when((sec == IN_SEC_V) | (sec == sec_cq))(lambda: run(ob_ref, lambda a: a))
    pl.when(sec == IN_SEC_RY)(lambda: run(ob_ref, jax.nn.gelu))
    pl.when((sec >= IN_SEC_GATES) & (sec < sec_cq))(
        lambda: run(ob_ref, lambda a: jax.nn.sigmoid(a + bg_ref[...])))


def in_proj(x, g, w, lb, bg, *, sec, tm, tn):
    t, d = x.shape
    n = w.shape[1]
    assert t % tm == 0 and sec % tn == 0 and n % sec == 0
    tps = sec // tn
    n_sec = n // sec
    n_gate_secs = bg.shape[1] // sec
    assert n_sec == IN_SEC_GATES + n_gate_secs + 1
    nf = IN_F32_SECTIONS * tps
    sub_rows = min(tm, 256)
    est = (2 * tm * d * 4 + 2 * d * tn * 2 + 2 * tm * tn * 4 + 2 * tm * tn * 2 + tm * d * 2
           + 6 * sub_rows * tn * 4)
    kern = functools.partial(_in_proj_kernel, tps=tps, n_gate_secs=n_gate_secs, sub_rows=sub_rows)
    return pl.pallas_call(
        kern,
        grid=(t // tm, n // tn),
        in_specs=[
            pl.BlockSpec((tm, d), lambda i, j: (i, 0)),
            pl.BlockSpec((1, d), lambda i, j: (0, 0)),
            pl.BlockSpec((d, tn), lambda i, j: (0, j)),
            pl.BlockSpec((1, tn), lambda i, j: (0, jnp.minimum(j, tps - 1))),
            pl.BlockSpec((1, tn), lambda i, j: (0, jnp.clip(j - IN_SEC_GATES * tps, 0, n_gate_secs * tps - 1))),
        ],
        out_specs=[
            pl.BlockSpec((tm, tn), lambda i, j: (i, jnp.minimum(j, nf - 1))),
            pl.BlockSpec((tm, tn), lambda i, j: (i, jnp.maximum(j - nf, 0))),
        ],
        out_shape=[
            jax.ShapeDtypeStruct((t, IN_F32_SECTIONS * sec), F32),
            jax.ShapeDtypeStruct((t, n - IN_F32_SECTIONS * sec), BF16),
        ],
        scratch_shapes=[pltpu.VMEM((tm, d), BF16)],
        compiler_params=_params(("parallel", "arbitrary"), est),
        name="in_proj",
    )(x, g, w, lb, bg)


def _hgrn_consts(chunk, dk):
    n_sub = chunk // SUB
    t = np.arange(chunk)[:, None]
    s = np.arange(chunk)[None, :]
    parts = [s <= t, s <= (t // SUB) * SUB + SUB - 1, s <= chunk - 1 + 0 * t]
    parts += [s <= (j + 1) * SUB - 1 + 0 * t for j in range(n_sub - 1)]
    cum = np.concatenate(parts, axis=0).astype(np.float32)
    lane_blk = np.arange(SUB * dk)[:, None] // dk
    sel = (lane_blk == (np.arange(chunk)[None, :] % SUB)).astype(np.float32)
    return jnp.asarray(cum, BF16), jnp.asarray(sel, BF16)


def _hgrn_kernel(q_ref, lf_ref, v_ref, og_ref, gn_ref, s0_ref, cum_ref, sel_ref,
                 o_ref, sfin_ref, st_ref, b_ref, k_ref, u_ref, sb_ref, *, chunk, n_chunks):
    l = pl.program_id(2)
    n_sub = chunk // SUB
    dk = q_ref.shape[1]

    @pl.when(l == 0)
    def _():
        st_ref[...] = s0_ref[0, 0].T

    q = q_ref[...].astype(F32)
    lf = lf_ref[...]
    kk = 1.0 - jnp.exp(lf)
    vb = v_ref[...]
    k_ref[...] = kk

    p0 = lf.astype(BF16)
    r1 = lf - p0.astype(F32)
    p1 = r1.astype(BF16)
    p2 = (r1 - p1.astype(F32)).astype(BF16)
    pieces = jnp.concatenate([p0, p1, p2], axis=1)
    cum = cum_ref[...]
    sums = []
    for c in range(n_chunks):
        r = _dot(cum, pieces[c * chunk:(c + 1) * chunk, :])
        sums.append(r[:, :dk] + r[:, dk:2 * dk] + r[:, 2 * dk:])

    def gather(i):
        return jnp.concatenate([s_[i * chunk:(i + 1) * chunk, :] for s_ in sums], axis=0)

    b = gather(0)
    b_end = gather(1)
    b_last = gather(2)
    b_ref[...] = b

    qe = (q * jnp.exp(b)).astype(BF16)
    k_dec = (kk * jnp.exp(b_end - b)).astype(BF16)
    k_end = (kk * jnp.exp(b_last - b)).astype(BF16)
    decay = jnp.exp(b_last)
    q_dec = [(q * jnp.exp(jnp.minimum(b - gather(3 + j), 0.0))).astype(BF16) for j in range(n_sub - 1)]

    row = lax.broadcasted_iota(jnp.int32, (chunk, chunk), 0)
    col = lax.broadcasted_iota(jnp.int32, (chunk, chunk), 1)
    row_blk = row // SUB
    col_blk = col // SUB
    diag_mask = (col_blk == row_blk) & (col <= row)
    sel = sel_ref[...]

    intra = []
    for c in range(n_chunks):
        rows = slice(c * chunk, (c + 1) * chunk)
        attn = jnp.zeros((chunk, chunk), F32)
        if n_sub > 1:
            qd = jnp.concatenate([qj[rows, :] for qj in q_dec], axis=0)
            m = _dot_nt(qd, k_dec[rows, :])
            for j in range(n_sub - 1):
                attn = jnp.where((col_blk == j) & (row_blk > j), m[j * chunk:(j + 1) * chunk, :], attn)

        q_c = q[rows, :]
        b_c = b[rows, :]
        ws = []
        for u in range(SUB):
            b_s = jnp.concatenate(
                [jnp.broadcast_to(b_ref[pl.ds(c * chunk + j * SUB + u, 1), :], (SUB, dk))
                 for j in range(n_sub)], axis=0)
            k_s = jnp.concatenate(
                [jnp.broadcast_to(k_ref[pl.ds(c * chunk + j * SUB + u, 1), :], (SUB, dk))
                 for j in range(n_sub)], axis=0)
            ws.append((q_c * jnp.exp(jnp.minimum(b_c - b_s, 0.0)) * k_s).astype(BF16))
        d = _dot(jnp.concatenate(ws, axis=1), sel)
        attn = jnp.where(diag_mask, d, attn)

        intra.append(_dot(attn.astype(BF16), vb[rows, :]))
        u_ref[c] = _dot_tn(vb[rows, :], k_end[rows, :])

    st = st_ref[...]
    for c in range(n_chunks):
        sb_ref[c] = st.astype(BF16)
        st = st * decay[c * chunk:c * chunk + 1, :] + u_ref[c]
    st_ref[...] = st

    inter = [_dot_nt(qe[c * chunk:(c + 1) * chunk, :], sb_ref[c]) for c in range(n_chunks)]
    o = jnp.concatenate(intra, axis=0) + jnp.concatenate(inter, axis=0)
    ms = jnp.mean(o * o, axis=-1, keepdims=True)
    o = o * lax.rsqrt(ms + EPS) * gn_ref[...] * og_ref[...].astype(F32)
    o_ref[...] = o.astype(o_ref.dtype)

    @pl.when(l == pl.num_programs(2) - 1)
    def _():
        sfin_ref[0, 0] = st.T


def hgrn2(pf, pb, gn, s0, *, bsz, seq, heads, dk, tl):
    chunk = min(CHUNK, seq)
    assert seq % tl == 0 and tl % chunk == 0 and chunk % SUB == 0
    nl = seq // tl
    d_a = heads * dk

    def sec(k):
        return pl.BlockSpec((tl, dk), lambda b, h, l, k=k: (b * nl + l, k * heads + h))

    n_chunks = tl // chunk
    cum, sel = _hgrn_consts(chunk, dk)
    est = (2 * tl * dk * (4 + 3 * 2) + 2 * tl * dk * 2 + 6 * dk * dk * 4 + 2 * tl * dk * 4
           + n_chunks * dk * dk * 6 + 2 * (cum.size + sel.size) * 2 + 24 * tl * dk * 4)
    kern = functools.partial(_hgrn_kernel, chunk=chunk, n_chunks=n_chunks)
    return pl.pallas_call(
        kern,
        grid=(bsz, heads, nl),
        in_specs=[
            sec(IN_SEC_Q - IN_F32_SECTIONS), sec(IN_SEC_LOGF),
            sec(IN_SEC_V - IN_F32_SECTIONS), sec(IN_SEC_OG - IN_F32_SECTIONS),
            pl.BlockSpec((1, dk), lambda b, h, l: (0, h)),
            pl.BlockSpec((1, 1, dk, dk), lambda b, h, l: (b, h, 0, 0)),
            pl.BlockSpec(cum.shape, lambda b, h, l: (0, 0)),
            pl.BlockSpec(sel.shape, lambda b, h, l: (0, 0)),
        ],
        out_specs=[
            pl.BlockSpec((tl, dk), lambda b, h, l: (b * nl + l, h)),
            pl.BlockSpec((1, 1, dk, dk), lambda b, h, l: (b, h, 0, 0)),
        ],
        out_shape=[
            jax.ShapeDtypeStruct((bsz * seq, d_a), BF16),
            jax.ShapeDtypeStruct((bsz, heads, dk, dk), F32),
        ],
        scratch_shapes=[
            pltpu.VMEM((dk, dk), F32),
            pltpu.VMEM((tl, dk), F32),
            pltpu.VMEM((tl, dk), F32),
            pltpu.VMEM((n_chunks, dk, dk), F32),
            pltpu.VMEM((n_chunks, dk, dk), BF16),
        ],
        compiler_params=_params(("parallel", "parallel", "arbitrary"), est),
        name="hgrn2",
    )(pb, pf, pb, pb, gn, s0, cum, sel)


def _lru_kernel(rx_ref, gy_ref, cw_ref, cb_ref, wax_ref, ba_ref, bx_ref, lam_ref, h0_ref, buf_ref,
                o_ref, hlast_ref, xp_ref, a_ref, u_ref, h_ref, *, tl, n_blocks, bw):
    l = pl.program_id(1)
    keep = CONV_W - 1

    @pl.when(l == 0)
    def _():
        xp_ref[CONV_PAD - keep:CONV_PAD, :] = buf_ref[0]
        h_ref[...] = h0_ref[0]

    x = rx_ref[...]
    xp_ref[CONV_PAD:CONV_PAD + tl, :] = x
    cw = cw_ref[...]
    xc = xp_ref[CONV_PAD - keep:CONV_PAD - keep + tl, :] * cw[0:1, :]
    for j in range(1, CONV_W):
        xc = xc + xp_ref[CONV_PAD - keep + j:CONV_PAD - keep + j + tl, :] * cw[j:j + 1, :]
    xc = xc + cb_ref[...]
    xp_ref[CONV_PAD - keep:CONV_PAD, :] = xp_ref[CONV_PAD + tl - keep:CONV_PAD + tl, :]

    xcb = xc.astype(BF16)
    pre = [_dot(xcb[:, n * bw:(n + 1) * bw], wax_ref[n]) for n in range(n_blocks)]
    r = jax.nn.sigmoid(jnp.concatenate([pn[:, :bw] for pn in pre], axis=-1) + ba_ref[...])
    ig = jax.nn.sigmoid(jnp.concatenate([pn[:, bw:] for pn in pre], axis=-1) + bx_ref[...])
    lam = lam_ref[...]
    softplus_neg = jnp.maximum(-lam, 0.0) + jnp.log1p(jnp.exp(-jnp.abs(lam)))
    log_a = -LRU_C * r * softplus_neg
    a = jnp.exp(log_a)
    mult = jnp.sqrt(jnp.maximum(-jnp.tanh(log_a) * (a * a + 1.0), 0.0))
    a_ref[...] = a
    u_ref[...] = mult * (ig * xc)

    def step(t, h):
        h = a_ref[pl.ds(t, 1), :] * h + u_ref[pl.ds(t, 1), :]
        u_ref[pl.ds(t, 1), :] = h
        return h

    h = lax.fori_loop(0, tl, step, h_ref[...], unroll=8)
    h_ref[...] = h
    o_ref[...] = (u_ref[...] * gy_ref[...].astype(F32)).astype(o_ref.dtype)

    @pl.when(l == pl.num_programs(1) - 1)
    def _():
        hlast_ref[0] = h


def conv_lru(pf, pb, cw, cb, wax, ba, bx, lam, h0, buf, *, bsz, seq, d, tl):
    assert seq % tl == 0 and tl >= CONV_W - 1
    nl = seq // tl
    n_blocks, bw = wax.shape[0], wax.shape[1]
    vec = pl.BlockSpec((1, d), lambda b, l: (0, 0))
    est = 2 * tl * d * (4 + 2) + 2 * tl * d * 2 + (3 * tl + CONV_PAD) * d * 4 + 8 * tl * d * 4
    kern = functools.partial(_lru_kernel, tl=tl, n_blocks=n_blocks, bw=bw)
    return pl.pallas_call(
        kern,
        grid=(bsz, nl),
        in_specs=[
            pl.BlockSpec((tl, d), lambda b, l: (b * nl + l, IN_SEC_RX)),
            pl.BlockSpec((tl, d), lambda b, l: (b * nl + l, IN_SEC_RY - IN_F32_SECTIONS)),
            pl.BlockSpec((CONV_W, d), lambda b, l: (0, 0)),
            vec,
            pl.BlockSpec((n_blocks, bw, 2 * bw), lambda b, l: (0, 0, 0)),
            vec, vec, vec,
            pl.BlockSpec((1, 1, d), lambda b, l: (b, 0, 0)),
            pl.BlockSpec((1, CONV_W - 1, d), lambda b, l: (b, 0, 0)),
        ],
        out_specs=[
            pl.BlockSpec((tl, d), lambda b, l: (b * nl + l, 0)),
            pl.BlockSpec((1, 1, d), lambda b, l: (b, 0, 0)),
        ],
        out_shape=[
            jax.ShapeDtypeStruct((bsz * seq, d), BF16),
            jax.ShapeDtypeStruct((bsz, 1, d), F32),
        ],
        scratch_shapes=[
            pltpu.VMEM((CONV_PAD + tl, d), F32),
            pltpu.VMEM((tl, d), F32),
            pltpu.VMEM((tl, d), F32),
            pltpu.VMEM((1, d), F32),
        ],
        compiler_params=_params(("parallel", "arbitrary"), est),
        name="conv_lru",
    )(pf, pb, cw, cb, wax, ba, bx, lam, h0, buf)


def _mem_attn_kernel(q_ref, k_ref, v_ref, o_ref, *, scale):
    q = q_ref[...]
    k = k_ref[0].astype(BF16)
    v = v_ref[0].astype(BF16)
    s = _dot_nt(q, k) * scale
    s = s - jnp.max(s, axis=-1, keepdims=True)
    p = jnp.exp(s)
    denom = jnp.sum(p, axis=-1, keepdims=True)
    o_ref[...] = (_dot(p.astype(BF16), v) / denom).astype(o_ref.dtype)


def mem_attn(pb, mem_k, mem_v, *, bsz, seq, heads, hd, col0, tl):
    assert seq % tl == 0
    nl = seq // tl
    n_mem = mem_k.shape[1]
    est = 2 * tl * hd * 2 + 4 * n_mem * hd * 4 + 2 * tl * hd * 2 + 4 * tl * n_mem * 4
    kern = functools.partial(_mem_attn_kernel, scale=1.0 / math.sqrt(hd))
    return pl.pallas_call(
        kern,
        grid=(bsz, nl, heads),
        in_specs=[
            pl.BlockSpec((tl, hd), lambda b, l, h: (b * nl + l, col0 + h)),
            pl.BlockSpec((1, n_mem, hd), lambda b, l, h: (b, 0, h)),
            pl.BlockSpec((1, n_mem, hd), lambda b, l, h: (b, 0, h)),
        ],
        out_specs=pl.BlockSpec((tl, hd), lambda b, l, h: (b * nl + l, h)),
        out_shape=jax.ShapeDtypeStruct((bsz * seq, heads * hd), BF16),
        compiler_params=_params(("parallel", "parallel", "parallel"), est),
        name="mem_attn",
    )(pb, mem_k, mem_v)


def _merge_kernel(x_ref, oa_ref, ob_ref, oc_ref, g0_ref, g1_ref, g2_ref,
                  wa_ref, wb_ref, wc_ref, wo_ref, gn_ref, y_ref):
    m = g0_ref[...].astype(F32) * _dot(oa_ref[...], wa_ref[...])
    m = m + g1_ref[...].astype(F32) * _dot(ob_ref[...], wb_ref[...])
    m = m + g2_ref[...].astype(F32) * _dot(oc_ref[...], wc_ref[...])
    z = _dot(m.astype(BF16), wo_ref[...])
    y_ref[...] = x_ref[...] + _rms(z, gn_ref[...])


def merge(x, oa, ob, oc, pb, wa, wb, wc, wo, gn, *, col_gates, tm):
    t, d = x.shape
    db = oa.shape[1]
    assert t % tm == 0
    row = lambda i: (i, 0)
    const = lambda i: (0, 0)
    single = dict(pipeline_mode=pl.Buffered(1))
    gate_specs = [pl.BlockSpec((tm, d), lambda i, k=k: (i, col_gates + k)) for k in range(N_GATES)]
    est = (4 * tm * d * 4 + 6 * tm * db * 2 + 6 * tm * d * 2
           + 3 * db * d * 2 + d * d * 2 + 6 * tm * d * 4)
    return pl.pallas_call(
        _merge_kernel,
        grid=(t // tm,),
        in_specs=[
            pl.BlockSpec((tm, d), row),
            pl.BlockSpec((tm, db), row), pl.BlockSpec((tm, db), row), pl.BlockSpec((tm, db), row),
            *gate_specs,
            pl.BlockSpec((db, d), const, **single),
            pl.BlockSpec((db, d), const, **single),
            pl.BlockSpec((db, d), const, **single),
            pl.BlockSpec((d, d), const, **single),
            pl.BlockSpec((1, d), const),
        ],
        out_specs=pl.BlockSpec((tm, d), row),
        out_shape=jax.ShapeDtypeStruct((t, d), F32),
        compiler_params=_params(("parallel",), est),
        name="merge",
    )(x, oa, ob, oc, pb, pb, pb, wa, wb, wc, wo, gn)


def _ffn_kernel(x_ref, gpre_ref, wg_ref, wu_ref, wd_ref, gpost_ref, y_ref, h_ref, acc_ref):
    j = pl.program_id(1)

    @pl.when(j == 0)
    def _():
        h_ref[...] = _rms(x_ref[...], gpre_ref[...]).astype(BF16)
        acc_ref[...] = jnp.zeros_like(acc_ref)

    h = h_ref[...]
    gt = _dot(h, wg_ref[...])
    up = _dot(h, wu_ref[...])
    act = (jax.nn.silu(gt) * up).astype(BF16)
    acc_ref[...] += _dot(act, wd_ref[...])

    @pl.when(j == pl.num_programs(1) - 1)
    def _():
        y_ref[...] = x_ref[...] + _rms(acc_ref[...], gpost_ref[...])


def ffn(x, gpre, w_gu, w_down, gpost, *, tm, tf):
    t, d = x.shape
    d_ff = w_down.shape[0]
    assert t % tm == 0 and d_ff % tf == 0
    nf = d_ff // tf
    est = 4 * tm * d * 4 + 2 * 3 * d * tf * 2 + tm * d * 2 + tm * d * 4 + 4 * tm * tf * 4
    return pl.pallas_call(
        _ffn_kernel,
        grid=(t // tm, nf),
        in_specs=[
            pl.BlockSpec((tm, d), lambda i, j: (i, 0)),
            pl.BlockSpec((1, d), lambda i, j: (0, 0)),
            pl.BlockSpec((d, tf), lambda i, j: (0, j)),
            pl.BlockSpec((d, tf), lambda i, j: (0, nf + j)),
            pl.BlockSpec((tf, d), lambda i, j: (j, 0)),
            pl.BlockSpec((1, d), lambda i, j: (0, 0)),
        ],
        out_specs=pl.BlockSpec((tm, d), lambda i, j: (i, 0)),
        out_shape=jax.ShapeDtypeStruct((t, d), F32),
        scratch_shapes=[pltpu.VMEM((tm, d), BF16), pltpu.VMEM((tm, d), F32)],
        compiler_params=_params(("parallel", "arbitrary"), est),
        name="ffn",
    )(x, gpre, w_gu, w_gu, w_down, gpost)


def _row_tile(n, target):
    t = min(n, target)
    while n % t:
        t //= 2
    return t


def _regroup_w_in(w, d_a, d_b, d_c):
    assert d_a == d_b == d_c
    s = d_a
    hq, hf, hi, hg, rx, ry, cq = (w[:, k * s:(k + 1) * s] for k in range(7))
    gates = w[:, 7 * s:]
    return jnp.concatenate([hf, rx, hq, hi, hg, ry, gates, cq], axis=1).astype(BF16)


def _trunk_layer(x2, bsz, seq, mem_k, mem_v, s_hg, h_lru, conv_buf, lb, p):
    t, d = x2.shape
    heads, dk = s_hg.shape[1], s_hg.shape[2]
    d_a = heads * dk
    d_b = h_lru.shape[-1]
    mem_heads, hd = p["mem_heads"], p["mem_hd"]
    if seq < CONV_W - 1:
        raise NotImplementedError("sequence shorter than the conv history")

    pf, pb = in_proj(x2, p["norm_pre_mix"], p["w_in"], lb, p["b_gate"], sec=d_a,
                     tm=_row_tile(t, 1024), tn=512)
    n_bf_sections = pb.shape[1] // d_a

    o_a, s_new = hgrn2(pf, pb, p["hgrn_out_norm"], s_hg, bsz=bsz, seq=seq, heads=heads, dk=dk,
                       tl=_row_tile(seq, 512))
    o_b, h_last = conv_lru(pf, pb, p["conv_w"], p["conv_b"], p["lru_wax"], p["lru_ba"], p["lru_bx"],
                           p["lru_lambda"], h_lru.reshape(bsz, 1, d_b), conv_buf,
                           bsz=bsz, seq=seq, d=d_b, tl=_row_tile(seq, 256))
    o_c = mem_attn(pb, mem_k, mem_v, bsz=bsz, seq=seq, heads=mem_heads, hd=hd,
                   col0=(n_bf_sections - 1) * d_a // hd, tl=_row_tile(seq, 512))

    gate_col = (IN_SEC_GATES - IN_F32_SECTIONS) * d_a
    assert gate_col % d == 0
    x2 = merge(x2, o_a, o_b, o_c, pb, p["w_branch_a"], p["w_branch_b"], p["w_branch_c"],
               p["w_out"], p["norm_post_mix"], col_gates=gate_col // d, tm=_row_tile(t, 256))
    x2 = ffn(x2, p["norm_pre_ffn"], p["ffn_w_gu"], p["ffn_w_down"], p["norm_post_ffn"],
             tm=_row_tile(t, 512), tf=512)

    rx_tail = pf.reshape(bsz, seq, -1)[:, seq - (CONV_W - 1):, IN_SEC_RX * d_a:(IN_SEC_RX + 1) * d_a]
    return x2, s_new, h_last.reshape(bsz, d_b), rx_tail


def kernel(x_prompt, x_sample, state_hgrn, state_lru, state_conv, cache_mem_k, cache_mem_v, mem_prompt, norm_mem, mem_w_kv, hgrn_lower_bound, norm_pre_mix, w_in, b_gate, hgrn_out_norm, conv_w, conv_b, lru_wa, lru_ba, lru_wx, lru_bx, lru_lambda, w_branch_a, w_branch_b, w_branch_c, w_out, norm_post_mix, norm_pre_ffn, ffn_w_gu, ffn_w_down, norm_post_ffn):
    depth = w_in.shape[0]
    bp, sp, d = x_prompt.shape
    bs, ss, _ = x_sample.shape
    _, _, heads, dk, dv = state_hgrn.shape
    d_a = heads * dk
    d_b = state_lru.shape[-1]
    n_mem, mem_heads, hd = cache_mem_k.shape[2:]
    d_c = mem_heads * hd
    assert dk == dv and d_b == d_a and d_c == d_a and d == 2 * d_a
    assert b_gate.shape[1] == N_GATES * d

    sm = jax.nn.softmax(hgrn_lower_bound.astype(F32), axis=0)
    lbs = jnp.cumsum(sm, axis=0) - sm[0:1]

    xp = x_prompt.reshape(bp * sp, d)
    xs = x_sample.reshape(bs * ss, d)
    mem2 = mem_prompt.reshape(bp * n_mem, d)
    zero_hg = jnp.zeros((bp, heads, dk, dv), F32)
    zero_lru = jnp.zeros((bp, d_b), F32)
    zero_conv = jnp.zeros((bp, CONV_W - 1, d_b), F32)

    outs = {k: [] for k in ("hg_p", "lru_p", "conv_p", "mk_p", "mv_p", "hg_s", "lru_s", "conv_s")}
    for l in range(depth):
        row = lambda a: a[l].reshape(1, -1)
        p = dict(
            mem_heads=mem_heads, mem_hd=hd,
            norm_pre_mix=row(norm_pre_mix), w_in=_regroup_w_in(w_in[l], d_a, d_b, d_c), b_gate=row(b_gate),
            hgrn_out_norm=row(hgrn_out_norm), conv_w=conv_w[l], conv_b=row(conv_b),
            lru_wax=jnp.concatenate([lru_wa[l], lru_wx[l]], axis=-1).astype(BF16),
            lru_ba=row(lru_ba), lru_bx=row(lru_bx), lru_lambda=row(lru_lambda),
            w_branch_a=w_branch_a[l].astype(BF16), w_branch_b=w_branch_b[l].astype(BF16),
            w_branch_c=w_branch_c[l].astype(BF16), w_out=w_out[l].astype(BF16),
            norm_post_mix=row(norm_post_mix), norm_pre_ffn=row(norm_pre_ffn),
            ffn_w_gu=ffn_w_gu[l].astype(BF16), ffn_w_down=ffn_w_down[l].astype(BF16),
            norm_post_ffn=row(norm_post_ffn),
        )
        lb = lbs[l].reshape(1, -1)

        kv = norm_matmul(mem2, row(norm_mem), mem_w_kv[l].astype(BF16),
                         tm=_row_tile(bp * n_mem, 512), tn=1024)
        mk = kv[:, :d_c].reshape(bp, n_mem, d_c)
        mv = kv[:, d_c:].reshape(bp, n_mem, d_c)
        xp, s1, h1, c1 = _trunk_layer(xp, bp, sp, mk, mv, zero_hg, zero_lru, zero_conv, lb, p)
        outs["hg_p"].append(s1); outs["lru_p"].append(h1); outs["conv_p"].append(c1)
        outs["mk_p"].append(mk.reshape(bp, n_mem, mem_heads, hd))
        outs["mv_p"].append(mv.reshape(bp, n_mem, mem_heads, hd))

        xs, s2, h2, c2 = _trunk_layer(
            xs, bs, ss, cache_mem_k[l].reshape(bs, n_mem, d_c), cache_mem_v[l].reshape(bs, n_mem, d_c),
            state_hgrn[l], state_lru[l], state_conv[l], lb, p)
        outs["hg_s"].append(s2); outs["lru_s"].append(h2); outs["conv_s"].append(c2)

    st = {k: jnp.stack(v) for k, v in outs.items()}
    return (xp.reshape(bp, sp, d), xs.reshape(bs, ss, d), st["hg_p"], st["lru_p"], st["conv_p"],
            st["mk_p"], st["mv_p"], st["hg_s"], st["lru_s"], st["conv_s"])
```

```python
import functools
import math

import numpy as np
import jax
import jax.numpy as jnp
from jax import lax
from jax.experimental import pallas as pl
from jax.experimental.pallas import tpu as pltpu

F32 = jnp.float32
BF16 = jnp.bfloat16

EPS = 1e-6
LRU_C = 8.0
CHUNK = 64
SUB = 16
HALF = 8
N_PIECES = 3
CONV_W = 4
CONV_PAD = 8
N_GATES = 3

V7X_VMEM_BYTES = 64 * 1024 * 1024
VMEM_LIMIT_CAP = 56 * 1024 * 1024


def _vmem_limit(estimate_bytes):
    return int(min(VMEM_LIMIT_CAP, max(16 * 1024 * 1024, estimate_bytes * 5 // 4)))


def _params(sem, vmem_estimate):
    return pltpu.CompilerParams(dimension_semantics=sem, vmem_limit_bytes=_vmem_limit(vmem_estimate))


def _rms(x, g):
    ms = jnp.mean(x * x, axis=-1, keepdims=True)
    return x * lax.rsqrt(ms + EPS) * g


def _dot(a, b):
    return jnp.dot(a, b, preferred_element_type=F32)


def _dot_nt(a, b):
    return lax.dot_general(a, b, (((1,), (1,)), ((), ())), preferred_element_type=F32)


def _dot_tn(a, b):
    return lax.dot_general(a, b, (((0,), (0,)), ((), ())), preferred_element_type=F32)


def _norm_matmul_kernel(x_ref, g_ref, w_ref, o_ref, h_ref):
    @pl.when(pl.program_id(1) == 0)
    def _():
        h_ref[...] = _rms(x_ref[...], g_ref[...]).astype(BF16)

    o_ref[...] = _dot(h_ref[...], w_ref[...]).astype(o_ref.dtype)


def norm_matmul(x, g, w, *, tm, tn, out_dtype=F32):
    t, d = x.shape
    n = w.shape[1]
    assert t % tm == 0 and n % tn == 0
    est = 2 * tm * d * 4 + 2 * d * tn * 2 + 2 * tm * tn * 4 + tm * d * 2
    return pl.pallas_call(
        _norm_matmul_kernel,
        grid=(t // tm, n // tn),
        in_specs=[
            pl.BlockSpec((tm, d), lambda i, j: (i, 0)),
            pl.BlockSpec((1, d), lambda i, j: (0, 0)),
            pl.BlockSpec((d, tn), lambda i, j: (0, j)),
        ],
        out_specs=pl.BlockSpec((tm, tn), lambda i, j: (i, j)),
        out_shape=jax.ShapeDtypeStruct((t, n), out_dtype),
        scratch_shapes=[pltpu.VMEM((tm, d), BF16)],
        compiler_params=_params(("parallel", "arbitrary"), est),
        name="norm_matmul",
    )(x, g, w)


IN_F32_SECTIONS = 2
IN_SEC_LOGF, IN_SEC_RX, IN_SEC_Q, IN_SEC_V, IN_SEC_OG, IN_SEC_RY, IN_SEC_GATES = range(7)


def _in_proj_kernel(x_ref, g_ref, w_ref, lb_ref, bg_ref, of_ref, ob_ref, h_ref, *, tps, n_gate_secs, sub_rows):
    j = pl.program_id(1)
    sec = j // tps

    @pl.when(j == 0)
    def _():
        h_ref[...] = _rms(x_ref[...], g_ref[...]).astype(BF16)

    tm = h_ref.shape[0]

    def run(out_ref, act):
        for r in range(tm // sub_rows):
            rs = slice(r * sub_rows, (r + 1) * sub_rows)
            out_ref[rs, :] = act(_dot(h_ref[rs, :], w_ref[...])).astype(out_ref.dtype)

    def log_forget(a):
        lb = lb_ref[...]
        return jnp.log(lb + (1.0 - lb) * jax.nn.sigmoid(a))

    sec_cq = IN_SEC_GATES + n_gate_secs
    pl.when(sec == IN_SEC_LOGF)(lambda: run(of_ref, log_forget))
    pl.when(sec == IN_SEC_RX)(lambda: run(of_ref, lambda a: a))
    pl.when((sec == IN_SEC_Q) | (sec == IN_SEC_OG))(lambda: run(ob_ref, jax.nn.silu))
    pl.when((sec == IN_SEC_V) | (sec == sec_cq))(lambda: run(ob_ref, lambda a: a))
    pl.when(sec == IN_SEC_RY)(lambda: run(ob_ref, jax.nn.gelu))
    pl.when((sec >= IN_SEC_GATES) & (sec < sec_cq))(
        lambda: run(ob_ref, lambda a: jax.nn.sigmoid(a + bg_ref[...])))


def in_proj(x, g, w, lb, bg, *, sec, tm, tn):
    t, d = x.shape
    n = w.shape[1]
    assert t % tm == 0 and sec % tn == 0 and n % sec == 0
    tps = sec // tn
    n_sec = n // sec
    n_gate_secs = bg.shape[1] // sec
    assert n_sec == IN_SEC_GATES + n_gate_secs + 1
    nf = IN_F32_SECTIONS * tps
    sub_rows = min(tm, 256)
    est = (2 * tm * d * 4 + 2 * d * tn * 2 + 2 * tm * tn * 4 + 2 * tm * tn * 2 + tm * d * 2
           + 6 * sub_rows * tn * 4)
    kern = functools.partial(_in_proj_kernel, tps=tps, n_gate_secs=n_gate_secs, sub_rows=sub_rows)
    return pl.pallas_call(
        kern,
        grid=(t // tm, n // tn),
        in_specs=[
            pl.BlockSpec((tm, d), lambda i, j: (i, 0)),
            pl.BlockSpec((1, d), lambda i, j: (0, 0)),
            pl.BlockSpec((d, tn), lambda i, j: (0, j)),
            pl.BlockSpec((1, tn), lambda i, j: (0, jnp.minimum(j, tps - 1))),
            pl.BlockSpec((1, tn), lambda i, j: (0, jnp.clip(j - IN_SEC_GATES * tps, 0, n_gate_secs * tps - 1))),
        ],
        out_specs=[
            pl.BlockSpec((tm, tn), lambda i, j: (i, jnp.minimum(j, nf - 1))),
            pl.BlockSpec((tm, tn), lambda i, j: (i, jnp.maximum(j - nf, 0))),
        ],
        out_shape=[
            jax.ShapeDtypeStruct((t, IN_F32_SECTIONS * sec), F32),
            jax.ShapeDtypeStruct((t, n - IN_F32_SECTIONS * sec), BF16),
        ],
        scratch_shapes=[pltpu.VMEM((tm, d), BF16)],
        compiler_params=_params(("parallel", "arbitrary"), est),
        name="in_proj",
    )(x, g, w, lb, bg)


def _hgrn_consts(chunk, dk):
    n_sub = chunk // SUB
    t = np.arange(chunk)[:, None]
    s = np.arange(chunk)[None, :]
    sub_start = (t // SUB) * SUB
    parts = [s <= t, s <= sub_start + SUB - 1, s <= chunk - 1 + 0 * t, s <= sub_start + HALF - 1]
    parts += [s <= (j + 1) * SUB - 1 + 0 * t for j in range(n_sub - 1)]
    cum = np.concatenate(parts, axis=0).astype(np.float32)
    cum = np.concatenate([cum] * N_PIECES, axis=1)
    lane_blk = np.arange(HALF * dk)[:, None] // dk
    sel = (lane_blk == (np.arange(chunk)[None, :] % HALF)).astype(np.float32)
    return jnp.asarray(cum, BF16), jnp.asarray(sel, BF16)


def _hgrn_kernel(q_ref, lf_ref, v_ref, og_ref, gn_ref, s0_ref, cum_ref, sel_ref,
                 o_ref, sfin_ref, st_ref, b_ref, k_ref, u_ref, sb_ref, *, chunk, n_chunks):
    l = pl.program_id(2)
    n_sub = chunk // SUB
    dk = q_ref.shape[1]

    @pl.when(l == 0)
    def _():
        st_ref[...] = s0_ref[0, 0].T

    q = q_ref[...].astype(F32)
    lf = lf_ref[...]
    kk = 1.0 - jnp.exp(lf)
    vb = v_ref[...]
    k_ref[...] = kk

    p0 = lf.astype(BF16)
    r1 = lf - p0.astype(F32)
    p1 = r1.astype(BF16)
    p2 = (r1 - p1.astype(F32)).astype(BF16)
    cum = cum_ref[...]
    sums = []
    for c in range(n_chunks):
        rows = slice(c * chunk, (c + 1) * chunk)
        sums.append(_dot(cum, jnp.concatenate([p0[rows, :], p1[rows, :], p2[rows, :]], axis=0)))

    def gather(i):
        return jnp.concatenate([s_[i * chunk:(i + 1) * chunk, :] for s_ in sums], axis=0)

    b = gather(0)
    b_end = gather(1)
    b_last = gather(2)
    b_mid = gather(3)
    b_ref[...] = b

    qe = (q * jnp.exp(b)).astype(BF16)
    k_dec = (kk * jnp.exp(b_end - b)).astype(BF16)
    k_end = (kk * jnp.exp(b_last - b)).astype(BF16)
    decay = jnp.exp(b_last)
    q_dec = [(q * jnp.exp(jnp.minimum(b - gather(4 + j), 0.0))).astype(BF16) for j in range(n_sub - 1)]
    q_mid = (q * jnp.exp(jnp.minimum(b - b_mid, 0.0))).astype(BF16)
    k_mid = (kk * jnp.exp(jnp.minimum(b_mid - b, 0.0))).astype(BF16)

    row = lax.broadcasted_iota(jnp.int32, (chunk, chunk), 0)
    col = lax.broadcasted_iota(jnp.int32, (chunk, chunk), 1)
    row_blk = row // SUB
    col_blk = col // SUB
    mid_mask = (col_blk == row_blk) & (row % SUB >= HALF) & (col % SUB < HALF)
    diag_mask = (col // HALF == row // HALF) & (col <= row)
    sel = sel_ref[...]
    n_half = chunk // HALF

    intra = []
    for c in range(n_chunks):
        rows = slice(c * chunk, (c + 1) * chunk)
        qd = jnp.concatenate([qj[rows, :] for qj in q_dec] + [q_mid[rows, :]], axis=0)
        kd = jnp.concatenate([k_dec[rows, :], k_mid[rows, :]], axis=0)
        m = _dot_nt(qd, kd)
        attn = jnp.where(mid_mask, m[(n_sub - 1) * chunk:, chunk:], 0.0)
        for j in range(n_sub - 1):
            attn = jnp.where((col_blk == j) & (row_blk > j), m[j * chunk:(j + 1) * chunk, :chunk], attn)

        q_c = q[rows, :]
        b_c = b[rows, :]
        ws = []
        for u in range(HALF):
            b_s = jnp.concatenate(
                [jnp.broadcast_to(b_ref[pl.ds(c * chunk + j * HALF + u, 1), :], (HALF, dk))
                 for j in range(n_half)], axis=0)
            k_s = jnp.concatenate(
                [jnp.broadcast_to(k_ref[pl.ds(c * chunk + j * HALF + u, 1), :], (HALF, dk))
                 for j in range(n_half)], axis=0)
            ws.append((q_c * jnp.exp(jnp.minimum(b_c - b_s, 0.0)) * k_s).astype(BF16))
        d = _dot(jnp.concatenate(ws, axis=1), sel)
        attn = jnp.where(diag_mask, d, attn)

        intra.append(_dot(attn.astype(BF16), vb[rows, :]))
        u_ref[c] = _dot_tn(vb[rows, :], k_end[rows, :])

    st = st_ref[...]
    for c in range(n_chunks):
        sb_ref[c] = st.astype(BF16)
        st = st * decay[c * chunk:c * chunk + 1, :] + u_ref[c]
    st_ref[...] = st

    inter = [_dot_nt(qe[c * chunk:(c + 1) * chunk, :], sb_ref[c]) for c in range(n_chunks)]
    o = jnp.concatenate(intra, axis=0) + jnp.concatenate(inter, axis=0)
    ms = jnp.mean(o * o, axis=-1, keepdims=True)
    o = o * lax.rsqrt(ms + EPS) * gn_ref[...] * og_ref[...].astype(F32)
    o_ref[...] = o.astype(o_ref.dtype)

    @pl.when(l == pl.num_programs(2) - 1)
    def _():
        sfin_ref[0, 0] = st.T


def hgrn2(pf, pb, gn, s0, *, bsz, seq, heads, dk, tl):
    chunk = min(CHUNK, seq)
    assert seq % tl == 0 and tl % chunk == 0 and chunk % SUB == 0
    nl = seq // tl
    d_a = heads * dk

    def sec(k):
        return pl.BlockSpec((tl, dk), lambda b, h, l, k=k: (b * nl + l, k * heads + h))

    n_chunks = tl // chunk
    cum, sel = _hgrn_consts(chunk, dk)
    est = (2 * tl * dk * (4 + 3 * 2) + 2 * tl * dk * 2 + 6 * dk * dk * 4 + 2 * tl * dk * 4
           + n_chunks * dk * dk * 6 + 2 * (cum.size + sel.size) * 2 + 24 * tl * dk * 4)
    kern = functools.partial(_hgrn_kernel, chunk=chunk, n_chunks=n_chunks)
    return pl.pallas_call(
        kern,
        grid=(bsz, heads, nl),
        in_specs=[
            sec(IN_SEC_Q - IN_F32_SECTIONS), sec(IN_SEC_LOGF),
            sec(IN_SEC_V - IN_F32_SECTIONS), sec(IN_SEC_OG - IN_F32_SECTIONS),
            pl.BlockSpec((1, dk), lambda b, h, l: (0, h)),
            pl.BlockSpec((1, 1, dk, dk), lambda b, h, l: (b, h, 0, 0)),
            pl.BlockSpec(cum.shape, lambda b, h, l: (0, 0)),
            pl.BlockSpec(sel.shape, lambda b, h, l: (0, 0)),
        ],
        out_specs=[
            pl.BlockSpec((tl, dk), lambda b, h, l: (b * nl + l, h)),
            pl.BlockSpec((1, 1, dk, dk), lambda b, h, l: (b, h, 0, 0)),
        ],
        out_shape=[
            jax.ShapeDtypeStruct((bsz * seq, d_a), BF16),
            jax.ShapeDtypeStruct((bsz, heads, dk, dk), F32),
        ],
        scratch_shapes=[
            pltpu.VMEM((dk, dk), F32),
            pltpu.VMEM((tl, dk), F32),
            pltpu.VMEM((tl, dk), F32),
            pltpu.VMEM((n_chunks, dk, dk), F32),
            pltpu.VMEM((n_chunks, dk, dk), BF16),
        ],
        compiler_params=_params(("parallel", "parallel", "arbitrary"), est),
        name="hgrn2",
    )(pb, pf, pb, pb, gn, s0, cum, sel)


def _lru_kernel(rx_ref, gy_ref, cw_ref, cb_ref, wax_ref, ba_ref, bx_ref, lam_ref, h0_ref, buf_ref,
                o_ref, hlast_ref, xp_ref, a_ref, u_ref, h_ref, *, tl, n_blocks, bw):
    l = pl.program_id(1)
    keep = CONV_W - 1

    @pl.when(l == 0)
    def _():
        xp_ref[CONV_PAD - keep:CONV_PAD, :] = buf_ref[0]
        h_ref[...] = h0_ref[0]

    x = rx_ref[...]
    xp_ref[CONV_PAD:CONV_PAD + tl, :] = x
    cw = cw_ref[...]
    xc = xp_ref[CONV_PAD - keep:CONV_PAD - keep + tl, :] * cw[0:1, :]
    for j in range(1, CONV_W):
        xc = xc + xp_ref[CONV_PAD - keep + j:CONV_PAD - keep + j + tl, :] * cw[j:j + 1, :]
    xc = xc + cb_ref[...]
    xp_ref[CONV_PAD - keep:CONV_PAD, :] = xp_ref[CONV_PAD + tl - keep:CONV_PAD + tl, :]

    xcb = xc.astype(BF16)
    pre = [_dot(xcb[:, n * bw:(n + 1) * bw], wax_ref[n]) for n in range(n_blocks)]
    r = jax.nn.sigmoid(jnp.concatenate([pn[:, :bw] for pn in pre], axis=-1) + ba_ref[...])
    ig = jax.nn.sigmoid(jnp.concatenate([pn[:, bw:] for pn in pre], axis=-1) + bx_ref[...])
    lam = lam_ref[...]
    softplus_neg = jnp.maximum(-lam, 0.0) + jnp.log1p(jnp.exp(-jnp.abs(lam)))
    log_a = -LRU_C * r * softplus_neg
    a = jnp.exp(log_a)
    mult = jnp.sqrt(jnp.maximum(-jnp.tanh(log_a) * (a * a + 1.0), 0.0))
    a_ref[...] = a
    u_ref[...] = mult * (ig * xc)

    def step(t, h):
        h = a_ref[pl.ds(t, 1), :] * h + u_ref[pl.ds(t, 1), :]
        u_ref[pl.ds(t, 1), :] = h
        return h

    h = lax.fori_loop(0, tl, step, h_ref[...], unroll=8)
    h_ref[...] = h
    o_ref[...] = (u_ref[...] * gy_ref[...].astype(F32)).astype(o_ref.dtype)

    @pl.when(l == pl.num_programs(1) - 1)
    def _():
        hlast_ref[0] = h


def conv_lru(pf, pb, cw, cb, wax, ba, bx, lam, h0, buf, *, bsz, seq, d, tl):
    assert seq % tl == 0 and tl >= CONV_W - 1
    nl = seq // tl
    n_blocks, bw = wax.shape[0], wax.shape[1]
    vec = pl.BlockSpec((1, d), lambda b, l: (0, 0))
    est = 2 * tl * d * (4 + 2) + 2 * tl * d * 2 + (3 * tl + CONV_PAD) * d * 4 + 8 * tl * d * 4
    kern = functools.partial(_lru_kernel, tl=tl, n_blocks=n_blocks, bw=bw)
    return pl.pallas_call(
        kern,
        grid=(bsz, nl),
        in_specs=[
            pl.BlockSpec((tl, d), lambda b, l: (b * nl + l, IN_SEC_RX)),
            pl.BlockSpec((tl, d), lambda b, l: (b * nl + l, IN_SEC_RY - IN_F32_SECTIONS)),
            pl.BlockSpec((CONV_W, d), lambda b, l: (0, 0)),
            vec,
            pl.BlockSpec((n_blocks, bw, 2 * bw), lambda b, l: (0, 0, 0)),
            vec, vec, vec,
            pl.BlockSpec((1, 1, d), lambda b, l: (b, 0, 0)),
            pl.BlockSpec((1, CONV_W - 1, d), lambda b, l: (b, 0, 0)),
        ],
        out_specs=[
            pl.BlockSpec((tl, d), lambda b, l: (b * nl + l, 0)),
            pl.BlockSpec((1, 1, d), lambda b, l: (b, 0, 0)),
        ],
        out_shape=[
            jax.ShapeDtypeStruct((bsz * seq, d), BF16),
            jax.ShapeDtypeStruct((bsz, 1, d), F32),
        ],
        scratch_shapes=[
            pltpu.VMEM((CONV_PAD + tl, d), F32),
            pltpu.VMEM((tl, d), F32),
            pltpu.VMEM((tl, d), F32),
            pltpu.VMEM((1, d), F32),
        ],
        compiler_params=_params(("parallel", "arbitrary"), est),
        name="conv_lru",
    )(pf, pb, cw, cb, wax, ba, bx, lam, h0, buf)


def _mem_attn_kernel(q_ref, k_ref, v_ref, o_ref, *, scale):
    q = q_ref[...]
    k = k_ref[0].astype(BF16)
    v = v_ref[0].astype(BF16)
    s = _dot_nt(q, k) * scale
    s = s - jnp.max(s, axis=-1, keepdims=True)
    p = jnp.exp(s)
    denom = jnp.sum(p, axis=-1, keepdims=True)
    o_ref[...] = (_dot(p.astype(BF16), v) / denom).astype(o_ref.dtype)


def mem_attn(pb, mem_k, mem_v, *, bsz, seq, heads, hd, col0, tl):
    assert seq % tl == 0
    nl = seq // tl
    n_mem = mem_k.shape[1]
    est = 2 * tl * hd * 2 + 4 * n_mem * hd * 4 + 2 * tl * hd * 2 + 4 * tl * n_mem * 4
    kern = functools.partial(_mem_attn_kernel, scale=1.0 / math.sqrt(hd))
    return pl.pallas_call(
        kern,
        grid=(bsz, nl, heads),
        in_specs=[
            pl.BlockSpec((tl, hd), lambda b, l, h: (b * nl + l, col0 + h)),
            pl.BlockSpec((1, n_mem, hd), lambda b, l, h: (b, 0, h)),
            pl.BlockSpec((1, n_mem, hd), lambda b, l, h: (b, 0, h)),
        ],
        out_specs=pl.BlockSpec((tl, hd), lambda b, l, h: (b * nl + l, h)),
        out_shape=jax.ShapeDtypeStruct((bsz * seq, heads * hd), BF16),
        compiler_params=_params(("parallel", "parallel", "parallel"), est),
        name="mem_attn",
    )(pb, mem_k, mem_v)


def _merge_kernel(x_ref, oa_ref, ob_ref, oc_ref, g0_ref, g1_ref, g2_ref,
                  wa_ref, wb_ref, wc_ref, wo_ref, gn_ref, y_ref):
    m = g0_ref[...].astype(F32) * _dot(oa_ref[...], wa_ref[...])
    m = m + g1_ref[...].astype(F32) * _dot(ob_ref[...], wb_ref[...])
    m = m + g2_ref[...].astype(F32) * _dot(oc_ref[...], wc_ref[...])
    z = _dot(m.astype(BF16), wo_ref[...])
    y_ref[...] = x_ref[...] + _rms(z, gn_ref[...])


def merge(x, oa, ob, oc, pb, wa, wb, wc, wo, gn, *, col_gates, tm):
    t, d = x.shape
    db = oa.shape[1]
    assert t % tm == 0
    row = lambda i: (i, 0)
    const = lambda i: (0, 0)
    single = dict(pipeline_mode=pl.Buffered(1))
    gate_specs = [pl.BlockSpec((tm, d), lambda i, k=k: (i, col_gates + k)) for k in range(N_GATES)]
    est = (4 * tm * d * 4 + 6 * tm * db * 2 + 6 * tm * d * 2
           + 3 * db * d * 2 + d * d * 2 + 6 * tm * d * 4)
    return pl.pallas_call(
        _merge_kernel,
        grid=(t // tm,),
        in_specs=[
            pl.BlockSpec((tm, d), row),
            pl.BlockSpec((tm, db), row), pl.BlockSpec((tm, db), row), pl.BlockSpec((tm, db), row),
            *gate_specs,
            pl.BlockSpec((db, d), const, **single),
            pl.BlockSpec((db, d), const, **single),
            pl.BlockSpec((db, d), const, **single),
            pl.BlockSpec((d, d), const, **single),
            pl.BlockSpec((1, d), const),
        ],
        out_specs=pl.BlockSpec((tm, d), row),
        out_shape=jax.ShapeDtypeStruct((t, d), F32),
        compiler_params=_params(("parallel",), est),
        name="merge",
    )(x, oa, ob, oc, pb, pb, pb, wa, wb, wc, wo, gn)


def _ffn_kernel(x_ref, gpre_ref, wg_ref, wu_ref, wd_ref, gpost_ref, y_ref, h_ref, acc_ref):
    j = pl.program_id(1)

    @pl.when(j == 0)
    def _():
        h_ref[...] = _rms(x_ref[...], gpre_ref[...]).astype(BF16)
        acc_ref[...] = jnp.zeros_like(acc_ref)

    h = h_ref[...]
    gt = _dot(h, wg_ref[...])
    up = _dot(h, wu_ref[...])
    act = (jax.nn.silu(gt) * up).astype(BF16)
    acc_ref[...] += _dot(act, wd_ref[...])

    @pl.when(j == pl.num_programs(1) - 1)
    def _():
        y_ref[...] = x_ref[...] + _rms(acc_ref[...], gpost_ref[...])


def ffn(x, gpre, w_gu, w_down, gpost, *, tm, tf):
    t, d = x.shape
    d_ff = w_down.shape[0]
    assert t % tm == 0 and d_ff % tf == 0
    nf = d_ff // tf
    est = 4 * tm * d * 4 + 2 * 3 * d * tf * 2 + tm * d * 2 + tm * d * 4 + 4 * tm * tf * 4
    return pl.pallas_call(
        _ffn_kernel,
        grid=(t // tm, nf),
        in_specs=[
            pl.BlockSpec((tm, d), lambda i, j: (i, 0)),
            pl.BlockSpec((1, d), lambda i, j: (0, 0)),
            pl.BlockSpec((d, tf), lambda i, j: (0, j)),
            pl.BlockSpec((d, tf), lambda i, j: (0, nf + j)),
            pl.BlockSpec((tf, d), lambda i, j: (j, 0)),
            pl.BlockSpec((1, d), lambda i, j: (0, 0)),
        ],
        out_specs=pl.BlockSpec((tm, d), lambda i, j: (i, 0)),
        out_shape=jax.ShapeDtypeStruct((t, d), F32),
        scratch_shapes=[pltpu.VMEM((tm, d), BF16), pltpu.VMEM((tm, d), F32)],
        compiler_params=_params(("parallel", "arbitrary"), est),
        name="ffn",
    )(x, gpre, w_gu, w_gu, w_down, gpost)


def _row_tile(n, target):
    t = min(n, target)
    while n % t:
        t //= 2
    return t


def _regroup_w_in(w, d_a, d_b, d_c):
    assert d_a == d_b == d_c
    s = d_a
    hq, hf, hi, hg, rx, ry, cq = (w[:, k * s:(k + 1) * s] for k in range(7))
    gates = w[:, 7 * s:]
    return jnp.concatenate([hf, rx, hq, hi, hg, ry, gates, cq], axis=1).astype(BF16)


def _trunk_layer(x2, bsz, seq, mem_k, mem_v, s_hg, h_lru, conv_buf, lb, p):
    t, d = x2.shape
    heads, dk = s_hg.shape[1], s_hg.shape[2]
    d_a = heads * dk
    d_b = h_lru.shape[-1]
    mem_heads, hd = p["mem_heads"], p["mem_hd"]
    if seq < CONV_W - 1:
        raise NotImplementedError("sequence shorter than the conv history")

    pf, pb = in_proj(x2, p["norm_pre_mix"], p["w_in"], lb, p["b_gate"], sec=d_a,
                     tm=_row_tile(t, 1024), tn=1024)
    n_bf_sections = pb.shape[1] // d_a

    o_a, s_new = hgrn2(pf, pb, p["hgrn_out_norm"], s_hg, bsz=bsz, seq=seq, heads=heads, dk=dk,
                       tl=_row_tile(seq, 1024))
    o_b, h_last = conv_lru(pf, pb, p["conv_w"], p["conv_b"], p["lru_wax"], p["lru_ba"], p["lru_bx"],
                           p["lru_lambda"], h_lru.reshape(bsz, 1, d_b), conv_buf,
                           bsz=bsz, seq=seq, d=d_b, tl=_row_tile(seq, 256))
    o_c = mem_attn(pb, mem_k, mem_v, bsz=bsz, seq=seq, heads=mem_heads, hd=hd,
                   col0=(n_bf_sections - 1) * d_a // hd, tl=_row_tile(seq, 512))

    gate_col = (IN_SEC_GATES - IN_F32_SECTIONS) * d_a
    assert gate_col % d == 0
    x2 = merge(x2, o_a, o_b, o_c, pb, p["w_branch_a"], p["w_branch_b"], p["w_branch_c"],
               p["w_out"], p["norm_post_mix"], col_gates=gate_col // d, tm=_row_tile(t, 256))
    x2 = ffn(x2, p["norm_pre_ffn"], p["ffn_w_gu"], p["ffn_w_down"], p["norm_post_ffn"],
             tm=_row_tile(t, 512), tf=512)

    rx_tail = pf.reshape(bsz, seq, -1)[:, seq - (CONV_W - 1):, IN_SEC_RX * d_a:(IN_SEC_RX + 1) * d_a]
    return x2, s_new, h_last.reshape(bsz, d_b), rx_tail


def kernel(x_prompt, x_sample, state_hgrn, state_lru, state_conv, cache_mem_k, cache_mem_v, mem_prompt, norm_mem, mem_w_kv, hgrn_lower_bound, norm_pre_mix, w_in, b_gate, hgrn_out_norm, conv_w, conv_b, lru_wa, lru_ba, lru_wx, lru_bx, lru_lambda, w_branch_a, w_branch_b, w_branch_c, w_out, norm_post_mix, norm_pre_ffn, ffn_w_gu, ffn_w_down, norm_post_ffn):
    depth = w_in.shape[0]
    bp, sp, d = x_prompt.shape
    bs, ss, _ = x_sample.shape
    _, _, heads, dk, dv = state_hgrn.shape
    d_a = heads * dk
    d_b = state_lru.shape[-1]
    n_mem, mem_heads, hd = cache_mem_k.shape[2:]
    d_c = mem_heads * hd
    assert dk == dv and d_b == d_a and d_c == d_a and d == 2 * d_a
    assert b_gate.shape[1] == N_GATES * d

    sm = jax.nn.softmax(hgrn_lower_bound.astype(F32), axis=0)
    lbs = jnp.cumsum(sm, axis=0) - sm[0:1]

    xp = x_prompt.reshape(bp * sp, d)
    xs = x_sample.reshape(bs * ss, d)
    mem2 = mem_prompt.reshape(bp * n_mem, d)
    zero_hg = jnp.zeros((bp, heads, dk, dv), F32)
    zero_lru = jnp.zeros((bp, d_b), F32)
    zero_conv = jnp.zeros((bp, CONV_W - 1, d_b), F32)

    outs = {k: [] for k in ("hg_p", "lru_p", "conv_p", "mk_p", "mv_p", "hg_s", "lru_s", "conv_s")}
    for l in range(depth):
        row = lambda a: a[l].reshape(1, -1)
        p = dict(
            mem_heads=mem_heads, mem_hd=hd,
            norm_pre_mix=row(norm_pre_mix), w_in=_regroup_w_in(w_in[l], d_a, d_b, d_c), b_gate=row(b_gate),
            hgrn_out_norm=row(hgrn_out_norm), conv_w=conv_w[l], conv_b=row(conv_b),
            lru_wax=jnp.concatenate([lru_wa[l], lru_wx[l]], axis=-1).astype(BF16),
            lru_ba=row(lru_ba), lru_bx=row(lru_bx), lru_lambda=row(lru_lambda),
            w_branch_a=w_branch_a[l].astype(BF16), w_branch_b=w_branch_b[l].astype(BF16),
            w_branch_c=w_branch_c[l].astype(BF16), w_out=w_out[l].astype(BF16),
            norm_post_mix=row(norm_post_mix), norm_pre_ffn=row(norm_pre_ffn),
            ffn_w_gu=ffn_w_gu[l].astype(BF16), ffn_w_down=ffn_w_down[l].astype(BF16),
            norm_post_ffn=row(norm_post_ffn),
        )
        lb = lbs[l].reshape(1, -1)

        kv = norm_matmul(mem2, row(norm_mem), mem_w_kv[l].astype(BF16),
                         tm=_row_tile(bp * n_mem, 512), tn=1024)
        mk = kv[:, :d_c].reshape(bp, n_mem, d_c)
        mv = kv[:, d_c:].reshape(bp, n_mem, d_c)
        xp, s1, h1, c1 = _trunk_layer(xp, bp, sp, mk, mv, zero_hg, zero_lru, zero_conv, lb, p)
        outs["hg_p"].append(s1); outs["lru_p"].append(h1); outs["conv_p"].append(c1)
        outs["mk_p"].append(mk.reshape(bp, n_mem, mem_heads, hd))
        outs["mv_p"].append(mv.reshape(bp, n_mem, mem_heads, hd))

        xs, s2, h2, c2 = _trunk_layer(
            xs, bs, ss, cache_mem_k[l].reshape(bs, n_mem, d_c), cache_mem_v[l].reshape(bs, n_mem, d_c),
            state_hgrn[l], state_lru[l], state_conv[l], lb, p)
        outs["hg_s"].append(s2); outs["lru_s"].append(h2); outs["conv_s"].append(c2)

    st = {k: jnp.stack(v) for k, v in outs.items()}
    return (xp.reshape(bp, sp, d), xs.reshape(bs, ss, d), st["hg_p"], st["lru_p"], st["conv_p"],
            st["mk_p"], st["mv_p"], st["hg_s"], st["lru_s"], st["conv_s"])
```

```python
import functools
import math

import numpy as np
import jax
import jax.numpy as jnp
from jax import lax
from jax.experimental import pallas as pl
from jax.experimental.pallas import tpu as pltpu

F32 = jnp.float32
BF16 = jnp.bfloat16

EPS = 1e-6
LRU_C = 8.0
CHUNK = 64
SUB = 16
HALF = 8
N_PIECES = 3
CONV_W = 4
CONV_PAD = 8
N_GATES = 3

V7X_VMEM_BYTES = 64 * 1024 * 1024
VMEM_LIMIT_CAP = 56 * 1024 * 1024


def _vmem_limit(estimate_bytes):
    return int(min(VMEM_LIMIT_CAP, max(16 * 1024 * 1024, estimate_bytes * 5 // 4)))


def _params(sem, vmem_estimate):
    return pltpu.CompilerParams(dimension_semantics=sem, vmem_limit_bytes=_vmem_limit(vmem_estimate))


def _rms(x, g):
    ms = jnp.mean(x * x, axis=-1, keepdims=True)
    return x * lax.rsqrt(ms + EPS) * g


def _dot(a, b):
    return jnp.dot(a, b, preferred_element_type=F32)


def _dot_nt(a, b):
    return lax.dot_general(a, b, (((1,), (1,)), ((), ())), preferred_element_type=F32)


def _dot_tn(a, b):
    return lax.dot_general(a, b, (((0,), (0,)), ((), ())), preferred_element_type=F32)


def _cast_kernel(w_ref, o_ref):
    o_ref[...] = w_ref[...].astype(o_ref.dtype)


CAST_BLOCK_BYTES = 4 * 1024 * 1024


def cast_bf16(w):
    depth, r, c = w.shape
    tr = r
    while tr * c * 4 > CAST_BLOCK_BYTES and tr % 32 == 0:
        tr //= 2
    spec = pl.BlockSpec((None, tr, c), lambda l, i: (l, i, 0))
    return pl.pallas_call(
        _cast_kernel,
        grid=(depth, r // tr),
        in_specs=[spec],
        out_specs=spec,
        out_shape=jax.ShapeDtypeStruct(w.shape, BF16),
        compiler_params=_params(("parallel", "parallel"), 2 * tr * c * 6),
        name="cast_bf16",
    )(w)


def _norm_matmul_kernel(x_ref, g_ref, w_ref, o_ref, h_ref):
    @pl.when(pl.program_id(1) == 0)
    def _():
        h_ref[...] = _rms(x_ref[...], g_ref[...]).astype(BF16)

    o_ref[...] = _dot(h_ref[...], w_ref[...]).astype(o_ref.dtype)


def norm_matmul(x, g, w, layer, *, tm, tn, out_dtype=F32):
    t, d = x.shape
    n = w.shape[2]
    assert t % tm == 0 and n % tn == 0
    est = 2 * tm * d * 4 + 2 * d * tn * 2 + 2 * tm * tn * 4 + tm * d * 2
    return pl.pallas_call(
        _norm_matmul_kernel,
        grid=(t // tm, n // tn),
        in_specs=[
            pl.BlockSpec((tm, d), lambda i, j: (i, 0)),
            pl.BlockSpec((1, d), lambda i, j: (0, 0)),
            pl.BlockSpec((None, d, tn), lambda i, j: (layer, 0, j)),
        ],
        out_specs=pl.BlockSpec((tm, tn), lambda i, j: (i, j)),
        out_shape=jax.ShapeDtypeStruct((t, n), out_dtype),
        scratch_shapes=[pltpu.VMEM((tm, d), BF16)],
        compiler_params=_params(("parallel", "arbitrary"), est),
        name="norm_matmul",
    )(x, g, w)


IN_F32_SECTIONS = 2
IN_SEC_LOGF, IN_SEC_RX, IN_SEC_Q, IN_SEC_V, IN_SEC_OG, IN_SEC_RY, IN_SEC_GATES = range(7)


def _in_proj_kernel(perm_ref, x_ref, g_ref, w_ref, lb_ref, bg_ref, of_ref, ob_ref, h_ref,
                    *, tps, n_gate_secs, sub_rows):
    del perm_ref
    j = pl.program_id(1)
    sec = j // tps

    @pl.when(j == 0)
    def _():
        h_ref[...] = _rms(x_ref[...], g_ref[...]).astype(BF16)

    tm = h_ref.shape[0]

    def run(out_ref, act):
        for r in range(tm // sub_rows):
            rs = slice(r * sub_rows, (r + 1) * sub_rows)
            out_ref[rs, :] = act(_dot(h_ref[rs, :], w_ref[...])).astype(out_ref.dtype)

    def log_forget(a):
        lb = lb_ref[...]
        return jnp.log(lb + (1.0 - lb) * jax.nn.sigmoid(a))

    sec_cq = IN_SEC_GATES + n_gate_secs
    pl.when(sec == IN_SEC_LOGF)(lambda: run(of_ref, log_forget))
    pl.when(sec == IN_SEC_RX)(lambda: run(of_ref, lambda a: a))
    pl.when((sec == IN_SEC_Q) | (sec == IN_SEC_OG))(lambda: run(ob_ref, jax.nn.silu))
    pl.when((sec == IN_SEC_V) | (sec == sec_cq))(lambda: run(ob_ref, lambda a: a))
    pl.when(sec == IN_SEC_RY)(lambda: run(ob_ref, jax.nn.gelu))
    pl.when((sec >= IN_SEC_GATES) & (sec < sec_cq))(
        lambda: run(ob_ref, lambda a: jax.nn.sigmoid(a + bg_ref[...])))


def in_proj(x, g, w, layer, lb, bg, *, sec, tm, tn):
    t, d = x.shape
    n = w.shape[2]
    assert t % tm == 0 and sec % tn == 0 and n % sec == 0
    tps = sec // tn
    n_sec = n // sec
    n_gate_secs = bg.shape[1] // sec
    assert n_sec == IN_SEC_GATES + n_gate_secs + 1
    perm = jnp.asarray([1, 4, 0, 2, 3, 5] + list(range(7, 7 + n_gate_secs)) + [6], jnp.int32)
    nf = IN_F32_SECTIONS * tps
    sub_rows = min(tm, 256)
    est = (2 * tm * d * 4 + 2 * d * tn * 2 + 2 * tm * tn * 4 + 2 * tm * tn * 2 + tm * d * 2
           + 6 * sub_rows * tn * 4)
    kern = functools.partial(_in_proj_kernel, tps=tps, n_gate_secs=n_gate_secs, sub_rows=sub_rows)
    grid_spec = pltpu.PrefetchScalarGridSpec(
        num_scalar_prefetch=1,
        grid=(t // tm, n // tn),
        in_specs=[
            pl.BlockSpec((tm, d), lambda i, j, perm: (i, 0)),
            pl.BlockSpec((1, d), lambda i, j, perm: (0, 0)),
            pl.BlockSpec((None, d, tn), lambda i, j, perm: (layer, 0, perm[j // tps] * tps + j % tps)),
            pl.BlockSpec((1, tn), lambda i, j, perm: (0, jnp.minimum(j, tps - 1))),
            pl.BlockSpec((1, tn), lambda i, j, perm: (0, jnp.clip(j - IN_SEC_GATES * tps, 0,
                                                                   n_gate_secs * tps - 1))),
        ],
        out_specs=[
            pl.BlockSpec((tm, tn), lambda i, j, perm: (i, jnp.minimum(j, nf - 1))),
            pl.BlockSpec((tm, tn), lambda i, j, perm: (i, jnp.maximum(j - nf, 0))),
        ],
        scratch_shapes=[pltpu.VMEM((tm, d), BF16)],
    )
    return pl.pallas_call(
        kern,
        grid_spec=grid_spec,
        out_shape=[
            jax.ShapeDtypeStruct((t, IN_F32_SECTIONS * sec), F32),
            jax.ShapeDtypeStruct((t, n - IN_F32_SECTIONS * sec), BF16),
        ],
        compiler_params=_params(("parallel", "arbitrary"), est),
        name="in_proj",
    )(perm, x, g, w, lb, bg)


def _hgrn_consts(chunk, dk):
    n_sub = chunk // SUB
    t = np.arange(chunk)[:, None]
    s = np.arange(chunk)[None, :]
    sub_start = (t // SUB) * SUB
    parts = [s <= t, s <= sub_start + SUB - 1, s <= chunk - 1 + 0 * t, s <= sub_start + HALF - 1]
    parts += [s <= (j + 1) * SUB - 1 + 0 * t for j in range(n_sub - 1)]
    cum = np.concatenate(parts, axis=0).astype(np.float32)
    cum = np.concatenate([cum] * N_PIECES, axis=1)
    lane_blk = np.arange(HALF * dk)[:, None] // dk
    sel = (lane_blk == (np.arange(chunk)[None, :] % HALF)).astype(np.float32)
    return jnp.asarray(cum, BF16), jnp.asarray(sel, BF16)


def _hgrn_kernel(q_ref, lf_ref, v_ref, og_ref, gn_ref, s0_ref, cum_ref, sel_ref,
                 o_ref, sfin_ref, st_ref, b_ref, k_ref, u_ref, sb_ref, *, chunk, n_chunks):
    l = pl.program_id(2)
    n_sub = chunk // SUB
    dk = q_ref.shape[1]

    @pl.when(l == 0)
    def _():
        st_ref[...] = s0_ref[0, 0].T

    q = q_ref[...].astype(F32)
    lf = lf_ref[...]
    kk = 1.0 - jnp.exp(lf)
    vb = v_ref[...]
    k_ref[...] = kk

    p0 = lf.astype(BF16)
    r1 = lf - p0.astype(F32)
    p1 = r1.astype(BF16)
    p2 = (r1 - p1.astype(F32)).astype(BF16)
    cum = cum_ref[...]
    sums = []
    for c in range(n_chunks):
        rows = slice(c * chunk, (c + 1) * chunk)
        sums.append(_dot(cum, jnp.concatenate([p0[rows, :], p1[rows, :], p2[rows, :]], axis=0)))

    def gather(i):
        return jnp.concatenate([s_[i * chunk:(i + 1) * chunk, :] for s_ in sums], axis=0)

    b = gather(0)
    b_end = gather(1)
    b_last = gather(2)
    b_mid = gather(3)
    b_ref[...] = b

    qe = (q * jnp.exp(b)).astype(BF16)
    k_dec = (kk * jnp.exp(b_end - b)).astype(BF16)
    k_end = (kk * jnp.exp(b_last - b)).astype(BF16)
    decay = jnp.exp(b_last)
    q_dec = [(q * jnp.exp(jnp.minimum(b - gather(4 + j), 0.0))).astype(BF16) for j in range(n_sub - 1)]
    q_mid = (q * jnp.exp(jnp.minimum(b - b_mid, 0.0))).astype(BF16)
    k_mid = (kk * jnp.exp(jnp.minimum(b_mid - b, 0.0))).astype(BF16)

    row = lax.broadcasted_iota(jnp.int32, (chunk, chunk), 0)
    col = lax.broadcasted_iota(jnp.int32, (chunk, chunk), 1)
    row_blk = row // SUB
    col_blk = col // SUB
    mid_mask = (col_blk == row_blk) & (row % SUB >= HALF) & (col % SUB < HALF)
    diag_mask = (col // HALF == row // HALF) & (col <= row)
    sel = sel_ref[...]
    n_half = chunk // HALF

    intra = []
    for c in range(n_chunks):
        rows = slice(c * chunk, (c + 1) * chunk)
        qd = jnp.concatenate([qj[rows, :] for qj in q_dec] + [q_mid[rows, :]], axis=0)
        kd = jnp.concatenate([k_dec[rows, :], k_mid[rows, :]], axis=0)
        m = _dot_nt(qd, kd)
        attn = jnp.where(mid_mask, m[(n_sub - 1) * chunk:, chunk:], 0.0)
        for j in range(n_sub - 1):
            attn = jnp.where((col_blk == j) & (row_blk > j), m[j * chunk:(j + 1) * chunk, :chunk], attn)

        q_c = q[rows, :]
        b_c = b[rows, :]
        ws = []
        for u in range(HALF):
            b_s = jnp.concatenate(
                [jnp.broadcast_to(b_ref[pl.ds(c * chunk + j * HALF + u, 1), :], (HALF, dk))
                 for j in range(n_half)], axis=0)
            k_s = jnp.concatenate(
                [jnp.broadcast_to(k_ref[pl.ds(c * chunk + j * HALF + u, 1), :], (HALF, dk))
                 for j in range(n_half)], axis=0)
            ws.append((q_c * jnp.exp(jnp.minimum(b_c - b_s, 0.0)) * k_s).astype(BF16))
        d = _dot(jnp.concatenate(ws, axis=1), sel)
        attn = jnp.where(diag_mask, d, attn)

        intra.append(_dot(attn.astype(BF16), vb[rows, :]))
        u_ref[c] = _dot_tn(vb[rows, :], k_end[rows, :])

    st = st_ref[...]
    for c in range(n_chunks):
        sb_ref[c] = st.astype(BF16)
        st = st * decay[c * chunk:c * chunk + 1, :] + u_ref[c]
    st_ref[...] = st

    inter = [_dot_nt(qe[c * chunk:(c + 1) * chunk, :], sb_ref[c]) for c in range(n_chunks)]
    o = jnp.concatenate(intra, axis=0) + jnp.concatenate(inter, axis=0)
    ms = jnp.mean(o * o, axis=-1, keepdims=True)
    o = o * lax.rsqrt(ms + EPS) * gn_ref[...] * og_ref[...].astype(F32)
    o_ref[...] = o.astype(o_ref.dtype)

    @pl.when(l == pl.num_programs(2) - 1)
    def _():
        sfin_ref[0, 0] = st.T


def hgrn2(pf, pb, gn, s0, layer, *, bsz, seq, heads, dk, tl):
    chunk = min(CHUNK, seq)
    assert seq % tl == 0 and tl % chunk == 0 and chunk % SUB == 0
    nl = seq // tl
    d_a = heads * dk

    def sec(k):
        return pl.BlockSpec((tl, dk), lambda b, h, l, k=k: (b * nl + l, k * heads + h))

    n_chunks = tl // chunk
    cum, sel = _hgrn_consts(chunk, dk)
    est = (2 * tl * dk * (4 + 3 * 2) + 2 * tl * dk * 2 + 6 * dk * dk * 4 + 2 * tl * dk * 4
           + n_chunks * dk * dk * 6 + 2 * (cum.size + sel.size) * 2 + 24 * tl * dk * 4)
    kern = functools.partial(_hgrn_kernel, chunk=chunk, n_chunks=n_chunks)
    return pl.pallas_call(
        kern,
        grid=(bsz, heads, nl),
        in_specs=[
            sec(IN_SEC_Q - IN_F32_SECTIONS), sec(IN_SEC_LOGF),
            sec(IN_SEC_V - IN_F32_SECTIONS), sec(IN_SEC_OG - IN_F32_SECTIONS),
            pl.BlockSpec((1, dk), lambda b, h, l: (0, h)),
            pl.BlockSpec((None, 1, 1, dk, dk), lambda b, h, l: (layer, b, h, 0, 0)),
            pl.BlockSpec(cum.shape, lambda b, h, l: (0, 0)),
            pl.BlockSpec(sel.shape, lambda b, h, l: (0, 0)),
        ],
        out_specs=[
            pl.BlockSpec((tl, dk), lambda b, h, l: (b * nl + l, h)),
            pl.BlockSpec((1, 1, dk, dk), lambda b, h, l: (b, h, 0, 0)),
        ],
        out_shape=[
            jax.ShapeDtypeStruct((bsz * seq, d_a), BF16),
            jax.ShapeDtypeStruct((bsz, heads, dk, dk), F32),
        ],
        scratch_shapes=[
            pltpu.VMEM((dk, dk), F32),
            pltpu.VMEM((tl, dk), F32),
            pltpu.VMEM((tl, dk), F32),
            pltpu.VMEM((n_chunks, dk, dk), F32),
            pltpu.VMEM((n_chunks, dk, dk), BF16),
        ],
        compiler_params=_params(("parallel", "parallel", "arbitrary"), est),
        name="hgrn2",
    )(pb, pf, pb, pb, gn, s0, cum, sel)


def _lru_kernel(rx_ref, gy_ref, cw_ref, cb_ref, wax_ref, ba_ref, bx_ref, lam_ref, h0_ref, buf_ref,
                o_ref, hlast_ref, xp_ref, a_ref, u_ref, h_ref, *, tl, n_blocks, bw):
    l = pl.program_id(1)
    keep = CONV_W - 1

    @pl.when(l == 0)
    def _():
        xp_ref[CONV_PAD - keep:CONV_PAD, :] = buf_ref[0]
        h_ref[...] = h0_ref[0]

    x = rx_ref[...]
    xp_ref[CONV_PAD:CONV_PAD + tl, :] = x
    cw = cw_ref[...]
    xc = xp_ref[CONV_PAD - keep:CONV_PAD - keep + tl, :] * cw[0:1, :]
    for j in range(1, CONV_W):
        xc = xc + xp_ref[CONV_PAD - keep + j:CONV_PAD - keep + j + tl, :] * cw[j:j + 1, :]
    xc = xc + cb_ref[...]
    xp_ref[CONV_PAD - keep:CONV_PAD, :] = xp_ref[CONV_PAD + tl - keep:CONV_PAD + tl, :]

    xcb = xc.astype(BF16)
    pre = [_dot(xcb[:, n * bw:(n + 1) * bw], wax_ref[n]) for n in range(n_blocks)]
    r = jax.nn.sigmoid(jnp.concatenate([pn[:, :bw] for pn in pre], axis=-1) + ba_ref[...])
    ig = jax.nn.sigmoid(jnp.concatenate([pn[:, bw:] for pn in pre], axis=-1) + bx_ref[...])
    lam = lam_ref[...]
    softplus_neg = jnp.maximum(-lam, 0.0) + jnp.log1p(jnp.exp(-jnp.abs(lam)))
    log_a = -LRU_C * r * softplus_neg
    a = jnp.exp(log_a)
    mult = jnp.sqrt(jnp.maximum(-jnp.tanh(log_a) * (a * a + 1.0), 0.0))
    a_ref[...] = a
    u_ref[...] = mult * (ig * xc)

    def step(t, h):
        h = a_ref[pl.ds(t, 1), :] * h + u_ref[pl.ds(t, 1), :]
        u_ref[pl.ds(t, 1), :] = h
        return h

    h = lax.fori_loop(0, tl, step, h_ref[...], unroll=8)
    h_ref[...] = h
    o_ref[...] = (u_ref[...] * gy_ref[...].astype(F32)).astype(o_ref.dtype)

    @pl.when(l == pl.num_programs(1) - 1)
    def _():
        hlast_ref[0] = h


def conv_lru(pf, pb, cw, cb, wax, ba, bx, lam, h0, buf, layer, *, bsz, seq, d, tl):
    assert seq % tl == 0 and tl >= CONV_W - 1
    nl = seq // tl
    n_blocks, bw = wax.shape[0], wax.shape[1]
    vec = pl.BlockSpec((1, d), lambda b, l: (0, 0))
    est = 2 * tl * d * (4 + 2) + 2 * tl * d * 2 + (3 * tl + CONV_PAD) * d * 4 + 8 * tl * d * 4
    kern = functools.partial(_lru_kernel, tl=tl, n_blocks=n_blocks, bw=bw)
    return pl.pallas_call(
        kern,
        grid=(bsz, nl),
        in_specs=[
            pl.BlockSpec((tl, d), lambda b, l: (b * nl + l, IN_SEC_RX)),
            pl.BlockSpec((tl, d), lambda b, l: (b * nl + l, IN_SEC_RY - IN_F32_SECTIONS)),
            pl.BlockSpec((CONV_W, d), lambda b, l: (0, 0)),
            vec,
            pl.BlockSpec((n_blocks, bw, 2 * bw), lambda b, l: (0, 0, 0)),
            vec, vec, vec,
            pl.BlockSpec((None, 1, 1, d), lambda b, l: (layer, b, 0, 0)),
            pl.BlockSpec((None, 1, CONV_W - 1, d), lambda b, l: (layer, b, 0, 0)),
        ],
        out_specs=[
            pl.BlockSpec((tl, d), lambda b, l: (b * nl + l, 0)),
            pl.BlockSpec((1, 1, d), lambda b, l: (b, 0, 0)),
        ],
        out_shape=[
            jax.ShapeDtypeStruct((bsz * seq, d), BF16),
            jax.ShapeDtypeStruct((bsz, 1, d), F32),
        ],
        scratch_shapes=[
            pltpu.VMEM((CONV_PAD + tl, d), F32),
            pltpu.VMEM((tl, d), F32),
            pltpu.VMEM((tl, d), F32),
            pltpu.VMEM((1, d), F32),
        ],
        compiler_params=_params(("parallel", "arbitrary"), est),
        name="conv_lru",
    )(pf, pb, cw, cb, wax, ba, bx, lam, h0, buf)


def _mem_attn_kernel(q_ref, k_ref, v_ref, o_ref, *, scale, heads, hd):
    for h in range(heads):
        cols = slice(h * hd, (h + 1) * hd)
        q = q_ref[:, cols]
        k = k_ref[0, :, cols].astype(BF16)
        v = v_ref[0, :, cols].astype(BF16)
        s = _dot_nt(q, k) * scale
        s = s - jnp.max(s, axis=-1, keepdims=True)
        p = jnp.exp(s)
        denom = jnp.sum(p, axis=-1, keepdims=True)
        o_ref[:, cols] = (_dot(p.astype(BF16), v) / denom).astype(o_ref.dtype)


def mem_attn(pb, mem_k, mem_v, layer, *, bsz, seq, heads, hd, col0, k_col, v_col, tl):
    assert seq % tl == 0
    nl = seq // tl
    n_mem = mem_k.shape[2]
    d_c = heads * hd
    est = 4 * tl * d_c * 2 + 4 * n_mem * d_c * 4 + 6 * tl * n_mem * 4
    kern = functools.partial(_mem_attn_kernel, scale=1.0 / math.sqrt(hd), heads=heads, hd=hd)
    return pl.pallas_call(
        kern,
        grid=(bsz, nl),
        in_specs=[
            pl.BlockSpec((tl, d_c), lambda b, l: (b * nl + l, col0)),
            pl.BlockSpec((None, 1, n_mem, d_c), lambda b, l: (layer, b, 0, k_col)),
            pl.BlockSpec((None, 1, n_mem, d_c), lambda b, l: (layer, b, 0, v_col)),
        ],
        out_specs=pl.BlockSpec((tl, d_c), lambda b, l: (b * nl + l, 0)),
        out_shape=jax.ShapeDtypeStruct((bsz * seq, d_c), BF16),
        compiler_params=_params(("parallel", "parallel"), est),
        name="mem_attn",
    )(pb, mem_k, mem_v)


def _merge_kernel(x_ref, oa_ref, ob_ref, oc_ref, g0_ref, g1_ref, g2_ref,
                  wa_ref, wb_ref, wc_ref, wo_ref, gn_ref, y_ref):
    m = g0_ref[...].astype(F32) * _dot(oa_ref[...], wa_ref[...])
    m = m + g1_ref[...].astype(F32) * _dot(ob_ref[...], wb_ref[...])
    m = m + g2_ref[...].astype(F32) * _dot(oc_ref[...], wc_ref[...])
    z = _dot(m.astype(BF16), wo_ref[...])
    y_ref[...] = x_ref[...] + _rms(z, gn_ref[...])


def merge(x, oa, ob, oc, pb, wa, wb, wc, wo, layer, gn, *, col_gates, tm):
    t, d = x.shape
    db = oa.shape[1]
    assert t % tm == 0
    row = lambda i: (i, 0)
    const = lambda i: (0, 0)
    wspec = lambda rows: pl.BlockSpec((None, rows, d), lambda i: (layer, 0, 0), pipeline_mode=pl.Buffered(1))
    gate_specs = [pl.BlockSpec((tm, d), lambda i, k=k: (i, col_gates + k)) for k in range(N_GATES)]
    est = (4 * tm * d * 4 + 6 * tm * db * 2 + 6 * tm * d * 2
           + 3 * db * d * 2 + d * d * 2 + 6 * tm * d * 4)
    return pl.pallas_call(
        _merge_kernel,
        grid=(t // tm,),
        in_specs=[
            pl.BlockSpec((tm, d), row),
            pl.BlockSpec((tm, db), row), pl.BlockSpec((tm, db), row), pl.BlockSpec((tm, db), row),
            *gate_specs,
            wspec(db), wspec(db), wspec(db), wspec(d),
            pl.BlockSpec((1, d), const),
        ],
        out_specs=pl.BlockSpec((tm, d), row),
        out_shape=jax.ShapeDtypeStruct((t, d), F32),
        compiler_params=_params(("parallel",), est),
        name="merge",
    )(x, oa, ob, oc, pb, pb, pb, wa, wb, wc, wo, gn)


def _ffn_kernel(x_ref, gpre_ref, wg_ref, wu_ref, wd_ref, gpost_ref, y_ref, h_ref, *, sub_rows):
    j = pl.program_id(1)
    tm = h_ref.shape[0]

    @pl.when(j == 0)
    def _():
        h_ref[...] = _rms(x_ref[...], gpre_ref[...]).astype(BF16)

    def partial_down(rs):
        h = h_ref[rs, :]
        act = (jax.nn.silu(_dot(h, wg_ref[...])) * _dot(h, wu_ref[...])).astype(BF16)
        return _dot(act, wd_ref[...])

    subs = [slice(r * sub_rows, (r + 1) * sub_rows) for r in range(tm // sub_rows)]

    @pl.when(j == 0)
    def _():
        for rs in subs:
            y_ref[rs, :] = partial_down(rs)

    @pl.when(j > 0)
    def _():
        for rs in subs:
            y_ref[rs, :] += partial_down(rs)

    @pl.when(j == pl.num_programs(1) - 1)
    def _():
        y_ref[...] = x_ref[...] + _rms(y_ref[...], gpost_ref[...])


def ffn(x, gpre, w_gu, w_down, layer, gpost, *, tm, tf):
    t, d = x.shape
    d_ff = w_down.shape[1]
    assert t % tm == 0 and d_ff % tf == 0
    nf = d_ff // tf
    sub_rows = min(tm, 256)
    est = 3 * tm * d * 4 + 2 * 3 * d * tf * 2 + tm * d * 2 + 4 * sub_rows * (tf + d) * 4
    return pl.pallas_call(
        functools.partial(_ffn_kernel, sub_rows=sub_rows),
        grid=(t // tm, nf),
        in_specs=[
            pl.BlockSpec((tm, d), lambda i, j: (i, 0), pipeline_mode=pl.Buffered(1)),
            pl.BlockSpec((1, d), lambda i, j: (0, 0)),
            pl.BlockSpec((None, d, tf), lambda i, j: (layer, 0, j)),
            pl.BlockSpec((None, d, tf), lambda i, j: (layer, 0, nf + j)),
            pl.BlockSpec((None, tf, d), lambda i, j: (layer, j, 0)),
            pl.BlockSpec((1, d), lambda i, j: (0, 0)),
        ],
        out_specs=pl.BlockSpec((tm, d), lambda i, j: (i, 0)),
        out_shape=jax.ShapeDtypeStruct((t, d), F32),
        scratch_shapes=[pltpu.VMEM((tm, d), BF16)],
        compiler_params=_params(("parallel", "arbitrary"), est),
        name="ffn",
    )(x, gpre, w_gu, w_gu, w_down, gpost)


def _row_tile(n, target):
    t = min(n, target)
    while n % t:
        t //= 2
    return t


def _trunk_layer(x2, bsz, seq, mem, state, lb, w, p, layer):
    t, d = x2.shape
    s_hg, h_lru, conv_buf, state_layer = state
    mem_k, mem_v, mem_layer, k_col, v_col = mem
    heads, dk = s_hg.shape[2], s_hg.shape[3]
    d_a = heads * dk
    d_b = h_lru.shape[-1]
    mem_heads, hd = p["mem_heads"], p["mem_hd"]
    if seq < CONV_W - 1:
        raise NotImplementedError("sequence shorter than the conv history")

    pf, pb = in_proj(x2, p["norm_pre_mix"], w["w_in"], layer, lb, p["b_gate"], sec=d_a,
                     tm=_row_tile(t, 1024), tn=1024)
    n_bf_sections = pb.shape[1] // d_a

    o_a, s_new = hgrn2(pf, pb, p["hgrn_out_norm"], s_hg, state_layer, bsz=bsz, seq=seq, heads=heads, dk=dk,
                       tl=_row_tile(seq, 1024))
    o_b, h_last = conv_lru(pf, pb, p["conv_w"], p["conv_b"], p["lru_wax"], p["lru_ba"], p["lru_bx"],
                           p["lru_lambda"], h_lru, conv_buf, state_layer,
                           bsz=bsz, seq=seq, d=d_b, tl=_row_tile(seq, 256))
    o_c = mem_attn(pb, mem_k, mem_v, mem_layer, bsz=bsz, seq=seq, heads=mem_heads, hd=hd,
                   col0=n_bf_sections - 1, k_col=k_col, v_col=v_col, tl=_row_tile(seq, 1024))

    gate_col = (IN_SEC_GATES - IN_F32_SECTIONS) * d_a
    assert gate_col % d == 0
    x2 = merge(x2, o_a, o_b, o_c, pb, w["w_branch_a"], w["w_branch_b"], w["w_branch_c"], w["w_out"], layer,
               p["norm_post_mix"], col_gates=gate_col // d, tm=_row_tile(t, 256))
    x2 = ffn(x2, p["norm_pre_ffn"], w["ffn_w_gu"], w["ffn_w_down"], layer, p["norm_post_ffn"],
             tm=_row_tile(t, 1024), tf=512)

    rx_tail = pf.reshape(bsz, seq, -1)[:, seq - (CONV_W - 1):, IN_SEC_RX * d_a:(IN_SEC_RX + 1) * d_a]
    return x2, s_new, h_last.reshape(bsz, d_b), rx_tail


def kernel(x_prompt, x_sample, state_hgrn, state_lru, state_conv, cache_mem_k, cache_mem_v, mem_prompt, norm_mem, mem_w_kv, hgrn_lower_bound, norm_pre_mix, w_in, b_gate, hgrn_out_norm, conv_w, conv_b, lru_wa, lru_ba, lru_wx, lru_bx, lru_lambda, w_branch_a, w_branch_b, w_branch_c, w_out, norm_post_mix, norm_pre_ffn, ffn_w_gu, ffn_w_down, norm_post_ffn):
    depth = w_in.shape[0]
    bp, sp, d = x_prompt.shape
    bs, ss, _ = x_sample.shape
    _, _, heads, dk, dv = state_hgrn.shape
    d_a = heads * dk
    d_b = state_lru.shape[-1]
    n_mem, mem_heads, hd = cache_mem_k.shape[2:]
    d_c = mem_heads * hd
    assert dk == dv and d_b == d_a and d_c == d_a and d == 2 * d_a
    assert b_gate.shape[1] == N_GATES * d

    sm = jax.nn.softmax(hgrn_lower_bound.astype(F32), axis=0)
    lbs = jnp.cumsum(sm, axis=0) - sm[0:1]

    xp = x_prompt.reshape(bp * sp, d)
    xs = x_sample.reshape(bs * ss, d)
    mem2 = mem_prompt.reshape(bp * n_mem, d)
    zero_state = (jnp.zeros((1, bp, heads, dk, dv), F32), jnp.zeros((1, bp, 1, d_b), F32),
                  jnp.zeros((1, bp, CONV_W - 1, d_b), F32), 0)
    cache_k = cache_mem_k.reshape(depth, bs, n_mem, d_c)
    cache_v = cache_mem_v.reshape(depth, bs, n_mem, d_c)
    lru_s4 = state_lru.reshape(depth, bs, 1, d_b)

    w = dict(w_in=cast_bf16(w_in), w_branch_a=cast_bf16(w_branch_a), w_branch_b=cast_bf16(w_branch_b),
             w_branch_c=cast_bf16(w_branch_c), w_out=cast_bf16(w_out), ffn_w_gu=cast_bf16(ffn_w_gu),
             ffn_w_down=cast_bf16(ffn_w_down), mem_w_kv=cast_bf16(mem_w_kv))

    outs = {k: [] for k in ("hg_p", "lru_p", "conv_p", "mk_p", "mv_p", "hg_s", "lru_s", "conv_s")}
    for l in range(depth):
        row = lambda a: a[l].reshape(1, -1)
        p = dict(
            mem_heads=mem_heads, mem_hd=hd,
            norm_pre_mix=row(norm_pre_mix), b_gate=row(b_gate),
            hgrn_out_norm=row(hgrn_out_norm), conv_w=conv_w[l], conv_b=row(conv_b),
            lru_wax=jnp.concatenate([lru_wa[l], lru_wx[l]], axis=-1).astype(BF16),
            lru_ba=row(lru_ba), lru_bx=row(lru_bx), lru_lambda=row(lru_lambda),
            norm_post_mix=row(norm_post_mix), norm_pre_ffn=row(norm_pre_ffn),
            norm_post_ffn=row(norm_post_ffn),
        )
        lb = lbs[l].reshape(1, -1)

        kv = norm_matmul(mem2, row(norm_mem), w["mem_w_kv"], l, tm=_row_tile(bp * n_mem, 512), tn=1024)
        kv4 = kv.reshape(1, bp, n_mem, 2 * d_c)
        xp, s1, h1, c1 = _trunk_layer(xp, bp, sp, (kv4, kv4, 0, 0, 1), zero_state, lb, w, p, l)
        outs["hg_p"].append(s1); outs["lru_p"].append(h1); outs["conv_p"].append(c1)
        outs["mk_p"].append(kv4[0, :, :, :d_c].reshape(bp, n_mem, mem_heads, hd))
        outs["mv_p"].append(kv4[0, :, :, d_c:].reshape(bp, n_mem, mem_heads, hd))

        xs, s2, h2, c2 = _trunk_layer(xs, bs, ss, (cache_k, cache_v, l, 0, 0),
                                      (state_hgrn, lru_s4, state_conv, l), lb, w, p, l)
        outs["hg_s"].append(s2); outs["lru_s"].append(h2); outs["conv_s"].append(c2)

    st = {k: jnp.stack(v) for k, v in outs.items()}
    return (xp.reshape(bp, sp, d), xs.reshape(bs, ss, d), st["hg_p"], st["lru_p"], st["conv_p"],
            st["mk_p"], st["mv_p"], st["hg_s"], st["lru_s"], st["conv_s"])
```

```python
import functools
import math

import numpy as np
import jax
import jax.numpy as jnp
from jax import lax
from jax.experimental import pallas as pl
from jax.experimental.pallas import tpu as pltpu

F32 = jnp.float32
BF16 = jnp.bfloat16

EPS = 1e-6
LRU_C = 8.0
CHUNK = 64
SUB = 16
HALF = 8
N_PIECES = 3
CONV_W = 4
CONV_PAD = 8
N_GATES = 3

V7X_VMEM_BYTES = 64 * 1024 * 1024
VMEM_LIMIT_CAP = 56 * 1024 * 1024


def _vmem_limit(estimate_bytes):
    return int(min(VMEM_LIMIT_CAP, max(16 * 1024 * 1024, estimate_bytes * 5 // 4)))


def _params(sem, vmem_estimate):
    return pltpu.CompilerParams(dimension_semantics=sem, vmem_limit_bytes=_vmem_limit(vmem_estimate))


def _rms(x, g):
    ms = jnp.mean(x * x, axis=-1, keepdims=True)
    return x * lax.rsqrt(ms + EPS) * g


def _dot(a, b):
    return jnp.dot(a, b, preferred_element_type=F32)


def _dot_nt(a, b):
    return lax.dot_general(a, b, (((1,), (1,)), ((), ())), preferred_element_type=F32)


def _dot_tn(a, b):
    return lax.dot_general(a, b, (((0,), (0,)), ((), ())), preferred_element_type=F32)


def _cast_kernel(w_ref, o_ref):
    o_ref[...] = w_ref[...].astype(o_ref.dtype)


CAST_BLOCK_BYTES = 4 * 1024 * 1024


def cast_bf16(w):
    depth, r, c = w.shape
    tr = r
    while tr * c * 4 > CAST_BLOCK_BYTES and tr % 32 == 0:
        tr //= 2
    spec = pl.BlockSpec((None, tr, c), lambda l, i: (l, i, 0))
    return pl.pallas_call(
        _cast_kernel,
        grid=(depth, r // tr),
        in_specs=[spec],
        out_specs=spec,
        out_shape=jax.ShapeDtypeStruct(w.shape, BF16),
        compiler_params=_params(("parallel", "parallel"), 2 * tr * c * 6),
        name="cast_bf16",
    )(w)


def _norm_matmul_kernel(x_ref, g_ref, w_ref, o_ref, h_ref):
    @pl.when(pl.program_id(1) == 0)
    def _():
        h_ref[...] = _rms(x_ref[...], g_ref[...]).astype(BF16)

    o_ref[...] = _dot(h_ref[...], w_ref[...]).astype(o_ref.dtype)


def norm_matmul(x, g, w, layer, *, tm, tn, out_dtype=F32):
    t, d = x.shape
    n = w.shape[2]
    assert t % tm == 0 and n % tn == 0
    est = 2 * tm * d * 4 + 2 * d * tn * 2 + 2 * tm * tn * 4 + tm * d * 2
    return pl.pallas_call(
        _norm_matmul_kernel,
        grid=(t // tm, n // tn),
        in_specs=[
            pl.BlockSpec((tm, d), lambda i, j: (i, 0)),
            pl.BlockSpec((1, d), lambda i, j: (0, 0)),
            pl.BlockSpec((None, d, tn), lambda i, j: (layer, 0, j)),
        ],
        out_specs=pl.BlockSpec((tm, tn), lambda i, j: (i, j)),
        out_shape=jax.ShapeDtypeStruct((t, n), out_dtype),
        scratch_shapes=[pltpu.VMEM((tm, d), BF16)],
        compiler_params=_params(("parallel", "arbitrary"), est),
        name="norm_matmul",
    )(x, g, w)


IN_F32_SECTIONS = 2
IN_SEC_LOGF, IN_SEC_RX, IN_SEC_Q, IN_SEC_V, IN_SEC_OG, IN_SEC_RY, IN_SEC_GATES = range(7)


def _in_proj_kernel(perm_ref, x_ref, g_ref, w_ref, lb_ref, bg_ref, of_ref, ob_ref, h_ref,
                    *, tps, n_gate_secs, sub_rows):
    del perm_ref
    j = pl.program_id(1)
    sec = j // tps

    @pl.when(j == 0)
    def _():
        h_ref[...] = _rms(x_ref[...], g_ref[...]).astype(BF16)

    tm = h_ref.shape[0]

    def run(out_ref, act):
        for r in range(tm // sub_rows):
            rs = slice(r * sub_rows, (r + 1) * sub_rows)
            out_ref[rs, :] = act(_dot(h_ref[rs, :], w_ref[...])).astype(out_ref.dtype)

    def log_forget(a):
        lb = lb_ref[...]
        return jnp.log(lb + (1.0 - lb) * jax.nn.sigmoid(a))

    sec_cq = IN_SEC_GATES + n_gate_secs
    pl.when(sec == IN_SEC_LOGF)(lambda: run(of_ref, log_forget))
    pl.when(sec == IN_SEC_RX)(lambda: run(of_ref, lambda a: a))
    pl.when((sec == IN_SEC_Q) | (sec == IN_SEC_OG))(lambda: run(ob_ref, jax.nn.silu))
    pl.when((sec == IN_SEC_V) | (sec == sec_cq))(lambda: run(ob_ref, lambda a: a))
    pl.when(sec == IN_SEC_RY)(lambda: run(ob_ref, jax.nn.gelu))
    pl.when((sec >= IN_SEC_GATES) & (sec < sec_cq))(
        lambda: run(ob_ref, lambda a: jax.nn.sigmoid(a + bg_ref[...])))


def in_proj(x, g, w, layer, lb, bg, *, sec, tm, tn):
    t, d = x.shape
    n = w.shape[2]
    assert t % tm == 0 and sec % tn == 0 and n % sec == 0
    tps = sec // tn
    n_sec = n // sec
    n_gate_secs = bg.shape[1] // sec
    assert n_sec == IN_SEC_GATES + n_gate_secs + 1
    perm = jnp.asarray([1, 4, 0, 2, 3, 5] + list(range(7, 7 + n_gate_secs)) + [6], jnp.int32)
    nf = IN_F32_SECTIONS * tps
    sub_rows = min(tm, 256)
    est = (2 * tm * d * 4 + 2 * d * tn * 2 + 2 * tm * tn * 4 + 2 * tm * tn * 2 + tm * d * 2
           + 6 * sub_rows * tn * 4)
    kern = functools.partial(_in_proj_kernel, tps=tps, n_gate_secs=n_gate_secs, sub_rows=sub_rows)
    grid_spec = pltpu.PrefetchScalarGridSpec(
        num_scalar_prefetch=1,
        grid=(t // tm, n // tn),
        in_specs=[
            pl.BlockSpec((tm, d), lambda i, j, perm: (i, 0)),
            pl.BlockSpec((1, d), lambda i, j, perm: (0, 0)),
            pl.BlockSpec((None, d, tn), lambda i, j, perm: (layer, 0, perm[j // tps] * tps + j % tps)),
            pl.BlockSpec((1, tn), lambda i, j, perm: (0, jnp.minimum(j, tps - 1))),
            pl.BlockSpec((1, tn), lambda i, j, perm: (0, jnp.clip(j - IN_SEC_GATES * tps, 0,
                                                                   n_gate_secs * tps - 1))),
        ],
        out_specs=[
            pl.BlockSpec((tm, tn), lambda i, j, perm: (i, jnp.minimum(j, nf - 1))),
            pl.BlockSpec((tm, tn), lambda i, j, perm: (i, jnp.maximum(j - nf, 0))),
        ],
        scratch_shapes=[pltpu.VMEM((tm, d), BF16)],
    )
    return pl.pallas_call(
        kern,
        grid_spec=grid_spec,
        out_shape=[
            jax.ShapeDtypeStruct((t, IN_F32_SECTIONS * sec), F32),
            jax.ShapeDtypeStruct((t, n - IN_F32_SECTIONS * sec), BF16),
        ],
        compiler_params=_params(("parallel", "arbitrary"), est),
        name="in_proj",
    )(perm, x, g, w, lb, bg)


def _hgrn_consts(chunk, dk):
    t = np.arange(chunk)[:, None]
    s = np.arange(chunk)[None, :]
    cum = np.concatenate([(s <= t).astype(np.float32)] * N_PIECES, axis=1)
    lane_blk = np.arange(HALF * dk)[:, None] // dk
    sel = (lane_blk == (np.arange(chunk)[None, :] % HALF)).astype(np.float32)
    return jnp.asarray(cum, BF16), jnp.asarray(sel, BF16)


def _hgrn_kernel(q_ref, lf_ref, v_ref, og_ref, gn_ref, s0_ref, cum_ref, sel_ref,
                 o_ref, sfin_ref, st_ref, b_ref, k_ref, u_ref, sb_ref, *, chunk, n_chunks):
    l = pl.program_id(2)
    n_sub = chunk // SUB
    dk = q_ref.shape[1]

    @pl.when(l == 0)
    def _():
        st_ref[...] = s0_ref[0, 0].T

    q = q_ref[...].astype(F32)
    lf = lf_ref[...]
    kk = 1.0 - jnp.exp(lf)
    vb = v_ref[...]
    k_ref[...] = kk

    p0 = lf.astype(BF16)
    r1 = lf - p0.astype(F32)
    p1 = r1.astype(BF16)
    p2 = (r1 - p1.astype(F32)).astype(BF16)
    cum = cum_ref[...]
    b = jnp.concatenate(
        [_dot(cum, jnp.concatenate([p[c * chunk:(c + 1) * chunk, :] for p in (p0, p1, p2)], axis=0))
         for c in range(n_chunks)], axis=0)
    b_ref[...] = b
    tl = n_chunks * chunk

    def rows_of(ref, group, offset):
        return jnp.concatenate(
            [jnp.broadcast_to(ref[pl.ds(g * group + offset, 1), :], (group, dk)) for g in range(tl // group)],
            axis=0)

    b_end = rows_of(b_ref, SUB, SUB - 1)
    b_mid = rows_of(b_ref, SUB, HALF - 1)
    b_last = rows_of(b_ref, chunk, chunk - 1)

    qe = (q * jnp.exp(b)).astype(BF16)
    k_dec = (kk * jnp.exp(b_end - b)).astype(BF16)
    k_end = (kk * jnp.exp(b_last - b)).astype(BF16)
    decay = jnp.exp(b_last)
    q_dec = [(q * jnp.exp(jnp.minimum(b - rows_of(b_ref, chunk, (j + 1) * SUB - 1), 0.0))).astype(BF16)
             for j in range(n_sub - 1)]
    q_mid = (q * jnp.exp(jnp.minimum(b - b_mid, 0.0))).astype(BF16)
    k_mid = (kk * jnp.exp(jnp.minimum(b_mid - b, 0.0))).astype(BF16)

    w = jnp.concatenate(
        [(q * jnp.exp(jnp.minimum(b - rows_of(b_ref, HALF, u), 0.0)) * rows_of(k_ref, HALF, u)).astype(BF16)
         for u in range(HALF)], axis=1)
    diag = _dot(w, sel_ref[...])

    row = lax.broadcasted_iota(jnp.int32, (chunk, chunk), 0)
    col = lax.broadcasted_iota(jnp.int32, (chunk, chunk), 1)
    row_blk = row // SUB
    col_blk = col // SUB
    mid_mask = (col_blk == row_blk) & (row % SUB >= HALF) & (col % SUB < HALF)
    diag_mask = (col // HALF == row // HALF) & (col <= row)
    eye = (lax.broadcasted_iota(jnp.int32, (dk, dk), 0)
           == lax.broadcasted_iota(jnp.int32, (dk, dk), 1)).astype(F32).astype(BF16)

    chunks = [slice(c * chunk, (c + 1) * chunk) for c in range(n_chunks)]
    cross, v_t = [], []
    for rows in chunks:
        qd = jnp.concatenate([qj[rows, :] for qj in q_dec] + [q_mid[rows, :]], axis=0)
        kd = jnp.concatenate([k_dec[rows, :], k_mid[rows, :]], axis=0)
        cross.append(_dot_nt(qd, kd))
        v_t.append(_dot_nt(eye, vb[rows, :]).astype(BF16))

    intra = []
    for c, rows in enumerate(chunks):
        m = cross[c]
        attn = jnp.where(mid_mask, m[(n_sub - 1) * chunk:, chunk:], 0.0)
        for j in range(n_sub - 1):
            attn = jnp.where((col_blk == j) & (row_blk > j), m[j * chunk:(j + 1) * chunk, :chunk], attn)
        attn = jnp.where(diag_mask, diag[rows, :], attn)
        intra.append(_dot(attn.astype(BF16), vb[rows, :]))
        u_ref[c] = _dot(v_t[c], k_end[rows, :])

    st = st_ref[...]
    for c in range(n_chunks):
        sb_ref[c] = st.astype(BF16)
        st = st * decay[c * chunk:c * chunk + 1, :] + u_ref[c]
    st_ref[...] = st

    inter = [_dot_nt(qe[c * chunk:(c + 1) * chunk, :], sb_ref[c]) for c in range(n_chunks)]
    o = jnp.concatenate(intra, axis=0) + jnp.concatenate(inter, axis=0)
    ms = jnp.mean(o * o, axis=-1, keepdims=True)
    o = o * lax.rsqrt(ms + EPS) * gn_ref[...] * og_ref[...].astype(F32)
    o_ref[...] = o.astype(o_ref.dtype)

    @pl.when(l == pl.num_programs(2) - 1)
    def _():
        sfin_ref[0, 0] = st.T


def hgrn2(pf, pb, gn, s0, layer, *, bsz, seq, heads, dk, tl):
    chunk = min(CHUNK, seq)
    assert seq % tl == 0 and tl % chunk == 0 and chunk % SUB == 0
    nl = seq // tl
    d_a = heads * dk

    def sec(k):
        return pl.BlockSpec((tl, dk), lambda b, h, l, k=k: (b * nl + l, k * heads + h))

    n_chunks = tl // chunk
    cum, sel = _hgrn_consts(chunk, dk)
    est = (2 * tl * dk * (4 + 3 * 2) + 2 * tl * dk * 2 + 6 * dk * dk * 4 + 2 * tl * dk * 4
           + n_chunks * dk * dk * 6 + 2 * (cum.size + sel.size) * 2 + 24 * tl * dk * 4)
    kern = functools.partial(_hgrn_kernel, chunk=chunk, n_chunks=n_chunks)
    return pl.pallas_call(
        kern,
        grid=(bsz, heads, nl),
        in_specs=[
            sec(IN_SEC_Q - IN_F32_SECTIONS), sec(IN_SEC_LOGF),
            sec(IN_SEC_V - IN_F32_SECTIONS), sec(IN_SEC_OG - IN_F32_SECTIONS),
            pl.BlockSpec((1, dk), lambda b, h, l: (0, h)),
            pl.BlockSpec((None, 1, 1, dk, dk), lambda b, h, l: (layer, b, h, 0, 0)),
            pl.BlockSpec(cum.shape, lambda b, h, l: (0, 0)),
            pl.BlockSpec(sel.shape, lambda b, h, l: (0, 0)),
        ],
        out_specs=[
            pl.BlockSpec((tl, dk), lambda b, h, l: (b * nl + l, h)),
            pl.BlockSpec((1, 1, dk, dk), lambda b, h, l: (b, h, 0, 0)),
        ],
        out_shape=[
            jax.ShapeDtypeStruct((bsz * seq, d_a), BF16),
            jax.ShapeDtypeStruct((bsz, heads, dk, dk), F32),
        ],
        scratch_shapes=[
            pltpu.VMEM((dk, dk), F32),
            pltpu.VMEM((tl, dk), F32),
            pltpu.VMEM((tl, dk), F32),
            pltpu.VMEM((n_chunks, dk, dk), F32),
            pltpu.VMEM((n_chunks, dk, dk), BF16),
        ],
        compiler_params=_params(("parallel", "parallel", "arbitrary"), est),
        name="hgrn2",
    )(pb, pf, pb, pb, gn, s0, cum, sel)


def _lru_kernel(rx_ref, gy_ref, cw_ref, cb_ref, wax_ref, ba_ref, bx_ref, lam_ref, h0_ref, buf_ref,
                o_ref, hlast_ref, xp_ref, a_ref, u_ref, h_ref, *, tl, n_blocks, bw):
    l = pl.program_id(1)
    keep = CONV_W - 1

    @pl.when(l == 0)
    def _():
        xp_ref[CONV_PAD - keep:CONV_PAD, :] = buf_ref[0]
        h_ref[...] = h0_ref[0]

    x = rx_ref[...]
    xp_ref[CONV_PAD:CONV_PAD + tl, :] = x
    cw = cw_ref[...]
    xc = xp_ref[CONV_PAD - keep:CONV_PAD - keep + tl, :] * cw[0:1, :]
    for j in range(1, CONV_W):
        xc = xc + xp_ref[CONV_PAD - keep + j:CONV_PAD - keep + j + tl, :] * cw[j:j + 1, :]
    xc = xc + cb_ref[...]
    xp_ref[CONV_PAD - keep:CONV_PAD, :] = xp_ref[CONV_PAD + tl - keep:CONV_PAD + tl, :]

    xcb = xc.astype(BF16)
    pre = [_dot(xcb[:, n * bw:(n + 1) * bw], wax_ref[n]) for n in range(n_blocks)]
    r = jax.nn.sigmoid(jnp.concatenate([pn[:, :bw] for pn in pre], axis=-1) + ba_ref[...])
    ig = jax.nn.sigmoid(jnp.concatenate([pn[:, bw:] for pn in pre], axis=-1) + bx_ref[...])
    lam = lam_ref[...]
    softplus_neg = jnp.maximum(-lam, 0.0) + jnp.log1p(jnp.exp(-jnp.abs(lam)))
    log_a = -LRU_C * r * softplus_neg
    a = jnp.exp(log_a)
    mult = jnp.sqrt(jnp.maximum(-jnp.tanh(log_a) * (a * a + 1.0), 0.0))
    a_ref[...] = a
    u_ref[...] = mult * (ig * xc)

    def step(t, h):
        h = a_ref[pl.ds(t, 1), :] * h + u_ref[pl.ds(t, 1), :]
        u_ref[pl.ds(t, 1), :] = h
        return h

    h = lax.fori_loop(0, tl, step, h_ref[...], unroll=8)
    h_ref[...] = h
    o_ref[...] = (u_ref[...] * gy_ref[...].astype(F32)).astype(o_ref.dtype)

    @pl.when(l == pl.num_programs(1) - 1)
    def _():
        hlast_ref[0] = h


def conv_lru(pf, pb, cw, cb, wax, ba, bx, lam, h0, buf, layer, *, bsz, seq, d, tl):
    assert seq % tl == 0 and tl >= CONV_W - 1
    nl = seq // tl
    n_blocks, bw = wax.shape[0], wax.shape[1]
    vec = pl.BlockSpec((1, d), lambda b, l: (0, 0))
    est = 2 * tl * d * (4 + 2) + 2 * tl * d * 2 + (3 * tl + CONV_PAD) * d * 4 + 8 * tl * d * 4
    kern = functools.partial(_lru_kernel, tl=tl, n_blocks=n_blocks, bw=bw)
    return pl.pallas_call(
        kern,
        grid=(bsz, nl),
        in_specs=[
            pl.BlockSpec((tl, d), lambda b, l: (b * nl + l, IN_SEC_RX)),
            pl.BlockSpec((tl, d), lambda b, l: (b * nl + l, IN_SEC_RY - IN_F32_SECTIONS)),
            pl.BlockSpec((CONV_W, d), lambda b, l: (0, 0)),
            vec,
            pl.BlockSpec((n_blocks, bw, 2 * bw), lambda b, l: (0, 0, 0)),
            vec, vec, vec,
            pl.BlockSpec((None, 1, 1, d), lambda b, l: (layer, b, 0, 0)),
            pl.BlockSpec((None, 1, CONV_W - 1, d), lambda b, l: (layer, b, 0, 0)),
        ],
        out_specs=[
            pl.BlockSpec((tl, d), lambda b, l: (b * nl + l, 0)),
            pl.BlockSpec((1, 1, d), lambda b, l: (b, 0, 0)),
        ],
        out_shape=[
            jax.ShapeDtypeStruct((bsz * seq, d), BF16),
            jax.ShapeDtypeStruct((bsz, 1, d), F32),
        ],
        scratch_shapes=[
            pltpu.VMEM((CONV_PAD + tl, d), F32),
            pltpu.VMEM((tl, d), F32),
            pltpu.VMEM((tl, d), F32),
            pltpu.VMEM((1, d), F32),
        ],
        compiler_params=_params(("parallel", "arbitrary"), est),
        name="conv_lru",
    )(pf, pb, cw, cb, wax, ba, bx, lam, h0, buf)


def _mem_attn_kernel(q_ref, k_ref, v_ref, o_ref, *, scale, heads, hd):
    for h in range(heads):
        cols = slice(h * hd, (h + 1) * hd)
        q = q_ref[:, cols]
        k = k_ref[0, :, cols].astype(BF16)
        v = v_ref[0, :, cols].astype(BF16)
        s = _dot_nt(q, k) * scale
        s = s - jnp.max(s, axis=-1, keepdims=True)
        p = jnp.exp(s)
        denom = jnp.sum(p, axis=-1, keepdims=True)
        o_ref[:, cols] = (_dot(p.astype(BF16), v) / denom).astype(o_ref.dtype)


def mem_attn(pb, mem_k, mem_v, layer, *, bsz, seq, heads, hd, col0, k_col, v_col, tl):
    assert seq % tl == 0
    nl = seq // tl
    n_mem = mem_k.shape[2]
    d_c = heads * hd
    est = 4 * tl * d_c * 2 + 4 * n_mem * d_c * 4 + 6 * tl * n_mem * 4
    kern = functools.partial(_mem_attn_kernel, scale=1.0 / math.sqrt(hd), heads=heads, hd=hd)
    return pl.pallas_call(
        kern,
        grid=(bsz, nl),
        in_specs=[
            pl.BlockSpec((tl, d_c), lambda b, l: (b * nl + l, col0)),
            pl.BlockSpec((None, 1, n_mem, d_c), lambda b, l: (layer, b, 0, k_col)),
            pl.BlockSpec((None, 1, n_mem, d_c), lambda b, l: (layer, b, 0, v_col)),
        ],
        out_specs=pl.BlockSpec((tl, d_c), lambda b, l: (b * nl + l, 0)),
        out_shape=jax.ShapeDtypeStruct((bsz * seq, d_c), BF16),
        compiler_params=_params(("parallel", "parallel"), est),
        name="mem_attn",
    )(pb, mem_k, mem_v)


def _merge_kernel(x_ref, oa_ref, ob_ref, oc_ref, g0_ref, g1_ref, g2_ref,
                  wa_ref, wb_ref, wc_ref, wo_ref, gn_ref, y_ref):
    m = g0_ref[...].astype(F32) * _dot(oa_ref[...], wa_ref[...])
    m = m + g1_ref[...].astype(F32) * _dot(ob_ref[...], wb_ref[...])
    m = m + g2_ref[...].astype(F32) * _dot(oc_ref[...], wc_ref[...])
    z = _dot(m.astype(BF16), wo_ref[...])
    y_ref[...] = x_ref[...] + _rms(z, gn_ref[...])


def merge(x, oa, ob, oc, pb, wa, wb, wc, wo, layer, gn, *, col_gates, tm):
    t, d = x.shape
    db = oa.shape[1]
    assert t % tm == 0
    row = lambda i: (i, 0)
    const = lambda i: (0, 0)
    wspec = lambda rows: pl.BlockSpec((None, rows, d), lambda i: (layer, 0, 0), pipeline_mode=pl.Buffered(1))
    gate_specs = [pl.BlockSpec((tm, d), lambda i, k=k: (i, col_gates + k)) for k in range(N_GATES)]
    est = (4 * tm * d * 4 + 6 * tm * db * 2 + 6 * tm * d * 2
           + 3 * db * d * 2 + d * d * 2 + 6 * tm * d * 4)
    return pl.pallas_call(
        _merge_kernel,
        grid=(t // tm,),
        in_specs=[
            pl.BlockSpec((tm, d), row),
            pl.BlockSpec((tm, db), row), pl.BlockSpec((tm, db), row), pl.BlockSpec((tm, db), row),
            *gate_specs,
            wspec(db), wspec(db), wspec(db), wspec(d),
            pl.BlockSpec((1, d), const),
        ],
        out_specs=pl.BlockSpec((tm, d), row),
        out_shape=jax.ShapeDtypeStruct((t, d), F32),
        compiler_params=_params(("parallel",), est),
        name="merge",
    )(x, oa, ob, oc, pb, pb, pb, wa, wb, wc, wo, gn)


def _ffn_kernel(x_ref, gpre_ref, wg_ref, wu_ref, wd_ref, gpost_ref, y_ref, h_ref, acc_ref):
    j = pl.program_id(1)

    @pl.when(j == 0)
    def _():
        h_ref[...] = _rms(x_ref[...], gpre_ref[...]).astype(BF16)
        acc_ref[...] = jnp.zeros_like(acc_ref)

    h = h_ref[...]
    gt = _dot(h, wg_ref[...])
    up = _dot(h, wu_ref[...])
    act = (jax.nn.silu(gt) * up).astype(BF16)
    acc_ref[...] += _dot(act, wd_ref[...])

    @pl.when(j == pl.num_programs(1) - 1)
    def _():
        y_ref[...] = x_ref[...] + _rms(acc_ref[...], gpost_ref[...])


def ffn(x, gpre, w_gu, w_down, layer, gpost, *, tm, tf):
    t, d = x.shape
    d_ff = w_down.shape[1]
    assert t % tm == 0 and d_ff % tf == 0
    nf = d_ff // tf
    est = 4 * tm * d * 4 + 2 * 3 * d * tf * 2 + tm * d * 2 + tm * d * 4 + 4 * tm * tf * 4
    return pl.pallas_call(
        _ffn_kernel,
        grid=(t // tm, nf),
        in_specs=[
            pl.BlockSpec((tm, d), lambda i, j: (i, 0)),
            pl.BlockSpec((1, d), lambda i, j: (0, 0)),
            pl.BlockSpec((None, d, tf), lambda i, j: (layer, 0, j)),
            pl.BlockSpec((None, d, tf), lambda i, j: (layer, 0, nf + j)),
            pl.BlockSpec((None, tf, d), lambda i, j: (layer, j, 0)),
            pl.BlockSpec((1, d), lambda i, j: (0, 0)),
        ],
        out_specs=pl.BlockSpec((tm, d), lambda i, j: (i, 0)),
        out_shape=jax.ShapeDtypeStruct((t, d), F32),
        scratch_shapes=[pltpu.VMEM((tm, d), BF16), pltpu.VMEM((tm, d), F32)],
        compiler_params=_params(("parallel", "arbitrary"), est),
        name="ffn",
    )(x, gpre, w_gu, w_gu, w_down, gpost)


def _row_tile(n, target):
    t = min(n, target)
    while n % t:
        t //= 2
    return t


def _trunk_layer(x2, bsz, seq, mem, state, lb, w, p, layer):
    t, d = x2.shape
    s_hg, h_lru, conv_buf, state_layer = state
    mem_k, mem_v, mem_layer, k_col, v_col = mem
    heads, dk = s_hg.shape[2], s_hg.shape[3]
    d_a = heads * dk
    d_b = h_lru.shape[-1]
    mem_heads, hd = p["mem_heads"], p["mem_hd"]
    if seq < CONV_W - 1:
        raise NotImplementedError("sequence shorter than the conv history")

    pf, pb = in_proj(x2, p["norm_pre_mix"], w["w_in"], layer, lb, p["b_gate"], sec=d_a,
                     tm=_row_tile(t, 1024), tn=1024)
    n_bf_sections = pb.shape[1] // d_a

    o_a, s_new = hgrn2(pf, pb, p["hgrn_out_norm"], s_hg, state_layer, bsz=bsz, seq=seq, heads=heads, dk=dk,
                       tl=_row_tile(seq, 1024))
    o_b, h_last = conv_lru(pf, pb, p["conv_w"], p["conv_b"], p["lru_wax"], p["lru_ba"], p["lru_bx"],
                           p["lru_lambda"], h_lru, conv_buf, state_layer,
                           bsz=bsz, seq=seq, d=d_b, tl=_row_tile(seq, 256))
    o_c = mem_attn(pb, mem_k, mem_v, mem_layer, bsz=bsz, seq=seq, heads=mem_heads, hd=hd,
                   col0=n_bf_sections - 1, k_col=k_col, v_col=v_col, tl=_row_tile(seq, 1024))

    gate_col = (IN_SEC_GATES - IN_F32_SECTIONS) * d_a
    assert gate_col % d == 0
    x2 = merge(x2, o_a, o_b, o_c, pb, w["w_branch_a"], w["w_branch_b"], w["w_branch_c"], w["w_out"], layer,
               p["norm_post_mix"], col_gates=gate_col // d, tm=_row_tile(t, 256))
    x2 = ffn(x2, p["norm_pre_ffn"], w["ffn_w_gu"], w["ffn_w_down"], layer, p["norm_post_ffn"],
             tm=_row_tile(t, 512), tf=512)

    rx_tail = pf.reshape(bsz, seq, -1)[:, seq - (CONV_W - 1):, IN_SEC_RX * d_a:(IN_SEC_RX + 1) * d_a]
    return x2, s_new, h_last.reshape(bsz, d_b), rx_tail


def kernel(x_prompt, x_sample, state_hgrn, state_lru, state_conv, cache_mem_k, cache_mem_v, mem_prompt, norm_mem, mem_w_kv, hgrn_lower_bound, norm_pre_mix, w_in, b_gate, hgrn_out_norm, conv_w, conv_b, lru_wa, lru_ba, lru_wx, lru_bx, lru_lambda, w_branch_a, w_branch_b, w_branch_c, w_out, norm_post_mix, norm_pre_ffn, ffn_w_gu, ffn_w_down, norm_post_ffn):
    depth = w_in.shape[0]
    bp, sp, d = x_prompt.shape
    bs, ss, _ = x_sample.shape
    _, _, heads, dk, dv = state_hgrn.shape
    d_a = heads * dk
    d_b = state_lru.shape[-1]
    n_mem, mem_heads, hd = cache_mem_k.shape[2:]
    d_c = mem_heads * hd
    assert dk == dv and d_b == d_a and d_c == d_a and d == 2 * d_a
    assert b_gate.shape[1] == N_GATES * d

    sm = jax.nn.softmax(hgrn_lower_bound.astype(F32), axis=0)
    lbs = jnp.cumsum(sm, axis=0) - sm[0:1]

    xp = x_prompt.reshape(bp * sp, d)
    xs = x_sample.reshape(bs * ss, d)
    mem2 = mem_prompt.reshape(bp * n_mem, d)
    zero_state = (jnp.zeros((1, bp, heads, dk, dv), F32), jnp.zeros((1, bp, 1, d_b), F32),
                  jnp.zeros((1, bp, CONV_W - 1, d_b), F32), 0)
    cache_k = cache_mem_k.reshape(depth, bs, n_mem, d_c)
    cache_v = cache_mem_v.reshape(depth, bs, n_mem, d_c)
    lru_s4 = state_lru.reshape(depth, bs, 1, d_b)

    w = dict(w_in=cast_bf16(w_in), w_branch_a=cast_bf16(w_branch_a), w_branch_b=cast_bf16(w_branch_b),
             w_branch_c=cast_bf16(w_branch_c), w_out=cast_bf16(w_out), ffn_w_gu=cast_bf16(ffn_w_gu),
             ffn_w_down=cast_bf16(ffn_w_down), mem_w_kv=cast_bf16(mem_w_kv))

    outs = {k: [] for k in ("hg_p", "lru_p", "conv_p", "mk_p", "mv_p", "hg_s", "lru_s", "conv_s")}
    for l in range(depth):
        row = lambda a: a[l].reshape(1, -1)
        p = dict(
            mem_heads=mem_heads, mem_hd=hd,
            norm_pre_mix=row(norm_pre_mix), b_gate=row(b_gate),
            hgrn_out_norm=row(hgrn_out_norm), conv_w=conv_w[l], conv_b=row(conv_b),
            lru_wax=jnp.concatenate([lru_wa[l], lru_wx[l]], axis=-1).astype(BF16),
            lru_ba=row(lru_ba), lru_bx=row(lru_bx), lru_lambda=row(lru_lambda),
            norm_post_mix=row(norm_post_mix), norm_pre_ffn=row(norm_pre_ffn),
            norm_post_ffn=row(norm_post_ffn),
        )
        lb = lbs[l].reshape(1, -1)

        kv = norm_matmul(mem2, row(norm_mem), w["mem_w_kv"], l, tm=_row_tile(bp * n_mem, 512), tn=1024)
        kv4 = kv.reshape(1, bp, n_mem, 2 * d_c)
        xp, s1, h1, c1 = _trunk_layer(xp, bp, sp, (kv4, kv4, 0, 0, 1), zero_state, lb, w, p, l)
        outs["hg_p"].append(s1); outs["lru_p"].append(h1); outs["conv_p"].append(c1)
        outs["mk_p"].append(kv4[0, :, :, :d_c].reshape(bp, n_mem, mem_heads, hd))
        outs["mv_p"].append(kv4[0, :, :, d_c:].reshape(bp, n_mem, mem_heads, hd))

        xs, s2, h2, c2 = _trunk_layer(xs, bs, ss, (cache_k, cache_v, l, 0, 0),
                                      (state_hgrn, lru_s4, state_conv, l), lb, w, p, l)
        outs["hg_s"].append(s2); outs["lru_s"].append(h2); outs["conv_s"].append(c2)

    st = {k: jnp.stack(v) for k, v in outs.items()}
    return (xp.reshape(bp, sp, d), xs.reshape(bs, ss, d), st["hg_p"], st["lru_p"], st["conv_p"],
            st["mk_p"], st["mv_p"], st["hg_s"], st["lru_s"], st["conv_s"])
```

```python
import functools
import math

import numpy as np
import jax
import jax.numpy as jnp
from jax import lax
from jax.experimental import pallas as pl
from jax.experimental.pallas import tpu as pltpu

F32 = jnp.float32
BF16 = jnp.bfloat16

EPS = 1e-6
LRU_C = 8.0
CHUNK = 64
SUB = 16
HALF = 8
N_PIECES = 3
CONV_W = 4
CONV_PAD = 8
LRU_SEGMENTS = 4
N_GATES = 3

V7X_VMEM_BYTES = 64 * 1024 * 1024
VMEM_LIMIT_CAP = 56 * 1024 * 1024


def _vmem_limit(estimate_bytes):
    return int(min(VMEM_LIMIT_CAP, max(16 * 1024 * 1024, estimate_bytes * 5 // 4)))


def _params(sem, vmem_estimate):
    return pltpu.CompilerParams(dimension_semantics=sem, vmem_limit_bytes=_vmem_limit(vmem_estimate))


def _rms(x, g):
    ms = jnp.mean(x * x, axis=-1, keepdims=True)
    return x * lax.rsqrt(ms + EPS) * g


def _dot(a, b):
    return jnp.dot(a, b, preferred_element_type=F32)


def _dot_nt(a, b):
    return lax.dot_general(a, b, (((1,), (1,)), ((), ())), preferred_element_type=F32)


def _dot_tn(a, b):
    return lax.dot_general(a, b, (((0,), (0,)), ((), ())), preferred_element_type=F32)


def _cast_kernel(w_ref, o_ref):
    o_ref[...] = w_ref[...].astype(o_ref.dtype)


CAST_BLOCK_BYTES = 4 * 1024 * 1024


def cast_bf16(w):
    depth, r, c = w.shape
    tr = r
    while tr * c * 4 > CAST_BLOCK_BYTES and tr % 32 == 0:
        tr //= 2
    spec = pl.BlockSpec((None, tr, c), lambda l, i: (l, i, 0))
    return pl.pallas_call(
        _cast_kernel,
        grid=(depth, r // tr),
        in_specs=[spec],
        out_specs=spec,
        out_shape=jax.ShapeDtypeStruct(w.shape, BF16),
        compiler_params=_params(("parallel", "parallel"), 2 * tr * c * 6),
        name="cast_bf16",
    )(w)


def _norm_matmul_kernel(x_ref, g_ref, w_ref, o_ref, h_ref):
    @pl.when(pl.program_id(1) == 0)
    def _():
        h_ref[...] = _rms(x_ref[...], g_ref[...]).astype(BF16)

    o_ref[...] = _dot(h_ref[...], w_ref[...]).astype(o_ref.dtype)


def norm_matmul(x, g, w, layer, *, tm, tn, out_dtype=F32):
    t, d = x.shape
    n = w.shape[2]
    assert t % tm == 0 and n % tn == 0
    est = 2 * tm * d * 4 + 2 * d * tn * 2 + 2 * tm * tn * 4 + tm * d * 2
    return pl.pallas_call(
        _norm_matmul_kernel,
        grid=(t // tm, n // tn),
        in_specs=[
            pl.BlockSpec((tm, d), lambda i, j: (i, 0)),
            pl.BlockSpec((1, d), lambda i, j: (0, 0)),
            pl.BlockSpec((None, d, tn), lambda i, j: (layer, 0, j)),
        ],
        out_specs=pl.BlockSpec((tm, tn), lambda i, j: (i, j)),
        out_shape=jax.ShapeDtypeStruct((t, n), out_dtype),
        scratch_shapes=[pltpu.VMEM((tm, d), BF16)],
        compiler_params=_params(("parallel", "arbitrary"), est),
        name="norm_matmul",
    )(x, g, w)


IN_F32_SECTIONS = 2
IN_SEC_LOGF, IN_SEC_RX, IN_SEC_Q, IN_SEC_V, IN_SEC_OG, IN_SEC_RY, IN_SEC_GATES = range(7)


def _in_proj_kernel(perm_ref, x_ref, g_ref, w_ref, lb_ref, bg_ref, of_ref, ob_ref, h_ref,
                    *, tps, n_gate_secs, sub_rows):
    del perm_ref
    j = pl.program_id(1)
    sec = j // tps

    @pl.when(j == 0)
    def _():
        h_ref[...] = _rms(x_ref[...], g_ref[...]).astype(BF16)

    tm = h_ref.shape[0]

    def run(out_ref, act):
        for r in range(tm // sub_rows):
            rs = slice(r * sub_rows, (r + 1) * sub_rows)
            out_ref[rs, :] = act(_dot(h_ref[rs, :], w_ref[...])).astype(out_ref.dtype)

    def log_forget(a):
        lb = lb_ref[...]
        return jnp.log(lb + (1.0 - lb) * jax.nn.sigmoid(a))

    sec_cq = IN_SEC_GATES + n_gate_secs
    pl.when(sec == IN_SEC_LOGF)(lambda: run(of_ref, log_forget))
    pl.when(sec == IN_SEC_RX)(lambda: run(of_ref, lambda a: a))
    pl.when((sec == IN_SEC_Q) | (sec == IN_SEC_OG))(lambda: run(ob_ref, jax.nn.silu))
    pl.when((sec == IN_SEC_V) | (sec == sec_cq))(lambda: run(ob_ref, lambda a: a))
    pl.when(sec == IN_SEC_RY)(lambda: run(ob_ref, jax.nn.gelu))
    pl.when((sec >= IN_SEC_GATES) & (sec < sec_cq))(
        lambda: run(ob_ref, lambda a: jax.nn.sigmoid(a + bg_ref[...])))


def in_proj(x, g, w, layer, lb, bg, *, sec, tm, tn):
    t, d = x.shape
    n = w.shape[2]
    assert t % tm == 0 and sec % tn == 0 and n % sec == 0
    tps = sec // tn
    n_sec = n // sec
    n_gate_secs = bg.shape[1] // sec
    assert n_sec == IN_SEC_GATES + n_gate_secs + 1
    perm = jnp.asarray([1, 4, 0, 2, 3, 5] + list(range(7, 7 + n_gate_secs)) + [6], jnp.int32)
    nf = IN_F32_SECTIONS * tps
    sub_rows = min(tm, 256)
    est = (2 * tm * d * 4 + 2 * d * tn * 2 + 2 * tm * tn * 4 + 2 * tm * tn * 2 + tm * d * 2
           + 6 * sub_rows * tn * 4)
    kern = functools.partial(_in_proj_kernel, tps=tps, n_gate_secs=n_gate_secs, sub_rows=sub_rows)
    grid_spec = pltpu.PrefetchScalarGridSpec(
        num_scalar_prefetch=1,
        grid=(t // tm, n // tn),
        in_specs=[
            pl.BlockSpec((tm, d), lambda i, j, perm: (i, 0)),
            pl.BlockSpec((1, d), lambda i, j, perm: (0, 0)),
            pl.BlockSpec((None, d, tn), lambda i, j, perm: (layer, 0, perm[j // tps] * tps + j % tps)),
            pl.BlockSpec((1, tn), lambda i, j, perm: (0, jnp.minimum(j, tps - 1))),
            pl.BlockSpec((1, tn), lambda i, j, perm: (0, jnp.clip(j - IN_SEC_GATES * tps, 0,
                                                                   n_gate_secs * tps - 1))),
        ],
        out_specs=[
            pl.BlockSpec((tm, tn), lambda i, j, perm: (i, jnp.minimum(j, nf - 1))),
            pl.BlockSpec((tm, tn), lambda i, j, perm: (i, jnp.maximum(j - nf, 0))),
        ],
        scratch_shapes=[pltpu.VMEM((tm, d), BF16)],
    )
    return pl.pallas_call(
        kern,
        grid_spec=grid_spec,
        out_shape=[
            jax.ShapeDtypeStruct((t, IN_F32_SECTIONS * sec), F32),
            jax.ShapeDtypeStruct((t, n - IN_F32_SECTIONS * sec), BF16),
        ],
        compiler_params=_params(("parallel", "arbitrary"), est),
        name="in_proj",
    )(perm, x, g, w, lb, bg)


def _hgrn_consts(chunk, dk):
    t = np.arange(chunk)[:, None]
    s = np.arange(chunk)[None, :]
    cum = np.concatenate([(s <= t).astype(np.float32)] * N_PIECES, axis=1)
    lane_blk = np.arange(HALF * dk)[:, None] // dk
    sel = (lane_blk == (np.arange(chunk)[None, :] % HALF)).astype(np.float32)
    return jnp.asarray(cum, BF16), jnp.asarray(sel, BF16)


def _hgrn_kernel(q_ref, lf_ref, v_ref, og_ref, gn_ref, s0_ref, cum_ref, sel_ref,
                 o_ref, sfin_ref, st_ref, b_ref, k_ref, u_ref, sb_ref, *, chunk, n_chunks, dk):
    l = pl.program_id(2)
    n_sub = chunk // SUB
    width = q_ref.shape[1]
    n_heads = width // dk
    heads = [slice(h * dk, (h + 1) * dk) for h in range(n_heads)]

    @pl.when(l == 0)
    def _():
        for h in range(n_heads):
            st_ref[h] = s0_ref[0, h].T

    q = q_ref[...].astype(F32)
    lf = lf_ref[...]
    kk = 1.0 - jnp.exp(lf)
    vb = v_ref[...]
    k_ref[...] = kk

    p0 = lf.astype(BF16)
    r1 = lf - p0.astype(F32)
    p1 = r1.astype(BF16)
    p2 = (r1 - p1.astype(F32)).astype(BF16)
    cum = cum_ref[...]
    b = jnp.concatenate(
        [_dot(cum, jnp.concatenate([p[c * chunk:(c + 1) * chunk, :] for p in (p0, p1, p2)], axis=0))
         for c in range(n_chunks)], axis=0)
    b_ref[...] = b
    tl = n_chunks * chunk

    def rows_of(ref, group, offset):
        return jnp.concatenate(
            [jnp.broadcast_to(ref[pl.ds(g * group + offset, 1), :], (group, width)) for g in range(tl // group)],
            axis=0)

    b_end = rows_of(b_ref, SUB, SUB - 1)
    b_mid = rows_of(b_ref, SUB, HALF - 1)
    b_last = rows_of(b_ref, chunk, chunk - 1)

    qe = (q * jnp.exp(b)).astype(BF16)
    k_dec = (kk * jnp.exp(b_end - b)).astype(BF16)
    k_end = (kk * jnp.exp(b_last - b)).astype(BF16)
    decay = jnp.exp(b_last)
    q_dec = [(q * jnp.exp(jnp.minimum(b - rows_of(b_ref, chunk, (j + 1) * SUB - 1), 0.0))).astype(BF16)
             for j in range(n_sub - 1)]
    q_mid = (q * jnp.exp(jnp.minimum(b - b_mid, 0.0))).astype(BF16)
    k_mid = (kk * jnp.exp(jnp.minimum(b_mid - b, 0.0))).astype(BF16)

    w = [(q * jnp.exp(jnp.minimum(b - rows_of(b_ref, HALF, u), 0.0)) * rows_of(k_ref, HALF, u)).astype(BF16)
         for u in range(HALF)]
    sel = sel_ref[...]
    diag = [_dot(jnp.concatenate([wu[:, hc] for wu in w], axis=1), sel) for hc in heads]

    row = lax.broadcasted_iota(jnp.int32, (chunk, chunk), 0)
    col = lax.broadcasted_iota(jnp.int32, (chunk, chunk), 1)
    row_blk = row // SUB
    col_blk = col // SUB
    mid_mask = (col_blk == row_blk) & (row % SUB >= HALF) & (col % SUB < HALF)
    diag_mask = (col // HALF == row // HALF) & (col <= row)
    eye = (lax.broadcasted_iota(jnp.int32, (dk, dk), 0)
           == lax.broadcasted_iota(jnp.int32, (dk, dk), 1)).astype(F32).astype(BF16)

    chunks = [slice(c * chunk, (c + 1) * chunk) for c in range(n_chunks)]
    units = [(h, hc, c, rows) for h, hc in enumerate(heads) for c, rows in enumerate(chunks)]
    cross, v_t = [], []
    for h, hc, c, rows in units:
        qd = jnp.concatenate([qj[rows, hc] for qj in q_dec] + [q_mid[rows, hc]], axis=0)
        kd = jnp.concatenate([k_dec[rows, hc], k_mid[rows, hc]], axis=0)
        cross.append(_dot_nt(qd, kd))
        v_t.append(_dot_nt(eye, vb[rows, hc]).astype(BF16))

    intra = []
    for i, (h, hc, c, rows) in enumerate(units):
        m = cross[i]
        attn = jnp.where(mid_mask, m[(n_sub - 1) * chunk:, chunk:], 0.0)
        for j in range(n_sub - 1):
            attn = jnp.where((col_blk == j) & (row_blk > j), m[j * chunk:(j + 1) * chunk, :chunk], attn)
        attn = jnp.where(diag_mask, diag[h][rows, :], attn)
        intra.append(_dot(attn.astype(BF16), vb[rows, hc]))
        u_ref[i] = _dot(v_t[i], k_end[rows, hc])

    st = [st_ref[h] for h in range(n_heads)]
    for i, (h, hc, c, rows) in enumerate(units):
        sb_ref[i] = st[h].astype(BF16)
        st[h] = st[h] * decay[c * chunk:c * chunk + 1, hc] + u_ref[i]
    for h in range(n_heads):
        st_ref[h] = st[h]

    inter = [_dot_nt(qe[rows, hc], sb_ref[i]) for i, (h, hc, c, rows) in enumerate(units)]
    gn = gn_ref[...]
    for h, hc in enumerate(heads):
        per_chunk = range(h * n_chunks, (h + 1) * n_chunks)
        o = jnp.concatenate([intra[i] + inter[i] for i in per_chunk], axis=0)
        ms = jnp.mean(o * o, axis=-1, keepdims=True)
        o = o * lax.rsqrt(ms + EPS) * gn[:, hc] * og_ref[:, hc].astype(F32)
        o_ref[:, hc] = o.astype(o_ref.dtype)

    @pl.when(l == pl.num_programs(2) - 1)
    def _():
        for h in range(n_heads):
            sfin_ref[0, h] = st[h].T


HGRN_UNITS_PER_STEP = 16


def hgrn2(pf, pb, gn, s0, layer, *, bsz, seq, heads, dk, tl):
    chunk = min(CHUNK, seq)
    assert seq % tl == 0 and tl % chunk == 0 and chunk % SUB == 0
    nl = seq // tl
    d_a = heads * dk
    n_chunks = tl // chunk
    hb = max(1, min(heads, HGRN_UNITS_PER_STEP // n_chunks))
    while heads % hb:
        hb -= 1
    n_groups = heads // hb
    width = hb * dk
    n_units = hb * n_chunks

    def sec(k):
        return pl.BlockSpec((tl, width), lambda b, h, l, k=k: (b * nl + l, k * n_groups + h))

    cum, sel = _hgrn_consts(chunk, dk)
    est = (2 * tl * width * (4 + 3 * 2) + 2 * tl * width * 2 + 6 * hb * dk * dk * 4 + 2 * tl * width * 4
           + n_units * dk * dk * 6 + 2 * (cum.size + sel.size) * 2 + 24 * tl * width * 4)
    kern = functools.partial(_hgrn_kernel, chunk=chunk, n_chunks=n_chunks, dk=dk)
    return pl.pallas_call(
        kern,
        grid=(bsz, n_groups, nl),
        in_specs=[
            sec(IN_SEC_Q - IN_F32_SECTIONS), sec(IN_SEC_LOGF),
            sec(IN_SEC_V - IN_F32_SECTIONS), sec(IN_SEC_OG - IN_F32_SECTIONS),
            pl.BlockSpec((1, width), lambda b, h, l: (0, h)),
            pl.BlockSpec((None, 1, hb, dk, dk), lambda b, h, l: (layer, b, h, 0, 0)),
            pl.BlockSpec(cum.shape, lambda b, h, l: (0, 0)),
            pl.BlockSpec(sel.shape, lambda b, h, l: (0, 0)),
        ],
        out_specs=[
            pl.BlockSpec((tl, width), lambda b, h, l: (b * nl + l, h)),
            pl.BlockSpec((1, hb, dk, dk), lambda b, h, l: (b, h, 0, 0)),
        ],
        out_shape=[
            jax.ShapeDtypeStruct((bsz * seq, d_a), BF16),
            jax.ShapeDtypeStruct((bsz, heads, dk, dk), F32),
        ],
        scratch_shapes=[
            pltpu.VMEM((hb, dk, dk), F32),
            pltpu.VMEM((tl, width), F32),
            pltpu.VMEM((tl, width), F32),
            pltpu.VMEM((n_units, dk, dk), F32),
            pltpu.VMEM((n_units, dk, dk), BF16),
        ],
        compiler_params=_params(("parallel", "parallel", "arbitrary"), est),
        name="hgrn2",
    )(pb, pf, pb, pb, gn, s0, cum, sel)


def _lru_kernel(rx_ref, gy_ref, cw_ref, cb_ref, wax_ref, ba_ref, bx_ref, lam_ref, h0_ref, buf_ref,
                o_ref, hlast_ref, xp_ref, a_ref, u_ref, hs_ref, ps_ref, h_ref, *, tl, n_blocks, bw):
    l = pl.program_id(1)
    keep = CONV_W - 1

    @pl.when(l == 0)
    def _():
        xp_ref[CONV_PAD - keep:CONV_PAD, :] = buf_ref[0]
        h_ref[...] = h0_ref[0]

    x = rx_ref[...]
    xp_ref[CONV_PAD:CONV_PAD + tl, :] = x
    cw = cw_ref[...]
    xc = xp_ref[CONV_PAD - keep:CONV_PAD - keep + tl, :] * cw[0:1, :]
    for j in range(1, CONV_W):
        xc = xc + xp_ref[CONV_PAD - keep + j:CONV_PAD - keep + j + tl, :] * cw[j:j + 1, :]
    xc = xc + cb_ref[...]
    xp_ref[CONV_PAD - keep:CONV_PAD, :] = xp_ref[CONV_PAD + tl - keep:CONV_PAD + tl, :]

    xcb = xc.astype(BF16)
    pre = [_dot(xcb[:, n * bw:(n + 1) * bw], wax_ref[n]) for n in range(n_blocks)]
    r = jax.nn.sigmoid(jnp.concatenate([pn[:, :bw] for pn in pre], axis=-1) + ba_ref[...])
    ig = jax.nn.sigmoid(jnp.concatenate([pn[:, bw:] for pn in pre], axis=-1) + bx_ref[...])
    lam = lam_ref[...]
    softplus_neg = jnp.maximum(-lam, 0.0) + jnp.log1p(jnp.exp(-jnp.abs(lam)))
    log_a = -LRU_C * r * softplus_neg
    a = jnp.exp(log_a)
    mult = jnp.sqrt(jnp.maximum(-jnp.tanh(log_a) * (a * a + 1.0), 0.0))
    a_ref[...] = a
    u_ref[...] = mult * (ig * xc)

    seg = tl // LRU_SEGMENTS
    one = jnp.ones_like(h_ref[...])

    def step(t, carry):
        hs, ps = carry
        new_h, new_p = [], []
        for s in range(LRU_SEGMENTS):
            r = s * seg + t
            a_t = a_ref[pl.ds(r, 1), :]
            h_s = a_t * hs[s] + u_ref[pl.ds(r, 1), :]
            hs_ref[pl.ds(r, 1), :] = h_s
            new_h.append(h_s)
            if s > 0:
                p_s = a_t * ps[s - 1]
                ps_ref[pl.ds(r, 1), :] = p_s
                new_p.append(p_s)
        return tuple(new_h), tuple(new_p)

    init = ((h_ref[...],) + (jnp.zeros_like(one),) * (LRU_SEGMENTS - 1), (one,) * (LRU_SEGMENTS - 1))
    hs, ps = lax.fori_loop(0, seg, step, init, unroll=min(seg, 4))
    h = hs[0]
    gy = gy_ref[...].astype(F32)
    o_ref[0:seg, :] = (hs_ref[0:seg, :] * gy[0:seg, :]).astype(o_ref.dtype)
    for s in range(1, LRU_SEGMENTS):
        rows = slice(s * seg, (s + 1) * seg)
        o_ref[rows, :] = ((hs_ref[rows, :] + ps_ref[rows, :] * h) * gy[rows, :]).astype(o_ref.dtype)
        h = hs[s] + ps[s - 1] * h
    h_ref[...] = h

    @pl.when(l == pl.num_programs(1) - 1)
    def _():
        hlast_ref[0] = h


def conv_lru(pf, pb, cw, cb, wax, ba, bx, lam, h0, buf, layer, *, bsz, seq, d, tl):
    assert seq % tl == 0 and tl >= CONV_W - 1 and tl % LRU_SEGMENTS == 0
    nl = seq // tl
    n_blocks, bw = wax.shape[0], wax.shape[1]
    vec = pl.BlockSpec((1, d), lambda b, l: (0, 0))
    est = 2 * tl * d * (4 + 2) + 2 * tl * d * 2 + (3 * tl + CONV_PAD) * d * 4 + 8 * tl * d * 4
    kern = functools.partial(_lru_kernel, tl=tl, n_blocks=n_blocks, bw=bw)
    return pl.pallas_call(
        kern,
        grid=(bsz, nl),
        in_specs=[
            pl.BlockSpec((tl, d), lambda b, l: (b * nl + l, IN_SEC_RX)),
            pl.BlockSpec((tl, d), lambda b, l: (b * nl + l, IN_SEC_RY - IN_F32_SECTIONS)),
            pl.BlockSpec((CONV_W, d), lambda b, l: (0, 0)),
            vec,
            pl.BlockSpec((n_blocks, bw, 2 * bw), lambda b, l: (0, 0, 0)),
            vec, vec, vec,
            pl.BlockSpec((None, 1, 1, d), lambda b, l: (layer, b, 0, 0)),
            pl.BlockSpec((None, 1, CONV_W - 1, d), lambda b, l: (layer, b, 0, 0)),
        ],
        out_specs=[
            pl.BlockSpec((tl, d), lambda b, l: (b * nl + l, 0)),
            pl.BlockSpec((1, 1, d), lambda b, l: (b, 0, 0)),
        ],
        out_shape=[
            jax.ShapeDtypeStruct((bsz * seq, d), BF16),
            jax.ShapeDtypeStruct((bsz, 1, d), F32),
        ],
        scratch_shapes=[
            pltpu.VMEM((CONV_PAD + tl, d), F32),
            pltpu.VMEM((tl, d), F32),
            pltpu.VMEM((tl, d), F32),
            pltpu.VMEM((tl, d), F32),
            pltpu.VMEM((tl, d), F32),
            pltpu.VMEM((1, d), F32),
        ],
        compiler_params=_params(("parallel", "arbitrary"), est),
        name="conv_lru",
    )(pf, pb, cw, cb, wax, ba, bx, lam, h0, buf)


def _mem_attn_kernel(q_ref, k_ref, v_ref, o_ref, *, scale, heads, hd):
    cols = [slice(h * hd, (h + 1) * hd) for h in range(heads)]
    scores = [_dot_nt(q_ref[:, c], k_ref[0, :, c].astype(BF16)) * scale for c in cols]
    probs = [jnp.exp(s - jnp.max(s, axis=-1, keepdims=True)) for s in scores]
    outs = [_dot(p.astype(BF16), v_ref[0, :, c].astype(BF16)) for p, c in zip(probs, cols)]
    for p, o, c in zip(probs, outs, cols):
        o_ref[:, c] = (o / jnp.sum(p, axis=-1, keepdims=True)).astype(o_ref.dtype)


def mem_attn(pb, mem_k, mem_v, layer, *, bsz, seq, heads, hd, col0, k_col, v_col, tl):
    assert seq % tl == 0
    nl = seq // tl
    n_mem = mem_k.shape[2]
    d_c = heads * hd
    est = 4 * tl * d_c * 2 + 4 * n_mem * d_c * 4 + 6 * tl * n_mem * 4
    kern = functools.partial(_mem_attn_kernel, scale=1.0 / math.sqrt(hd), heads=heads, hd=hd)
    return pl.pallas_call(
        kern,
        grid=(bsz, nl),
        in_specs=[
            pl.BlockSpec((tl, d_c), lambda b, l: (b * nl + l, col0)),
            pl.BlockSpec((None, 1, n_mem, d_c), lambda b, l: (layer, b, 0, k_col)),
            pl.BlockSpec((None, 1, n_mem, d_c), lambda b, l: (layer, b, 0, v_col)),
        ],
        out_specs=pl.BlockSpec((tl, d_c), lambda b, l: (b * nl + l, 0)),
        out_shape=jax.ShapeDtypeStruct((bsz * seq, d_c), BF16),
        compiler_params=_params(("parallel", "parallel"), est),
        name="mem_attn",
    )(pb, mem_k, mem_v)


def _merge_kernel(x_ref, oa_ref, ob_ref, oc_ref, g0_ref, g1_ref, g2_ref,
                  wa_ref, wb_ref, wc_ref, wo_ref, gn_ref, y_ref):
    m = g0_ref[...].astype(F32) * _dot(oa_ref[...], wa_ref[...])
    m = m + g1_ref[...].astype(F32) * _dot(ob_ref[...], wb_ref[...])
    m = m + g2_ref[...].astype(F32) * _dot(oc_ref[...], wc_ref[...])
    z = _dot(m.astype(BF16), wo_ref[...])
    y_ref[...] = x_ref[...] + _rms(z, gn_ref[...])


def merge(x, oa, ob, oc, pb, wa, wb, wc, wo, layer, gn, *, col_gates, tm):
    t, d = x.shape
    db = oa.shape[1]
    assert t % tm == 0
    row = lambda i: (i, 0)
    const = lambda i: (0, 0)
    wspec = lambda rows: pl.BlockSpec((None, rows, d), lambda i: (layer, 0, 0), pipeline_mode=pl.Buffered(1))
    gate_specs = [pl.BlockSpec((tm, d), lambda i, k=k: (i, col_gates + k)) for k in range(N_GATES)]
    est = (4 * tm * d * 4 + 6 * tm * db * 2 + 6 * tm * d * 2
           + 3 * db * d * 2 + d * d * 2 + 6 * tm * d * 4)
    return pl.pallas_call(
        _merge_kernel,
        grid=(t // tm,),
        in_specs=[
            pl.BlockSpec((tm, d), row),
            pl.BlockSpec((tm, db), row), pl.BlockSpec((tm, db), row), pl.BlockSpec((tm, db), row),
            *gate_specs,
            wspec(db), wspec(db), wspec(db), wspec(d),
            pl.BlockSpec((1, d), const),
        ],
        out_specs=pl.BlockSpec((tm, d), row),
        out_shape=jax.ShapeDtypeStruct((t, d), F32),
        compiler_params=_params(("parallel",), est),
        name="merge",
    )(x, oa, ob, oc, pb, pb, pb, wa, wb, wc, wo, gn)


def _ffn_kernel(x_ref, gpre_ref, wg_ref, wu_ref, wd_ref, gpost_ref, y_ref, h_ref, acc_ref):
    j = pl.program_id(1)

    @pl.when(j == 0)
    def _():
        h_ref[...] = _rms(x_ref[...], gpre_ref[...]).astype(BF16)
        acc_ref[...] = jnp.zeros_like(acc_ref)

    h = h_ref[...]
    gt = _dot(h, wg_ref[...])
    up = _dot(h, wu_ref[...])
    act = (jax.nn.silu(gt) * up).astype(BF16)
    acc_ref[...] += _dot(act, wd_ref[...])

    @pl.when(j == pl.num_programs(1) - 1)
    def _():
        y_ref[...] = x_ref[...] + _rms(acc_ref[...], gpost_ref[...])


def ffn(x, gpre, w_gu, w_down, layer, gpost, *, tm, tf):
    t, d = x.shape
    d_ff = w_down.shape[1]
    assert t % tm == 0 and d_ff % tf == 0
    nf = d_ff // tf
    est = 4 * tm * d * 4 + 2 * 3 * d * tf * 2 + tm * d * 2 + tm * d * 4 + 4 * tm * tf * 4
    return pl.pallas_call(
        _ffn_kernel,
        grid=(t // tm, nf),
        in_specs=[
            pl.BlockSpec((tm, d), lambda i, j: (i, 0)),
            pl.BlockSpec((1, d), lambda i, j: (0, 0)),
            pl.BlockSpec((None, d, tf), lambda i, j: (layer, 0, j)),
            pl.BlockSpec((None, d, tf), lambda i, j: (layer, 0, nf + j)),
            pl.BlockSpec((None, tf, d), lambda i, j: (layer, j, 0)),
            pl.BlockSpec((1, d), lambda i, j: (0, 0)),
        ],
        out_specs=pl.BlockSpec((tm, d), lambda i, j: (i, 0)),
        out_shape=jax.ShapeDtypeStruct((t, d), F32),
        scratch_shapes=[pltpu.VMEM((tm, d), BF16), pltpu.VMEM((tm, d), F32)],
        compiler_params=_params(("parallel", "arbitrary"), est),
        name="ffn",
    )(x, gpre, w_gu, w_gu, w_down, gpost)


def _row_tile(n, target):
    t = min(n, target)
    while n % t:
        t //= 2
    return t


def _trunk_layer(x2, bsz, seq, mem, state, lb, w, p, layer):
    t, d = x2.shape
    s_hg, h_lru, conv_buf, state_layer = state
    mem_k, mem_v, mem_layer, k_col, v_col = mem
    heads, dk = s_hg.shape[2], s_hg.shape[3]
    d_a = heads * dk
    d_b = h_lru.shape[-1]
    mem_heads, hd = p["mem_heads"], p["mem_hd"]
    if seq < CONV_W - 1:
        raise NotImplementedError("sequence shorter than the conv history")

    pf, pb = in_proj(x2, p["norm_pre_mix"], w["w_in"], layer, lb, p["b_gate"], sec=d_a,
                     tm=_row_tile(t, 1024), tn=1024)
    n_bf_sections = pb.shape[1] // d_a

    o_a, s_new = hgrn2(pf, pb, p["hgrn_out_norm"], s_hg, state_layer, bsz=bsz, seq=seq, heads=heads, dk=dk,
                       tl=_row_tile(seq, 1024))
    o_b, h_last = conv_lru(pf, pb, p["conv_w"], p["conv_b"], p["lru_wax"], p["lru_ba"], p["lru_bx"],
                           p["lru_lambda"], h_lru, conv_buf, state_layer,
                           bsz=bsz, seq=seq, d=d_b, tl=_row_tile(seq, 256))
    o_c = mem_attn(pb, mem_k, mem_v, mem_layer, bsz=bsz, seq=seq, heads=mem_heads, hd=hd,
                   col0=n_bf_sections - 1, k_col=k_col, v_col=v_col, tl=_row_tile(seq, 1024))

    gate_col = (IN_SEC_GATES - IN_F32_SECTIONS) * d_a
    assert gate_col % d == 0
    x2 = merge(x2, o_a, o_b, o_c, pb, w["w_branch_a"], w["w_branch_b"], w["w_branch_c"], w["w_out"], layer,
               p["norm_post_mix"], col_gates=gate_col // d, tm=_row_tile(t, 256))
    x2 = ffn(x2, p["norm_pre_ffn"], w["ffn_w_gu"], w["ffn_w_down"], layer, p["norm_post_ffn"],
             tm=_row_tile(t, 512), tf=512)

    rx_tail = pf.reshape(bsz, seq, -1)[:, seq - (CONV_W - 1):, IN_SEC_RX * d_a:(IN_SEC_RX + 1) * d_a]
    return x2, s_new, h_last.reshape(bsz, d_b), rx_tail


def kernel(x_prompt, x_sample, state_hgrn, state_lru, state_conv, cache_mem_k, cache_mem_v, mem_prompt, norm_mem, mem_w_kv, hgrn_lower_bound, norm_pre_mix, w_in, b_gate, hgrn_out_norm, conv_w, conv_b, lru_wa, lru_ba, lru_wx, lru_bx, lru_lambda, w_branch_a, w_branch_b, w_branch_c, w_out, norm_post_mix, norm_pre_ffn, ffn_w_gu, ffn_w_down, norm_post_ffn):
    depth = w_in.shape[0]
    bp, sp, d = x_prompt.shape
    bs, ss, _ = x_sample.shape
    _, _, heads, dk, dv = state_hgrn.shape
    d_a = heads * dk
    d_b = state_lru.shape[-1]
    n_mem, mem_heads, hd = cache_mem_k.shape[2:]
    d_c = mem_heads * hd
    assert dk == dv and d_b == d_a and d_c == d_a and d == 2 * d_a
    assert b_gate.shape[1] == N_GATES * d

    sm = jax.nn.softmax(hgrn_lower_bound.astype(F32), axis=0)
    lbs = jnp.cumsum(sm, axis=0) - sm[0:1]

    xp = x_prompt.reshape(bp * sp, d)
    xs = x_sample.reshape(bs * ss, d)
    mem2 = mem_prompt.reshape(bp * n_mem, d)
    zero_state = (jnp.zeros((1, bp, heads, dk, dv), F32), jnp.zeros((1, bp, 1, d_b), F32),
                  jnp.zeros((1, bp, CONV_W - 1, d_b), F32), 0)
    cache_k = cache_mem_k.reshape(depth, bs, n_mem, d_c)
    cache_v = cache_mem_v.reshape(depth, bs, n_mem, d_c)
    lru_s4 = state_lru.reshape(depth, bs, 1, d_b)

    w = dict(w_in=cast_bf16(w_in), w_branch_a=cast_bf16(w_branch_a), w_branch_b=cast_bf16(w_branch_b),
             w_branch_c=cast_bf16(w_branch_c), w_out=cast_bf16(w_out), ffn_w_gu=cast_bf16(ffn_w_gu),
             ffn_w_down=cast_bf16(ffn_w_down), mem_w_kv=cast_bf16(mem_w_kv))

    outs = {k: [] for k in ("hg_p", "lru_p", "conv_p", "mk_p", "mv_p", "hg_s", "lru_s", "conv_s")}
    for l in range(depth):
        row = lambda a: a[l].reshape(1, -1)
        p = dict(
            mem_heads=mem_heads, mem_hd=hd,
            norm_pre_mix=row(norm_pre_mix), b_gate=row(b_gate),
            hgrn_out_norm=row(hgrn_out_norm), conv_w=conv_w[l], conv_b=row(conv_b),
            lru_wax=jnp.concatenate([lru_wa[l], lru_wx[l]], axis=-1).astype(BF16),
            lru_ba=row(lru_ba), lru_bx=row(lru_bx), lru_lambda=row(lru_lambda),
            norm_post_mix=row(norm_post_mix), norm_pre_ffn=row(norm_pre_ffn),
            norm_post_ffn=row(norm_post_ffn),
        )
        lb = lbs[l].reshape(1, -1)

        kv = norm_matmul(mem2, row(norm_mem), w["mem_w_kv"], l, tm=_row_tile(bp * n_mem, 512), tn=1024)
        kv4 = kv.reshape(1, bp, n_mem, 2 * d_c)
        xp, s1, h1, c1 = _trunk_layer(xp, bp, sp, (kv4, kv4, 0, 0, 1), zero_state, lb, w, p, l)
        outs["hg_p"].append(s1); outs["lru_p"].append(h1); outs["conv_p"].append(c1)
        outs["mk_p"].append(kv4[0, :, :, :d_c].reshape(bp, n_mem, mem_heads, hd))
        outs["mv_p"].append(kv4[0, :, :, d_c:].reshape(bp, n_mem, mem_heads, hd))

        xs, s2, h2, c2 = _trunk_layer(xs, bs, ss, (cache_k, cache_v, l, 0, 0),
                                      (state_hgrn, lru_s4, state_conv, l), lb, w, p, l)
        outs["hg_s"].append(s2); outs["lru_s"].append(h2); outs["conv_s"].append(c2)

    st = {k: jnp.stack(v) for k, v in outs.items()}
    return (xp.reshape(bp, sp, d), xs.reshape(bs, ss, d), st["hg_p"], st["lru_p"], st["conv_p"],
            st["mk_p"], st["mv_p"], st["hg_s"], st["lru_s"], st["conv_s"])
```

```python
import functools
import math

import numpy as np
import jax
import jax.numpy as jnp
from jax import lax
from jax.experimental import pallas as pl
from jax.experimental.pallas import tpu as pltpu

F32 = jnp.float32
BF16 = jnp.bfloat16

EPS = 1e-6
LRU_C = 8.0
CHUNK = 64
SUB = 16
HALF = 8
N_PIECES = 3
CONV_W = 4
CONV_PAD = 8
LRU_SEGMENTS = 4
N_GATES = 3

V7X_VMEM_BYTES = 64 * 1024 * 1024
VMEM_LIMIT_CAP = 56 * 1024 * 1024


def _vmem_limit(estimate_bytes):
    return int(min(VMEM_LIMIT_CAP, max(16 * 1024 * 1024, estimate_bytes * 5 // 4)))


def _params(sem, vmem_estimate):
    return pltpu.CompilerParams(dimension_semantics=sem, vmem_limit_bytes=_vmem_limit(vmem_estimate))


def _rms(x, g):
    ms = jnp.mean(x * x, axis=-1, keepdims=True)
    return x * lax.rsqrt(ms + EPS) * g


def _dot(a, b):
    return jnp.dot(a, b, preferred_element_type=F32)


def _dot_nt(a, b):
    return lax.dot_general(a, b, (((1,), (1,)), ((), ())), preferred_element_type=F32)


def _dot_tn(a, b):
    return lax.dot_general(a, b, (((0,), (0,)), ((), ())), preferred_element_type=F32)


def _cast_kernel(w_ref, o_ref):
    o_ref[...] = w_ref[...].astype(o_ref.dtype)


CAST_BLOCK_BYTES = 4 * 1024 * 1024


def cast_bf16(w):
    depth, r, c = w.shape
    tr = r
    while tr * c * 4 > CAST_BLOCK_BYTES and tr % 32 == 0:
        tr //= 2
    spec = pl.BlockSpec((None, tr, c), lambda l, i: (l, i, 0))
    return pl.pallas_call(
        _cast_kernel,
        grid=(depth, r // tr),
        in_specs=[spec],
        out_specs=spec,
        out_shape=jax.ShapeDtypeStruct(w.shape, BF16),
        compiler_params=_params(("parallel", "parallel"), 2 * tr * c * 6),
        name="cast_bf16",
    )(w)


def _norm_matmul_kernel(x_ref, g_ref, w_ref, o_ref, h_ref):
    @pl.when(pl.program_id(1) == 0)
    def _():
        h_ref[...] = _rms(x_ref[...], g_ref[...]).astype(BF16)

    o_ref[...] = _dot(h_ref[...], w_ref[...]).astype(o_ref.dtype)


def norm_matmul(x, g, w, layer, *, tm, tn, out_dtype=F32):
    t, d = x.shape
    n = w.shape[2]
    assert t % tm == 0 and n % tn == 0
    est = 2 * tm * d * 4 + 2 * d * tn * 2 + 2 * tm * tn * 4 + tm * d * 2
    return pl.pallas_call(
        _norm_matmul_kernel,
        grid=(t // tm, n // tn),
        in_specs=[
            pl.BlockSpec((tm, d), lambda i, j: (i, 0)),
            pl.BlockSpec((1, d), lambda i, j: (0, 0)),
            pl.BlockSpec((None, d, tn), lambda i, j: (layer, 0, j)),
        ],
        out_specs=pl.BlockSpec((tm, tn), lambda i, j: (i, j)),
        out_shape=jax.ShapeDtypeStruct((t, n), out_dtype),
        scratch_shapes=[pltpu.VMEM((tm, d), BF16)],
        compiler_params=_params(("parallel", "arbitrary"), est),
        name="norm_matmul",
    )(x, g, w)


IN_F32_SECTIONS = 2
IN_SEC_LOGF, IN_SEC_RX, IN_SEC_Q, IN_SEC_V, IN_SEC_OG, IN_SEC_RY, IN_SEC_GATES = range(7)


def _in_proj_kernel(perm_ref, x_ref, g_ref, w_ref, lb_ref, bg_ref, of_ref, ob_ref, h_ref,
                    *, tps, n_gate_secs, sub_rows):
    del perm_ref
    j = pl.program_id(1)
    sec = j // tps

    @pl.when(j == 0)
    def _():
        h_ref[...] = _rms(x_ref[...], g_ref[...]).astype(BF16)

    tm = h_ref.shape[0]

    def run(out_ref, act):
        for r in range(tm // sub_rows):
            rs = slice(r * sub_rows, (r + 1) * sub_rows)
            out_ref[rs, :] = act(_dot(h_ref[rs, :], w_ref[...])).astype(out_ref.dtype)

    def log_forget(a):
        lb = lb_ref[...]
        return jnp.log(lb + (1.0 - lb) * jax.nn.sigmoid(a))

    sec_cq = IN_SEC_GATES + n_gate_secs
    pl.when(sec == IN_SEC_LOGF)(lambda: run(of_ref, log_forget))
    pl.when(sec == IN_SEC_RX)(lambda: run(of_ref, lambda a: a))
    pl.when((sec == IN_SEC_Q) | (sec == IN_SEC_OG))(lambda: run(ob_ref, jax.nn.silu))
    pl.when((sec == IN_SEC_V) | (sec == sec_cq))(lambda: run(ob_ref, lambda a: a))
    pl.when(sec == IN_SEC_RY)(lambda: run(ob_ref, jax.nn.gelu))
    pl.when((sec >= IN_SEC_GATES) & (sec < sec_cq))(
        lambda: run(ob_ref, lambda a: jax.nn.sigmoid(a + bg_ref[...])))


def in_proj(x, g, w, layer, lb, bg, *, sec, tm, tn):
    t, d = x.shape
    n = w.shape[2]
    assert t % tm == 0 and sec % tn == 0 and n % sec == 0
    tps = sec // tn
    n_sec = n // sec
    n_gate_secs = bg.shape[1] // sec
    assert n_sec == IN_SEC_GATES + n_gate_secs + 1
    perm = jnp.asarray([1, 4, 0, 2, 3, 5] + list(range(7, 7 + n_gate_secs)) + [6], jnp.int32)
    nf = IN_F32_SECTIONS * tps
    sub_rows = min(tm, 256)
    est = (2 * tm * d * 4 + 2 * d * tn * 2 + 2 * tm * tn * 4 + 2 * tm * tn * 2 + tm * d * 2
           + 6 * sub_rows * tn * 4)
    kern = functools.partial(_in_proj_kernel, tps=tps, n_gate_secs=n_gate_secs, sub_rows=sub_rows)
    grid_spec = pltpu.PrefetchScalarGridSpec(
        num_scalar_prefetch=1,
        grid=(t // tm, n // tn),
        in_specs=[
            pl.BlockSpec((tm, d), lambda i, j, perm: (i, 0)),
            pl.BlockSpec((1, d), lambda i, j, perm: (0, 0)),
            pl.BlockSpec((None, d, tn), lambda i, j, perm: (layer, 0, perm[j // tps] * tps + j % tps)),
            pl.BlockSpec((1, tn), lambda i, j, perm: (0, jnp.minimum(j, tps - 1))),
            pl.BlockSpec((1, tn), lambda i, j, perm: (0, jnp.clip(j - IN_SEC_GATES * tps, 0,
                                                                   n_gate_secs * tps - 1))),
        ],
        out_specs=[
            pl.BlockSpec((tm, tn), lambda i, j, perm: (i, jnp.minimum(j, nf - 1))),
            pl.BlockSpec((tm, tn), lambda i, j, perm: (i, jnp.maximum(j - nf, 0))),
        ],
        scratch_shapes=[pltpu.VMEM((tm, d), BF16)],
    )
    return pl.pallas_call(
        kern,
        grid_spec=grid_spec,
        out_shape=[
            jax.ShapeDtypeStruct((t, IN_F32_SECTIONS * sec), F32),
            jax.ShapeDtypeStruct((t, n - IN_F32_SECTIONS * sec), BF16),
        ],
        compiler_params=_params(("parallel", "arbitrary"), est),
        name="in_proj",
    )(perm, x, g, w, lb, bg)


def _hgrn_consts(chunk, dk):
    t = np.arange(chunk)[:, None]
    s = np.arange(chunk)[None, :]
    cum = np.concatenate([(s <= t).astype(np.float32)] * N_PIECES, axis=1)
    lane_blk = np.arange(HALF * dk)[:, None] // dk
    sel = (lane_blk == (np.arange(chunk)[None, :] % HALF)).astype(np.float32)
    return jnp.asarray(cum, BF16), jnp.asarray(sel, BF16)


def _hgrn_kernel(q_ref, lf_ref, v_ref, og_ref, gn_ref, s0_ref, cum_ref, sel_ref,
                 o_ref, sfin_ref, st_ref, b_ref, k_ref, u_ref, sb_ref, *, chunk, n_chunks, dk):
    l = pl.program_id(2)
    n_sub = chunk // SUB
    width = q_ref.shape[1]
    n_heads = width // dk
    heads = [slice(h * dk, (h + 1) * dk) for h in range(n_heads)]

    @pl.when(l == 0)
    def _():
        for h in range(n_heads):
            st_ref[h] = s0_ref[0, h].T

    q = q_ref[...].astype(F32)
    lf = lf_ref[...]
    kk = 1.0 - jnp.exp(lf)
    vb = v_ref[...]
    k_ref[...] = kk

    p0 = lf.astype(BF16)
    r1 = lf - p0.astype(F32)
    p1 = r1.astype(BF16)
    p2 = (r1 - p1.astype(F32)).astype(BF16)
    cum = cum_ref[...]
    b = jnp.concatenate(
        [_dot(cum, jnp.concatenate([p[c * chunk:(c + 1) * chunk, :] for p in (p0, p1, p2)], axis=0))
         for c in range(n_chunks)], axis=0)
    b_ref[...] = b
    tl = n_chunks * chunk

    def rows_of(ref, group, offset):
        return jnp.concatenate(
            [jnp.broadcast_to(ref[pl.ds(g * group + offset, 1), :], (group, width)) for g in range(tl // group)],
            axis=0)

    b_end = rows_of(b_ref, SUB, SUB - 1)
    b_mid = rows_of(b_ref, SUB, HALF - 1)
    b_last = rows_of(b_ref, chunk, chunk - 1)

    qe = (q * jnp.exp(b)).astype(BF16)
    k_dec = (kk * jnp.exp(b_end - b)).astype(BF16)
    k_end = (kk * jnp.exp(b_last - b)).astype(BF16)
    decay = [jnp.exp(b_ref[pl.ds((c + 1) * chunk - 1, 1), :]) for c in range(n_chunks)]
    q_dec = [(q * jnp.exp(jnp.minimum(b - rows_of(b_ref, chunk, (j + 1) * SUB - 1), 0.0))).astype(BF16)
             for j in range(n_sub - 1)]
    q_mid = (q * jnp.exp(jnp.minimum(b - b_mid, 0.0))).astype(BF16)
    k_mid = (kk * jnp.exp(jnp.minimum(b_mid - b, 0.0))).astype(BF16)

    w = [(q * jnp.exp(jnp.minimum(b - rows_of(b_ref, HALF, u), 0.0)) * rows_of(k_ref, HALF, u)).astype(BF16)
         for u in range(HALF)]
    sel = sel_ref[...]
    diag = [_dot(jnp.concatenate([wu[:, hc] for wu in w], axis=1), sel) for hc in heads]

    row = lax.broadcasted_iota(jnp.int32, (chunk, chunk), 0)
    col = lax.broadcasted_iota(jnp.int32, (chunk, chunk), 1)
    row_blk = row // SUB
    col_blk = col // SUB
    mid_mask = (col_blk == row_blk) & (row % SUB >= HALF) & (col % SUB < HALF)
    diag_mask = (col // HALF == row // HALF) & (col <= row)
    eye = (lax.broadcasted_iota(jnp.int32, (dk, dk), 0)
           == lax.broadcasted_iota(jnp.int32, (dk, dk), 1)).astype(F32).astype(BF16)

    chunks = [slice(c * chunk, (c + 1) * chunk) for c in range(n_chunks)]
    units = [(h, hc, c, rows) for h, hc in enumerate(heads) for c, rows in enumerate(chunks)]
    cross, v_t = [], []
    for h, hc, c, rows in units:
        qd = jnp.concatenate([qj[rows, hc] for qj in q_dec] + [q_mid[rows, hc]], axis=0)
        kd = jnp.concatenate([k_dec[rows, hc], k_mid[rows, hc]], axis=0)
        cross.append(_dot_nt(qd, kd))
        v_t.append(_dot_nt(eye, vb[rows, hc]).astype(BF16))

    intra = []
    for i, (h, hc, c, rows) in enumerate(units):
        m = cross[i]
        attn = jnp.where(mid_mask, m[(n_sub - 1) * chunk:, chunk:], 0.0)
        for j in range(n_sub - 1):
            attn = jnp.where((col_blk == j) & (row_blk > j), m[j * chunk:(j + 1) * chunk, :chunk], attn)
        attn = jnp.where(diag_mask, diag[h][rows, :], attn)
        intra.append(_dot(attn.astype(BF16), vb[rows, hc]))
        u_ref[i] = _dot(v_t[i], k_end[rows, hc])

    st = [st_ref[h] for h in range(n_heads)]
    for i, (h, hc, c, rows) in enumerate(units):
        sb_ref[i] = st[h].astype(BF16)
        st[h] = st[h] * decay[c][:, hc] + u_ref[i]
    for h in range(n_heads):
        st_ref[h] = st[h]

    inter = [_dot_nt(qe[rows, hc], sb_ref[i]) for i, (h, hc, c, rows) in enumerate(units)]
    gn = gn_ref[...]
    for h, hc in enumerate(heads):
        per_chunk = range(h * n_chunks, (h + 1) * n_chunks)
        o = jnp.concatenate([intra[i] + inter[i] for i in per_chunk], axis=0)
        ms = jnp.mean(o * o, axis=-1, keepdims=True)
        o = o * lax.rsqrt(ms + EPS) * gn[:, hc] * og_ref[:, hc].astype(F32)
        o_ref[:, hc] = o.astype(o_ref.dtype)

    @pl.when(l == pl.num_programs(2) - 1)
    def _():
        for h in range(n_heads):
            sfin_ref[0, h] = st[h].T


HGRN_UNITS_PER_STEP = 32


def hgrn2(pf, pb, gn, s0, layer, *, bsz, seq, heads, dk, tl):
    chunk = min(CHUNK, seq)
    assert seq % tl == 0 and tl % chunk == 0 and chunk % SUB == 0
    nl = seq // tl
    d_a = heads * dk
    n_chunks = tl // chunk
    hb = max(1, min(heads, HGRN_UNITS_PER_STEP // n_chunks))
    while heads % hb:
        hb -= 1
    n_groups = heads // hb
    width = hb * dk
    n_units = hb * n_chunks

    def sec(k):
        return pl.BlockSpec((tl, width), lambda b, h, l, k=k: (b * nl + l, k * n_groups + h))

    cum, sel = _hgrn_consts(chunk, dk)
    est = (2 * tl * width * (4 + 3 * 2) + 2 * tl * width * 2 + 6 * hb * dk * dk * 4 + 2 * tl * width * 4
           + n_units * dk * dk * 6 + 2 * (cum.size + sel.size) * 2 + 24 * tl * width * 4)
    kern = functools.partial(_hgrn_kernel, chunk=chunk, n_chunks=n_chunks, dk=dk)
    return pl.pallas_call(
        kern,
        grid=(bsz, n_groups, nl),
        in_specs=[
            sec(IN_SEC_Q - IN_F32_SECTIONS), sec(IN_SEC_LOGF),
            sec(IN_SEC_V - IN_F32_SECTIONS), sec(IN_SEC_OG - IN_F32_SECTIONS),
            pl.BlockSpec((1, width), lambda b, h, l: (0, h)),
            pl.BlockSpec((None, 1, hb, dk, dk), lambda b, h, l: (layer, b, h, 0, 0)),
            pl.BlockSpec(cum.shape, lambda b, h, l: (0, 0)),
            pl.BlockSpec(sel.shape, lambda b, h, l: (0, 0)),
        ],
        out_specs=[
            pl.BlockSpec((tl, width), lambda b, h, l: (b * nl + l, h)),
            pl.BlockSpec((1, hb, dk, dk), lambda b, h, l: (b, h, 0, 0)),
        ],
        out_shape=[
            jax.ShapeDtypeStruct((bsz * seq, d_a), BF16),
            jax.ShapeDtypeStruct((bsz, heads, dk, dk), F32),
        ],
        scratch_shapes=[
            pltpu.VMEM((hb, dk, dk), F32),
            pltpu.VMEM((tl, width), F32),
            pltpu.VMEM((tl, width), F32),
            pltpu.VMEM((n_units, dk, dk), F32),
            pltpu.VMEM((n_units, dk, dk), BF16),
        ],
        compiler_params=_params(("parallel", "parallel", "arbitrary"), est),
        name="hgrn2",
    )(pb, pf, pb, pb, gn, s0, cum, sel)


def _lru_kernel(rx_ref, gy_ref, cw_ref, cb_ref, wax_ref, ba_ref, bx_ref, lam_ref, h0_ref, buf_ref,
                o_ref, hlast_ref, xp_ref, a_ref, u_ref, hs_ref, ps_ref, h_ref, *, tl, n_blocks, bw):
    l = pl.program_id(1)
    keep = CONV_W - 1

    @pl.when(l == 0)
    def _():
        xp_ref[CONV_PAD - keep:CONV_PAD, :] = buf_ref[0]
        h_ref[...] = h0_ref[0]

    x = rx_ref[...]
    xp_ref[CONV_PAD:CONV_PAD + tl, :] = x
    cw = cw_ref[...]
    xc = xp_ref[CONV_PAD - keep:CONV_PAD - keep + tl, :] * cw[0:1, :]
    for j in range(1, CONV_W):
        xc = xc + xp_ref[CONV_PAD - keep + j:CONV_PAD - keep + j + tl, :] * cw[j:j + 1, :]
    xc = xc + cb_ref[...]
    xp_ref[CONV_PAD - keep:CONV_PAD, :] = xp_ref[CONV_PAD + tl - keep:CONV_PAD + tl, :]

    xcb = xc.astype(BF16)
    pre = [_dot(xcb[:, n * bw:(n + 1) * bw], wax_ref[n]) for n in range(n_blocks)]
    r = jax.nn.sigmoid(jnp.concatenate([pn[:, :bw] for pn in pre], axis=-1) + ba_ref[...])
    ig = jax.nn.sigmoid(jnp.concatenate([pn[:, bw:] for pn in pre], axis=-1) + bx_ref[...])
    lam = lam_ref[...]
    softplus_neg = jnp.maximum(-lam, 0.0) + jnp.log1p(jnp.exp(-jnp.abs(lam)))
    log_a = -LRU_C * r * softplus_neg
    a = jnp.exp(log_a)
    mult = jnp.sqrt(jnp.maximum(-jnp.tanh(log_a) * (a * a + 1.0), 0.0))
    a_ref[...] = a
    u_ref[...] = mult * (ig * xc)

    seg = tl // LRU_SEGMENTS
    one = jnp.ones_like(h_ref[...])

    def step(t, carry):
        hs, ps = carry
        new_h, new_p = [], []
        for s in range(LRU_SEGMENTS):
            r = s * seg + t
            a_t = a_ref[pl.ds(r, 1), :]
            h_s = a_t * hs[s] + u_ref[pl.ds(r, 1), :]
            hs_ref[pl.ds(r, 1), :] = h_s
            new_h.append(h_s)
            if s > 0:
                p_s = a_t * ps[s - 1]
                ps_ref[pl.ds(r, 1), :] = p_s
                new_p.append(p_s)
        return tuple(new_h), tuple(new_p)

    init = ((h_ref[...],) + (jnp.zeros_like(one),) * (LRU_SEGMENTS - 1), (one,) * (LRU_SEGMENTS - 1))
    hs, ps = lax.fori_loop(0, seg, step, init, unroll=min(seg, 4))
    h = hs[0]
    gy = gy_ref[...].astype(F32)
    o_ref[0:seg, :] = (hs_ref[0:seg, :] * gy[0:seg, :]).astype(o_ref.dtype)
    for s in range(1, LRU_SEGMENTS):
        rows = slice(s * seg, (s + 1) * seg)
        o_ref[rows, :] = ((hs_ref[rows, :] + ps_ref[rows, :] * h) * gy[rows, :]).astype(o_ref.dtype)
        h = hs[s] + ps[s - 1] * h
    h_ref[...] = h

    @pl.when(l == pl.num_programs(1) - 1)
    def _():
        hlast_ref[0] = h


def conv_lru(pf, pb, cw, cb, wax, ba, bx, lam, h0, buf, layer, *, bsz, seq, d, tl):
    assert seq % tl == 0 and tl >= CONV_W - 1 and tl % LRU_SEGMENTS == 0
    nl = seq // tl
    n_blocks, bw = wax.shape[0], wax.shape[1]
    vec = pl.BlockSpec((1, d), lambda b, l: (0, 0))
    est = 2 * tl * d * (4 + 2) + 2 * tl * d * 2 + (3 * tl + CONV_PAD) * d * 4 + 8 * tl * d * 4
    kern = functools.partial(_lru_kernel, tl=tl, n_blocks=n_blocks, bw=bw)
    return pl.pallas_call(
        kern,
        grid=(bsz, nl),
        in_specs=[
            pl.BlockSpec((tl, d), lambda b, l: (b * nl + l, IN_SEC_RX)),
            pl.BlockSpec((tl, d), lambda b, l: (b * nl + l, IN_SEC_RY - IN_F32_SECTIONS)),
            pl.BlockSpec((CONV_W, d), lambda b, l: (0, 0)),
            vec,
            pl.BlockSpec((n_blocks, bw, 2 * bw), lambda b, l: (0, 0, 0)),
            vec, vec, vec,
            pl.BlockSpec((None, 1, 1, d), lambda b, l: (layer, b, 0, 0)),
            pl.BlockSpec((None, 1, CONV_W - 1, d), lambda b, l: (layer, b, 0, 0)),
        ],
        out_specs=[
            pl.BlockSpec((tl, d), lambda b, l: (b * nl + l, 0)),
            pl.BlockSpec((1, 1, d), lambda b, l: (b, 0, 0)),
        ],
        out_shape=[
            jax.ShapeDtypeStruct((bsz * seq, d), BF16),
            jax.ShapeDtypeStruct((bsz, 1, d), F32),
        ],
        scratch_shapes=[
            pltpu.VMEM((CONV_PAD + tl, d), F32),
            pltpu.VMEM((tl, d), F32),
            pltpu.VMEM((tl, d), F32),
            pltpu.VMEM((tl, d), F32),
            pltpu.VMEM((tl, d), F32),
            pltpu.VMEM((1, d), F32),
        ],
        compiler_params=_params(("parallel", "arbitrary"), est),
        name="conv_lru",
    )(pf, pb, cw, cb, wax, ba, bx, lam, h0, buf)


def _mem_attn_kernel(q_ref, k_ref, v_ref, o_ref, *, scale, heads, hd):
    cols = [slice(h * hd, (h + 1) * hd) for h in range(heads)]
    scores = [_dot_nt(q_ref[:, c], k_ref[0, :, c].astype(BF16)) * scale for c in cols]
    probs = [jnp.exp(s - jnp.max(s, axis=-1, keepdims=True)) for s in scores]
    outs = [_dot(p.astype(BF16), v_ref[0, :, c].astype(BF16)) for p, c in zip(probs, cols)]
    for p, o, c in zip(probs, outs, cols):
        o_ref[:, c] = (o / jnp.sum(p, axis=-1, keepdims=True)).astype(o_ref.dtype)


def mem_attn(pb, mem_k, mem_v, layer, *, bsz, seq, heads, hd, col0, k_col, v_col, tl):
    assert seq % tl == 0
    nl = seq // tl
    n_mem = mem_k.shape[2]
    d_c = heads * hd
    est = 4 * tl * d_c * 2 + 4 * n_mem * d_c * 4 + 6 * tl * n_mem * 4
    kern = functools.partial(_mem_attn_kernel, scale=1.0 / math.sqrt(hd), heads=heads, hd=hd)
    return pl.pallas_call(
        kern,
        grid=(bsz, nl),
        in_specs=[
            pl.BlockSpec((tl, d_c), lambda b, l: (b * nl + l, col0)),
            pl.BlockSpec((None, 1, n_mem, d_c), lambda b, l: (layer, b, 0, k_col)),
            pl.BlockSpec((None, 1, n_mem, d_c), lambda b, l: (layer, b, 0, v_col)),
        ],
        out_specs=pl.BlockSpec((tl, d_c), lambda b, l: (b * nl + l, 0)),
        out_shape=jax.ShapeDtypeStruct((bsz * seq, d_c), BF16),
        compiler_params=_params(("parallel", "parallel"), est),
        name="mem_attn",
    )(pb, mem_k, mem_v)


def _merge_kernel(x_ref, oa_ref, ob_ref, oc_ref, g0_ref, g1_ref, g2_ref,
                  wa_ref, wb_ref, wc_ref, wo_ref, gn_ref, y_ref):
    m = g0_ref[...].astype(F32) * _dot(oa_ref[...], wa_ref[...])
    m = m + g1_ref[...].astype(F32) * _dot(ob_ref[...], wb_ref[...])
    m = m + g2_ref[...].astype(F32) * _dot(oc_ref[...], wc_ref[...])
    z = _dot(m.astype(BF16), wo_ref[...])
    y_ref[...] = x_ref[...] + _rms(z, gn_ref[...])


def merge(x, oa, ob, oc, pb, wa, wb, wc, wo, layer, gn, *, col_gates, tm):
    t, d = x.shape
    db = oa.shape[1]
    assert t % tm == 0
    row = lambda i: (i, 0)
    const = lambda i: (0, 0)
    wspec = lambda rows: pl.BlockSpec((None, rows, d), lambda i: (layer, 0, 0), pipeline_mode=pl.Buffered(1))
    gate_specs = [pl.BlockSpec((tm, d), lambda i, k=k: (i, col_gates + k)) for k in range(N_GATES)]
    est = (4 * tm * d * 4 + 6 * tm * db * 2 + 6 * tm * d * 2
           + 3 * db * d * 2 + d * d * 2 + 6 * tm * d * 4)
    return pl.pallas_call(
        _merge_kernel,
        grid=(t // tm,),
        in_specs=[
            pl.BlockSpec((tm, d), row),
            pl.BlockSpec((tm, db), row), pl.BlockSpec((tm, db), row), pl.BlockSpec((tm, db), row),
            *gate_specs,
            wspec(db), wspec(db), wspec(db), wspec(d),
            pl.BlockSpec((1, d), const),
        ],
        out_specs=pl.BlockSpec((tm, d), row),
        out_shape=jax.ShapeDtypeStruct((t, d), F32),
        compiler_params=_params(("parallel",), est),
        name="merge",
    )(x, oa, ob, oc, pb, pb, pb, wa, wb, wc, wo, gn)


def _ffn_kernel(x_ref, gpre_ref, wg_ref, wu_ref, wd_ref, gpost_ref, y_ref, h_ref, acc_ref):
    j = pl.program_id(1)

    @pl.when(j == 0)
    def _():
        h_ref[...] = _rms(x_ref[...], gpre_ref[...]).astype(BF16)
        acc_ref[...] = jnp.zeros_like(acc_ref)

    h = h_ref[...]
    gt = _dot(h, wg_ref[...])
    up = _dot(h, wu_ref[...])
    act = (jax.nn.silu(gt) * up).astype(BF16)
    acc_ref[...] += _dot(act, wd_ref[...])

    @pl.when(j == pl.num_programs(1) - 1)
    def _():
        y_ref[...] = x_ref[...] + _rms(acc_ref[...], gpost_ref[...])


def ffn(x, gpre, w_gu, w_down, layer, gpost, *, tm, tf):
    t, d = x.shape
    d_ff = w_down.shape[1]
    assert t % tm == 0 and d_ff % tf == 0
    nf = d_ff // tf
    est = 4 * tm * d * 4 + 2 * 3 * d * tf * 2 + tm * d * 2 + tm * d * 4 + 4 * tm * tf * 4
    return pl.pallas_call(
        _ffn_kernel,
        grid=(t // tm, nf),
        in_specs=[
            pl.BlockSpec((tm, d), lambda i, j: (i, 0)),
            pl.BlockSpec((1, d), lambda i, j: (0, 0)),
            pl.BlockSpec((None, d, tf), lambda i, j: (layer, 0, j)),
            pl.BlockSpec((None, d, tf), lambda i, j: (layer, 0, nf + j)),
            pl.BlockSpec((None, tf, d), lambda i, j: (layer, j, 0)),
            pl.BlockSpec((1, d), lambda i, j: (0, 0)),
        ],
        out_specs=pl.BlockSpec((tm, d), lambda i, j: (i, 0)),
        out_shape=jax.ShapeDtypeStruct((t, d), F32),
        scratch_shapes=[pltpu.VMEM((tm, d), BF16), pltpu.VMEM((tm, d), F32)],
        compiler_params=_params(("parallel", "arbitrary"), est),
        name="ffn",
    )(x, gpre, w_gu, w_gu, w_down, gpost)


def _row_tile(n, target):
    t = min(n, target)
    while n % t:
        t //= 2
    return t


def _trunk_layer(x2, bsz, seq, mem, state, lb, w, p, layer):
    t, d = x2.shape
    s_hg, h_lru, conv_buf, state_layer = state
    mem_k, mem_v, mem_layer, k_col, v_col = mem
    heads, dk = s_hg.shape[2], s_hg.shape[3]
    d_a = heads * dk
    d_b = h_lru.shape[-1]
    mem_heads, hd = p["mem_heads"], p["mem_hd"]
    if seq < CONV_W - 1:
        raise NotImplementedError("sequence shorter than the conv history")

    pf, pb = in_proj(x2, p["norm_pre_mix"], w["w_in"], layer, lb, p["b_gate"], sec=d_a,
                     tm=_row_tile(t, 1024), tn=1024)
    n_bf_sections = pb.shape[1] // d_a

    o_a, s_new = hgrn2(pf, pb, p["hgrn_out_norm"], s_hg, state_layer, bsz=bsz, seq=seq, heads=heads, dk=dk,
                       tl=_row_tile(seq, 2048))
    o_b, h_last = conv_lru(pf, pb, p["conv_w"], p["conv_b"], p["lru_wax"], p["lru_ba"], p["lru_bx"],
                           p["lru_lambda"], h_lru, conv_buf, state_layer,
                           bsz=bsz, seq=seq, d=d_b, tl=_row_tile(seq, 512))
    o_c = mem_attn(pb, mem_k, mem_v, mem_layer, bsz=bsz, seq=seq, heads=mem_heads, hd=hd,
                   col0=n_bf_sections - 1, k_col=k_col, v_col=v_col, tl=_row_tile(seq, 1024))

    gate_col = (IN_SEC_GATES - IN_F32_SECTIONS) * d_a
    assert gate_col % d == 0
    x2 = merge(x2, o_a, o_b, o_c, pb, w["w_branch_a"], w["w_branch_b"], w["w_branch_c"], w["w_out"], layer,
               p["norm_post_mix"], col_gates=gate_col // d, tm=_row_tile(t, 256))
    x2 = ffn(x2, p["norm_pre_ffn"], w["ffn_w_gu"], w["ffn_w_down"], layer, p["norm_post_ffn"],
             tm=_row_tile(t, 512), tf=512)

    rx_tail = pf.reshape(bsz, seq, -1)[:, seq - (CONV_W - 1):, IN_SEC_RX * d_a:(IN_SEC_RX + 1) * d_a]
    return x2, s_new, h_last.reshape(bsz, d_b), rx_tail


def kernel(x_prompt, x_sample, state_hgrn, state_lru, state_conv, cache_mem_k, cache_mem_v, mem_prompt, norm_mem, mem_w_kv, hgrn_lower_bound, norm_pre_mix, w_in, b_gate, hgrn_out_norm, conv_w, conv_b, lru_wa, lru_ba, lru_wx, lru_bx, lru_lambda, w_branch_a, w_branch_b, w_branch_c, w_out, norm_post_mix, norm_pre_ffn, ffn_w_gu, ffn_w_down, norm_post_ffn):
    depth = w_in.shape[0]
    bp, sp, d = x_prompt.shape
    bs, ss, _ = x_sample.shape
    _, _, heads, dk, dv = state_hgrn.shape
    d_a = heads * dk
    d_b = state_lru.shape[-1]
    n_mem, mem_heads, hd = cache_mem_k.shape[2:]
    d_c = mem_heads * hd
    assert dk == dv and d_b == d_a and d_c == d_a and d == 2 * d_a
    assert b_gate.shape[1] == N_GATES * d

    sm = jax.nn.softmax(hgrn_lower_bound.astype(F32), axis=0)
    lbs = jnp.cumsum(sm, axis=0) - sm[0:1]

    xp = x_prompt.reshape(bp * sp, d)
    xs = x_sample.reshape(bs * ss, d)
    mem2 = mem_prompt.reshape(bp * n_mem, d)
    zero_state = (jnp.zeros((1, bp, heads, dk, dv), F32), jnp.zeros((1, bp, 1, d_b), F32),
                  jnp.zeros((1, bp, CONV_W - 1, d_b), F32), 0)
    cache_k = cache_mem_k.reshape(depth, bs, n_mem, d_c)
    cache_v = cache_mem_v.reshape(depth, bs, n_mem, d_c)
    lru_s4 = state_lru.reshape(depth, bs, 1, d_b)

    w = dict(w_in=cast_bf16(w_in), w_branch_a=cast_bf16(w_branch_a), w_branch_b=cast_bf16(w_branch_b),
             w_branch_c=cast_bf16(w_branch_c), w_out=cast_bf16(w_out), ffn_w_gu=cast_bf16(ffn_w_gu),
             ffn_w_down=cast_bf16(ffn_w_down), mem_w_kv=cast_bf16(mem_w_kv))

    outs = {k: [] for k in ("hg_p", "lru_p", "conv_p", "mk_p", "mv_p", "hg_s", "lru_s", "conv_s")}
    for l in range(depth):
        row = lambda a: a[l].reshape(1, -1)
        p = dict(
            mem_heads=mem_heads, mem_hd=hd,
            norm_pre_mix=row(norm_pre_mix), b_gate=row(b_gate),
            hgrn_out_norm=row(hgrn_out_norm), conv_w=conv_w[l], conv_b=row(conv_b),
            lru_wax=jnp.concatenate([lru_wa[l], lru_wx[l]], axis=-1).astype(BF16),
            lru_ba=row(lru_ba), lru_bx=row(lru_bx), lru_lambda=row(lru_lambda),
            norm_post_mix=row(norm_post_mix), norm_pre_ffn=row(norm_pre_ffn),
            norm_post_ffn=row(norm_post_ffn),
        )
        lb = lbs[l].reshape(1, -1)

        kv = norm_matmul(mem2, row(norm_mem), w["mem_w_kv"], l, tm=_row_tile(bp * n_mem, 512), tn=1024)
        kv4 = kv.reshape(1, bp, n_mem, 2 * d_c)
        xp, s1, h1, c1 = _trunk_layer(xp, bp, sp, (kv4, kv4, 0, 0, 1), zero_state, lb, w, p, l)
        outs["hg_p"].append(s1); outs["lru_p"].append(h1); outs["conv_p"].append(c1)
        outs["mk_p"].append(kv4[0, :, :, :d_c].reshape(bp, n_mem, mem_heads, hd))
        outs["mv_p"].append(kv4[0, :, :, d_c:].reshape(bp, n_mem, mem_heads, hd))

        xs, s2, h2, c2 = _trunk_layer(xs, bs, ss, (cache_k, cache_v, l, 0, 0),
                                      (state_hgrn, lru_s4, state_conv, l), lb, w, p, l)
        outs["hg_s"].append(s2); outs["lru_s"].append(h2); outs["conv_s"].append(c2)

    st = {k: jnp.stack(v) for k, v in outs.items()}
    return (xp.reshape(bp, sp, d), xs.reshape(bs, ss, d), st["hg_p"], st["lru_p"], st["conv_p"],
            st["mk_p"], st["mv_p"], st["hg_s"], st["lru_s"], st["conv_s"])
```

```python
import functools
import math

import numpy as np
import jax
import jax.numpy as jnp
from jax import lax
from jax.experimental import pallas as pl
from jax.experimental.pallas import tpu as pltpu

F32 = jnp.float32
BF16 = jnp.bfloat16

EPS = 1e-6
LRU_C = 8.0
CHUNK = 64
SUB = 16
HALF = 8
N_PIECES = 3
CONV_W = 4
CONV_PAD = 8
LRU_SEGMENTS = 4
N_GATES = 3

LANES = 128
BF16_SUBLANES = 16

V7X_VMEM_BYTES = 64 * 1024 * 1024
VMEM_LIMIT_CAP = 56 * 1024 * 1024


def _vmem_limit(estimate_bytes):
    return int(min(VMEM_LIMIT_CAP, max(16 * 1024 * 1024, estimate_bytes * 5 // 4)))


def _params(sem, vmem_estimate):
    return pltpu.CompilerParams(dimension_semantics=sem, vmem_limit_bytes=_vmem_limit(vmem_estimate))


def _rms(x, g):
    ms = jnp.mean(x * x, axis=-1, keepdims=True)
    return x * lax.rsqrt(ms + EPS) * g


def _dot(a, b):
    return jnp.dot(a, b, preferred_element_type=F32)


def _dot_nt(a, b):
    return lax.dot_general(a, b, (((1,), (1,)), ((), ())), preferred_element_type=F32)


def _dot_tn(a, b):
    return lax.dot_general(a, b, (((0,), (0,)), ((), ())), preferred_element_type=F32)


def _cast_kernel(w_ref, o_ref):
    o_ref[...] = w_ref[...].astype(o_ref.dtype)


CAST_BLOCK_BYTES = 4 * 1024 * 1024


def cast_bf16(w, layer):
    _, r, c = w.shape
    tr = r
    while tr * c * 4 > CAST_BLOCK_BYTES and tr % 32 == 0:
        tr //= 2
    return pl.pallas_call(
        _cast_kernel,
        grid=(r // tr,),
        in_specs=[pl.BlockSpec((None, tr, c), lambda i: (layer, i, 0))],
        out_specs=pl.BlockSpec((None, tr, c), lambda i: (0, i, 0)),
        out_shape=jax.ShapeDtypeStruct((1, r, c), BF16),
        compiler_params=_params(("parallel",), 2 * tr * c * 6),
        name="cast_bf16",
    )(w)


def _side_cast_specs(side, n_i, n_j):
    in_specs, out_specs, out_shapes, vmem = [], [], [], 0
    for w, layer in side:
        _, r, c = w.shape
        assert r % n_i == 0 and (r // n_i) % BF16_SUBLANES == 0
        br = r // n_i
        split = n_j > 1 and c % n_j == 0 and (c // n_j) % LANES == 0
        bc = c // n_j if split else c
        in_specs.append(pl.BlockSpec(
            (None, br, bc), lambda i, *rest, layer=layer, split=split: (layer, i, rest[0] if split else 0)))
        out_specs.append(pl.BlockSpec(
            (None, br, bc), lambda i, *rest, split=split: (0, i, rest[0] if split else 0)))
        out_shapes.append(jax.ShapeDtypeStruct((1, r, c), BF16))
        vmem += 2 * br * bc * 6
    return in_specs, out_specs, out_shapes, vmem


def _run_side_casts(side_in, side_out):
    for w_ref, o_ref in zip(side_in, side_out):
        o_ref[...] = w_ref[...].astype(o_ref.dtype)


def _norm_matmul_kernel(x_ref, g_ref, w_ref, o_ref, h_ref):
    @pl.when(pl.program_id(1) == 0)
    def _():
        h_ref[...] = _rms(x_ref[...], g_ref[...]).astype(BF16)

    o_ref[...] = _dot(h_ref[...], w_ref[...]).astype(o_ref.dtype)


def norm_matmul(x, g, w, *, tm, tn, out_dtype=F32):
    t, d = x.shape
    n = w.shape[2]
    assert t % tm == 0 and n % tn == 0
    est = 2 * tm * d * 4 + 2 * d * tn * 2 + 2 * tm * tn * 4 + tm * d * 2
    return pl.pallas_call(
        _norm_matmul_kernel,
        grid=(t // tm, n // tn),
        in_specs=[
            pl.BlockSpec((tm, d), lambda i, j: (i, 0)),
            pl.BlockSpec((1, d), lambda i, j: (0, 0)),
            pl.BlockSpec((None, d, tn), lambda i, j: (0, 0, j)),
        ],
        out_specs=pl.BlockSpec((tm, tn), lambda i, j: (i, j)),
        out_shape=jax.ShapeDtypeStruct((t, n), out_dtype),
        scratch_shapes=[pltpu.VMEM((tm, d), BF16)],
        compiler_params=_params(("parallel", "arbitrary"), est),
        name="norm_matmul",
    )(x, g, w)


IN_F32_SECTIONS = 2
IN_SEC_LOGF, IN_SEC_RX, IN_SEC_Q, IN_SEC_V, IN_SEC_OG, IN_SEC_RY, IN_SEC_GATES = range(7)


def _in_proj_kernel(perm_ref, x_ref, g_ref, w_ref, lb_ref, bg_ref, *rest, tps, n_gate_secs, sub_rows, n_side):
    del perm_ref
    side_in, (of_ref, ob_ref), side_out, (h_ref,) = (
        rest[:n_side], rest[n_side:n_side + 2], rest[n_side + 2:2 * n_side + 2], rest[2 * n_side + 2:])
    _run_side_casts(side_in, side_out)
    j = pl.program_id(1)
    sec = j // tps

    @pl.when(j == 0)
    def _():
        h_ref[...] = _rms(x_ref[...], g_ref[...]).astype(BF16)

    tm = h_ref.shape[0]

    def run(out_ref, act):
        for r in range(tm // sub_rows):
            rs = slice(r * sub_rows, (r + 1) * sub_rows)
            out_ref[rs, :] = act(_dot(h_ref[rs, :], w_ref[...])).astype(out_ref.dtype)

    def log_forget(a):
        lb = lb_ref[...]
        return jnp.log(lb + (1.0 - lb) * jax.nn.sigmoid(a))

    sec_cq = IN_SEC_GATES + n_gate_secs
    pl.when(sec == IN_SEC_LOGF)(lambda: run(of_ref, log_forget))
    pl.when(sec == IN_SEC_RX)(lambda: run(of_ref, lambda a: a))
    pl.when((sec == IN_SEC_Q) | (sec == IN_SEC_OG))(lambda: run(ob_ref, jax.nn.silu))
    pl.when((sec == IN_SEC_V) | (sec == sec_cq))(lambda: run(ob_ref, lambda a: a))
    pl.when(sec == IN_SEC_RY)(lambda: run(ob_ref, jax.nn.gelu))
    pl.when((sec >= IN_SEC_GATES) & (sec < sec_cq))(
        lambda: run(ob_ref, lambda a: jax.nn.sigmoid(a + bg_ref[...])))


def in_proj(x, g, w, lb, bg, *, sec, tm, tn, side=()):
    t, d = x.shape
    n = w.shape[2]
    assert t % tm == 0 and sec % tn == 0 and n % sec == 0
    tps = sec // tn
    n_sec = n // sec
    n_gate_secs = bg.shape[1] // sec
    assert n_sec == IN_SEC_GATES + n_gate_secs + 1
    perm = jnp.asarray([1, 4, 0, 2, 3, 5] + list(range(7, 7 + n_gate_secs)) + [6], jnp.int32)
    nf = IN_F32_SECTIONS * tps
    sub_rows = min(tm, 256)
    grid = (t // tm, n // tn)
    side_in, side_out, side_shapes, side_vmem = _side_cast_specs(side, *grid)
    est = (2 * tm * d * 4 + 2 * d * tn * 2 + 2 * tm * tn * 4 + 2 * tm * tn * 2 + tm * d * 2
           + 6 * sub_rows * tn * 4 + side_vmem)
    kern = functools.partial(_in_proj_kernel, tps=tps, n_gate_secs=n_gate_secs, sub_rows=sub_rows,
                             n_side=len(side))
    grid_spec = pltpu.PrefetchScalarGridSpec(
        num_scalar_prefetch=1,
        grid=grid,
        in_specs=[
            pl.BlockSpec((tm, d), lambda i, j, perm: (i, 0)),
            pl.BlockSpec((1, d), lambda i, j, perm: (0, 0)),
            pl.BlockSpec((None, d, tn), lambda i, j, perm: (0, 0, perm[j // tps] * tps + j % tps)),
            pl.BlockSpec((1, tn), lambda i, j, perm: (0, jnp.minimum(j, tps - 1))),
            pl.BlockSpec((1, tn), lambda i, j, perm: (0, jnp.clip(j - IN_SEC_GATES * tps, 0,
                                                                   n_gate_secs * tps - 1))),
            *side_in,
        ],
        out_specs=[
            pl.BlockSpec((tm, tn), lambda i, j, perm: (i, jnp.minimum(j, nf - 1))),
            pl.BlockSpec((tm, tn), lambda i, j, perm: (i, jnp.maximum(j - nf, 0))),
            *side_out,
        ],
        scratch_shapes=[pltpu.VMEM((tm, d), BF16)],
    )
    return pl.pallas_call(
        kern,
        grid_spec=grid_spec,
        out_shape=[
            jax.ShapeDtypeStruct((t, IN_F32_SECTIONS * sec), F32),
            jax.ShapeDtypeStruct((t, n - IN_F32_SECTIONS * sec), BF16),
            *side_shapes,
        ],
        compiler_params=_params(("parallel", "arbitrary"), est),
        name="in_proj",
    )(perm, x, g, w, lb, bg, *[w_ for w_, _ in side])


def _hgrn_consts(chunk, dk):
    t = np.arange(chunk)[:, None]
    s = np.arange(chunk)[None, :]
    cum = np.concatenate([(s <= t).astype(np.float32)] * N_PIECES, axis=1)
    lane_blk = np.arange(HALF * dk)[:, None] // dk
    sel = (lane_blk == (np.arange(chunk)[None, :] % HALF)).astype(np.float32)
    return jnp.asarray(cum, BF16), jnp.asarray(sel, BF16)


def _hgrn_kernel(q_ref, lf_ref, v_ref, og_ref, gn_ref, s0_ref, cum_ref, sel_ref,
                 o_ref, sfin_ref, st_ref, b_ref, k_ref, u_ref, sb_ref, *, chunk, n_chunks, dk):
    l = pl.program_id(2)
    n_sub = chunk // SUB
    width = q_ref.shape[1]
    n_heads = width // dk
    heads = [slice(h * dk, (h + 1) * dk) for h in range(n_heads)]

    @pl.when(l == 0)
    def _():
        for h in range(n_heads):
            st_ref[h] = s0_ref[0, h].T

    q = q_ref[...].astype(F32)
    lf = lf_ref[...]
    kk = 1.0 - jnp.exp(lf)
    vb = v_ref[...]
    k_ref[...] = kk

    p0 = lf.astype(BF16)
    r1 = lf - p0.astype(F32)
    p1 = r1.astype(BF16)
    p2 = (r1 - p1.astype(F32)).astype(BF16)
    cum = cum_ref[...]
    b = jnp.concatenate(
        [_dot(cum, jnp.concatenate([p[c * chunk:(c + 1) * chunk, :] for p in (p0, p1, p2)], axis=0))
         for c in range(n_chunks)], axis=0)
    b_ref[...] = b
    tl = n_chunks * chunk

    def rows_of(ref, group, offset):
        return jnp.concatenate(
            [jnp.broadcast_to(ref[pl.ds(g * group + offset, 1), :], (group, width)) for g in range(tl // group)],
            axis=0)

    b_end = rows_of(b_ref, SUB, SUB - 1)
    b_mid = rows_of(b_ref, SUB, HALF - 1)
    b_last = rows_of(b_ref, chunk, chunk - 1)

    qe = (q * jnp.exp(b)).astype(BF16)
    k_dec = (kk * jnp.exp(b_end - b)).astype(BF16)
    k_end = (kk * jnp.exp(b_last - b)).astype(BF16)
    decay = [jnp.exp(b_ref[pl.ds((c + 1) * chunk - 1, 1), :]) for c in range(n_chunks)]
    q_dec = [(q * jnp.exp(jnp.minimum(b - rows_of(b_ref, chunk, (j + 1) * SUB - 1), 0.0))).astype(BF16)
             for j in range(n_sub - 1)]
    q_mid = (q * jnp.exp(jnp.minimum(b - b_mid, 0.0))).astype(BF16)
    k_mid = (kk * jnp.exp(jnp.minimum(b_mid - b, 0.0))).astype(BF16)

    w = [(q * jnp.exp(jnp.minimum(b - rows_of(b_ref, HALF, u), 0.0)) * rows_of(k_ref, HALF, u)).astype(BF16)
         for u in range(HALF)]
    sel = sel_ref[...]
    diag = [_dot(jnp.concatenate([wu[:, hc] for wu in w], axis=1), sel) for hc in heads]

    row = lax.broadcasted_iota(jnp.int32, (chunk, chunk), 0)
    col = lax.broadcasted_iota(jnp.int32, (chunk, chunk), 1)
    row_blk = row // SUB
    col_blk = col // SUB
    mid_mask = (col_blk == row_blk) & (row % SUB >= HALF) & (col % SUB < HALF)
    diag_mask = (col // HALF == row // HALF) & (col <= row)
    eye = (lax.broadcasted_iota(jnp.int32, (dk, dk), 0)
           == lax.broadcasted_iota(jnp.int32, (dk, dk), 1)).astype(F32).astype(BF16)

    chunks = [slice(c * chunk, (c + 1) * chunk) for c in range(n_chunks)]
    units = [(h, hc, c, rows) for h, hc in enumerate(heads) for c, rows in enumerate(chunks)]
    cross, v_t = [], []
    for h, hc, c, rows in units:
        qd = jnp.concatenate([qj[rows, hc] for qj in q_dec] + [q_mid[rows, hc]], axis=0)
        kd = jnp.concatenate([k_dec[rows, hc], k_mid[rows, hc]], axis=0)
        cross.append(_dot_nt(qd, kd))
        v_t.append(_dot_nt(eye, vb[rows, hc]).astype(BF16))

    intra = []
    for i, (h, hc, c, rows) in enumerate(units):
        m = cross[i]
        attn = jnp.where(mid_mask, m[(n_sub - 1) * chunk:, chunk:], 0.0)
        for j in range(n_sub - 1):
            attn = jnp.where((col_blk == j) & (row_blk > j), m[j * chunk:(j + 1) * chunk, :chunk], attn)
        attn = jnp.where(diag_mask, diag[h][rows, :], attn)
        intra.append(_dot(attn.astype(BF16), vb[rows, hc]))
        u_ref[i] = _dot(v_t[i], k_end[rows, hc])

    st = [st_ref[h] for h in range(n_heads)]
    for i, (h, hc, c, rows) in enumerate(units):
        sb_ref[i] = st[h].astype(BF16)
        st[h] = st[h] * decay[c][:, hc] + u_ref[i]
    for h in range(n_heads):
        st_ref[h] = st[h]

    inter = [_dot_nt(qe[rows, hc], sb_ref[i]) for i, (h, hc, c, rows) in enumerate(units)]
    gn = gn_ref[...]
    for h, hc in enumerate(heads):
        per_chunk = range(h * n_chunks, (h + 1) * n_chunks)
        o = jnp.concatenate([intra[i] + inter[i] for i in per_chunk], axis=0)
        ms = jnp.mean(o * o, axis=-1, keepdims=True)
        o = o * lax.rsqrt(ms + EPS) * gn[:, hc] * og_ref[:, hc].astype(F32)
        o_ref[:, hc] = o.astype(o_ref.dtype)

    @pl.when(l == pl.num_programs(2) - 1)
    def _():
        for h in range(n_heads):
            sfin_ref[0, h] = st[h].T


HGRN_UNITS_PER_STEP = 32


def hgrn2(pf, pb, gn, s0, layer, *, bsz, seq, heads, dk, tl):
    chunk = min(CHUNK, seq)
    assert seq % tl == 0 and tl % chunk == 0 and chunk % SUB == 0
    nl = seq // tl
    d_a = heads * dk
    n_chunks = tl // chunk
    hb = max(1, min(heads, HGRN_UNITS_PER_STEP // n_chunks))
    while heads % hb:
        hb -= 1
    n_groups = heads // hb
    width = hb * dk
    n_units = hb * n_chunks

    def sec(k):
        return pl.BlockSpec((tl, width), lambda b, h, l, k=k: (b * nl + l, k * n_groups + h))

    cum, sel = _hgrn_consts(chunk, dk)
    est = (2 * tl * width * (4 + 3 * 2) + 2 * tl * width * 2 + 6 * hb * dk * dk * 4 + 2 * tl * width * 4
           + n_units * dk * dk * 6 + 2 * (cum.size + sel.size) * 2 + 24 * tl * width * 4)
    kern = functools.partial(_hgrn_kernel, chunk=chunk, n_chunks=n_chunks, dk=dk)
    return pl.pallas_call(
        kern,
        grid=(bsz, n_groups, nl),
        in_specs=[
            sec(IN_SEC_Q - IN_F32_SECTIONS), sec(IN_SEC_LOGF),
            sec(IN_SEC_V - IN_F32_SECTIONS), sec(IN_SEC_OG - IN_F32_SECTIONS),
            pl.BlockSpec((1, width), lambda b, h, l: (0, h)),
            pl.BlockSpec((None, 1, hb, dk, dk), lambda b, h, l: (layer, b, h, 0, 0)),
            pl.BlockSpec(cum.shape, lambda b, h, l: (0, 0)),
            pl.BlockSpec(sel.shape, lambda b, h, l: (0, 0)),
        ],
        out_specs=[
            pl.BlockSpec((tl, width), lambda b, h, l: (b * nl + l, h)),
            pl.BlockSpec((1, hb, dk, dk), lambda b, h, l: (b, h, 0, 0)),
        ],
        out_shape=[
            jax.ShapeDtypeStruct((bsz * seq, d_a), BF16),
            jax.ShapeDtypeStruct((bsz, heads, dk, dk), F32),
        ],
        scratch_shapes=[
            pltpu.VMEM((hb, dk, dk), F32),
            pltpu.VMEM((tl, width), F32),
            pltpu.VMEM((tl, width), F32),
            pltpu.VMEM((n_units, dk, dk), F32),
            pltpu.VMEM((n_units, dk, dk), BF16),
        ],
        compiler_params=_params(("parallel", "parallel", "arbitrary"), est),
        name="hgrn2",
    )(pb, pf, pb, pb, gn, s0, cum, sel)


def _lru_kernel(rx_ref, gy_ref, cw_ref, cb_ref, wax_ref, ba_ref, bx_ref, lam_ref, h0_ref, buf_ref,
                o_ref, hlast_ref, xp_ref, a_ref, u_ref, hs_ref, ps_ref, h_ref, *, tl, n_blocks, bw):
    l = pl.program_id(1)
    keep = CONV_W - 1

    @pl.when(l == 0)
    def _():
        xp_ref[CONV_PAD - keep:CONV_PAD, :] = buf_ref[0]
        h_ref[...] = h0_ref[0]

    x = rx_ref[...]
    xp_ref[CONV_PAD:CONV_PAD + tl, :] = x
    cw = cw_ref[...]
    xc = xp_ref[CONV_PAD - keep:CONV_PAD - keep + tl, :] * cw[0:1, :]
    for j in range(1, CONV_W):
        xc = xc + xp_ref[CONV_PAD - keep + j:CONV_PAD - keep + j + tl, :] * cw[j:j + 1, :]
    xc = xc + cb_ref[...]
    xp_ref[CONV_PAD - keep:CONV_PAD, :] = xp_ref[CONV_PAD + tl - keep:CONV_PAD + tl, :]

    xcb = xc.astype(BF16)
    pre = [_dot(xcb[:, n * bw:(n + 1) * bw], wax_ref[n]) for n in range(n_blocks)]
    r = jax.nn.sigmoid(jnp.concatenate([pn[:, :bw] for pn in pre], axis=-1) + ba_ref[...])
    ig = jax.nn.sigmoid(jnp.concatenate([pn[:, bw:] for pn in pre], axis=-1) + bx_ref[...])
    lam = lam_ref[...]
    softplus_neg = jnp.maximum(-lam, 0.0) + jnp.log1p(jnp.exp(-jnp.abs(lam)))
    log_a = -LRU_C * r * softplus_neg
    a = jnp.exp(log_a)
    mult = jnp.sqrt(jnp.maximum(-jnp.tanh(log_a) * (a * a + 1.0), 0.0))
    a_ref[...] = a
    u_ref[...] = mult * (ig * xc)

    seg = tl // LRU_SEGMENTS
    one = jnp.ones_like(h_ref[...])

    def step(t, carry):
        hs, ps = carry
        new_h, new_p = [], []
        for s in range(LRU_SEGMENTS):
            r = s * seg + t
            a_t = a_ref[pl.ds(r, 1), :]
            h_s = a_t * hs[s] + u_ref[pl.ds(r, 1), :]
            hs_ref[pl.ds(r, 1), :] = h_s
            new_h.append(h_s)
            if s > 0:
                p_s = a_t * ps[s - 1]
                ps_ref[pl.ds(r, 1), :] = p_s
                new_p.append(p_s)
        return tuple(new_h), tuple(new_p)

    init = ((h_ref[...],) + (jnp.zeros_like(one),) * (LRU_SEGMENTS - 1), (one,) * (LRU_SEGMENTS - 1))
    hs, ps = lax.fori_loop(0, seg, step, init, unroll=min(seg, 4))
    h = hs[0]
    gy = gy_ref[...].astype(F32)
    o_ref[0:seg, :] = (hs_ref[0:seg, :] * gy[0:seg, :]).astype(o_ref.dtype)
    for s in range(1, LRU_SEGMENTS):
        rows = slice(s * seg, (s + 1) * seg)
        o_ref[rows, :] = ((hs_ref[rows, :] + ps_ref[rows, :] * h) * gy[rows, :]).astype(o_ref.dtype)
        h = hs[s] + ps[s - 1] * h
    h_ref[...] = h

    @pl.when(l == pl.num_programs(1) - 1)
    def _():
        hlast_ref[0] = h


def conv_lru(pf, pb, cw, cb, wax, ba, bx, lam, h0, buf, layer, *, bsz, seq, d, tl):
    assert seq % tl == 0 and tl >= CONV_W - 1 and tl % LRU_SEGMENTS == 0
    nl = seq // tl
    n_blocks, bw = wax.shape[0], wax.shape[1]
    vec = pl.BlockSpec((1, d), lambda b, l: (0, 0))
    est = 2 * tl * d * (4 + 2) + 2 * tl * d * 2 + (3 * tl + CONV_PAD) * d * 4 + 8 * tl * d * 4
    kern = functools.partial(_lru_kernel, tl=tl, n_blocks=n_blocks, bw=bw)
    return pl.pallas_call(
        kern,
        grid=(bsz, nl),
        in_specs=[
            pl.BlockSpec((tl, d), lambda b, l: (b * nl + l, IN_SEC_RX)),
            pl.BlockSpec((tl, d), lambda b, l: (b * nl + l, IN_SEC_RY - IN_F32_SECTIONS)),
            pl.BlockSpec((CONV_W, d), lambda b, l: (0, 0)),
            vec,
            pl.BlockSpec((n_blocks, bw, 2 * bw), lambda b, l: (0, 0, 0)),
            vec, vec, vec,
            pl.BlockSpec((None, 1, 1, d), lambda b, l: (layer, b, 0, 0)),
            pl.BlockSpec((None, 1, CONV_W - 1, d), lambda b, l: (layer, b, 0, 0)),
        ],
        out_specs=[
            pl.BlockSpec((tl, d), lambda b, l: (b * nl + l, 0)),
            pl.BlockSpec((1, 1, d), lambda b, l: (b, 0, 0)),
        ],
        out_shape=[
            jax.ShapeDtypeStruct((bsz * seq, d), BF16),
            jax.ShapeDtypeStruct((bsz, 1, d), F32),
        ],
        scratch_shapes=[
            pltpu.VMEM((CONV_PAD + tl, d), F32),
            pltpu.VMEM((tl, d), F32),
            pltpu.VMEM((tl, d), F32),
            pltpu.VMEM((tl, d), F32),
            pltpu.VMEM((tl, d), F32),
            pltpu.VMEM((1, d), F32),
        ],
        compiler_params=_params(("parallel", "arbitrary"), est),
        name="conv_lru",
    )(pf, pb, cw, cb, wax, ba, bx, lam, h0, buf)


def _mem_attn_kernel(q_ref, k_ref, v_ref, o_ref, *, scale, heads, hd):
    cols = [slice(h * hd, (h + 1) * hd) for h in range(heads)]
    scores = [_dot_nt(q_ref[:, c], k_ref[0, :, c].astype(BF16)) * scale for c in cols]
    probs = [jnp.exp(s - jnp.max(s, axis=-1, keepdims=True)) for s in scores]
    outs = [_dot(p.astype(BF16), v_ref[0, :, c].astype(BF16)) for p, c in zip(probs, cols)]
    for p, o, c in zip(probs, outs, cols):
        o_ref[:, c] = (o / jnp.sum(p, axis=-1, keepdims=True)).astype(o_ref.dtype)


def mem_attn(pb, mem_k, mem_v, layer, *, bsz, seq, heads, hd, col0, k_col, v_col, tl):
    assert seq % tl == 0
    nl = seq // tl
    n_mem = mem_k.shape[2]
    d_c = heads * hd
    est = 4 * tl * d_c * 2 + 4 * n_mem * d_c * 4 + 6 * tl * n_mem * 4
    kern = functools.partial(_mem_attn_kernel, scale=1.0 / math.sqrt(hd), heads=heads, hd=hd)
    return pl.pallas_call(
        kern,
        grid=(bsz, nl),
        in_specs=[
            pl.BlockSpec((tl, d_c), lambda b, l: (b * nl + l, col0)),
            pl.BlockSpec((None, 1, n_mem, d_c), lambda b, l: (layer, b, 0, k_col)),
            pl.BlockSpec((None, 1, n_mem, d_c), lambda b, l: (layer, b, 0, v_col)),
        ],
        out_specs=pl.BlockSpec((tl, d_c), lambda b, l: (b * nl + l, 0)),
        out_shape=jax.ShapeDtypeStruct((bsz * seq, d_c), BF16),
        compiler_params=_params(("parallel", "parallel"), est),
        name="mem_attn",
    )(pb, mem_k, mem_v)


def _merge_kernel(x_ref, oa_ref, ob_ref, oc_ref, g0_ref, g1_ref, g2_ref,
                  wa_ref, wb_ref, wc_ref, wo_ref, gn_ref, *rest, n_side):
    side_in, y_ref, side_out = rest[:n_side], rest[n_side], rest[n_side + 1:]
    _run_side_casts(side_in, side_out)
    m = g0_ref[...].astype(F32) * _dot(oa_ref[...], wa_ref[...])
    m = m + g1_ref[...].astype(F32) * _dot(ob_ref[...], wb_ref[...])
    m = m + g2_ref[...].astype(F32) * _dot(oc_ref[...], wc_ref[...])
    z = _dot(m.astype(BF16), wo_ref[...])
    y_ref[...] = x_ref[...] + _rms(z, gn_ref[...])


def merge(x, oa, ob, oc, pb, wa, wb, wc, wo, gn, *, col_gates, tm, side=()):
    t, d = x.shape
    db = oa.shape[1]
    assert t % tm == 0
    row = lambda i: (i, 0)
    const = lambda i: (0, 0)
    wspec = lambda rows: pl.BlockSpec((None, rows, d), lambda i: (0, 0, 0), pipeline_mode=pl.Buffered(1))
    gate_specs = [pl.BlockSpec((tm, d), lambda i, k=k: (i, col_gates + k)) for k in range(N_GATES)]
    side_in, side_out, side_shapes, side_vmem = _side_cast_specs(side, t // tm, 1)
    est = (4 * tm * d * 4 + 6 * tm * db * 2 + 6 * tm * d * 2
           + 3 * db * d * 2 + d * d * 2 + 6 * tm * d * 4 + side_vmem)
    return pl.pallas_call(
        functools.partial(_merge_kernel, n_side=len(side)),
        grid=(t // tm,),
        in_specs=[
            pl.BlockSpec((tm, d), row),
            pl.BlockSpec((tm, db), row), pl.BlockSpec((tm, db), row), pl.BlockSpec((tm, db), row),
            *gate_specs,
            wspec(db), wspec(db), wspec(db), wspec(d),
            pl.BlockSpec((1, d), const),
            *side_in,
        ],
        out_specs=[pl.BlockSpec((tm, d), row), *side_out],
        out_shape=[jax.ShapeDtypeStruct((t, d), F32), *side_shapes],
        compiler_params=_params(("parallel",), est),
        name="merge",
    )(x, oa, ob, oc, pb, pb, pb, wa, wb, wc, wo, gn, *[w_ for w_, _ in side])


def _ffn_kernel(x_ref, gpre_ref, wg_ref, wu_ref, wd_ref, gpost_ref, *rest, n_side):
    side_in, y_ref, side_out, (h_ref, acc_ref) = (
        rest[:n_side], rest[n_side], rest[n_side + 1:2 * n_side + 1], rest[2 * n_side + 1:])
    _run_side_casts(side_in, side_out)
    j = pl.program_id(1)

    @pl.when(j == 0)
    def _():
        h_ref[...] = _rms(x_ref[...], gpre_ref[...]).astype(BF16)
        acc_ref[...] = jnp.zeros_like(acc_ref)

    h = h_ref[...]
    gt = _dot(h, wg_ref[...])
    up = _dot(h, wu_ref[...])
    act = (jax.nn.silu(gt) * up).astype(BF16)
    acc_ref[...] += _dot(act, wd_ref[...])

    @pl.when(j == pl.num_programs(1) - 1)
    def _():
        y_ref[...] = x_ref[...] + _rms(acc_ref[...], gpost_ref[...])


def ffn(x, gpre, w_gu, w_down, gpost, *, tm, tf, side=()):
    t, d = x.shape
    d_ff = w_down.shape[1]
    assert t % tm == 0 and d_ff % tf == 0
    nf = d_ff // tf
    grid = (t // tm, nf)
    side_in, side_out, side_shapes, side_vmem = _side_cast_specs(side, *grid)
    est = 4 * tm * d * 4 + 2 * 3 * d * tf * 2 + tm * d * 2 + tm * d * 4 + 4 * tm * tf * 4 + side_vmem
    return pl.pallas_call(
        functools.partial(_ffn_kernel, n_side=len(side)),
        grid=grid,
        in_specs=[
            pl.BlockSpec((tm, d), lambda i, j: (i, 0)),
            pl.BlockSpec((1, d), lambda i, j: (0, 0)),
            pl.BlockSpec((None, d, tf), lambda i, j: (0, 0, j)),
            pl.BlockSpec((None, d, tf), lambda i, j: (0, 0, nf + j)),
            pl.BlockSpec((None, tf, d), lambda i, j: (0, j, 0)),
            pl.BlockSpec((1, d), lambda i, j: (0, 0)),
            *side_in,
        ],
        out_specs=[pl.BlockSpec((tm, d), lambda i, j: (i, 0)), *side_out],
        out_shape=[jax.ShapeDtypeStruct((t, d), F32), *side_shapes],
        scratch_shapes=[pltpu.VMEM((tm, d), BF16), pltpu.VMEM((tm, d), F32)],
        compiler_params=_params(("parallel", "arbitrary"), est),
        name="ffn",
    )(x, gpre, w_gu, w_gu, w_down, gpost, *[w_ for w_, _ in side])


def _row_tile(n, target):
    t = min(n, target)
    while n % t:
        t //= 2
    return t


SIDE_HOSTS = {
    "in_proj": ("w_in",),
    "merge": ("w_branch_a", "w_branch_b", "w_branch_c", "w_out", "mem_w_kv"),
    "ffn": ("ffn_w_gu", "ffn_w_down"),
}


def _trunk_layer(x2, bsz, seq, mem, state, lb, w, p, next_w=None):
    t, d = x2.shape
    s_hg, h_lru, conv_buf, state_layer = state
    mem_k, mem_v, mem_layer, k_col, v_col = mem
    heads, dk = s_hg.shape[2], s_hg.shape[3]
    d_a = heads * dk
    d_b = h_lru.shape[-1]
    mem_heads, hd = p["mem_heads"], p["mem_hd"]
    if seq < CONV_W - 1:
        raise NotImplementedError("sequence shorter than the conv history")

    def side(host):
        return [] if next_w is None else [(next_w[0][name], next_w[1]) for name in SIDE_HOSTS[host]]

    casts = {}

    def keep(host, outs):
        casts.update(zip(SIDE_HOSTS[host], outs))

    pf, pb, *extra = in_proj(x2, p["norm_pre_mix"], w["w_in"], lb, p["b_gate"], sec=d_a,
                             tm=_row_tile(t, 1024), tn=1024, side=side("in_proj"))
    keep("in_proj", extra)
    n_bf_sections = pb.shape[1] // d_a

    o_a, s_new = hgrn2(pf, pb, p["hgrn_out_norm"], s_hg, state_layer, bsz=bsz, seq=seq, heads=heads, dk=dk,
                       tl=_row_tile(seq, 2048))
    o_b, h_last = conv_lru(pf, pb, p["conv_w"], p["conv_b"], p["lru_wax"], p["lru_ba"], p["lru_bx"],
                           p["lru_lambda"], h_lru, conv_buf, state_layer,
                           bsz=bsz, seq=seq, d=d_b, tl=_row_tile(seq, 512))
    o_c = mem_attn(pb, mem_k, mem_v, mem_layer, bsz=bsz, seq=seq, heads=mem_heads, hd=hd,
                   col0=n_bf_sections - 1, k_col=k_col, v_col=v_col, tl=_row_tile(seq, 1024))

    gate_col = (IN_SEC_GATES - IN_F32_SECTIONS) * d_a
    assert gate_col % d == 0
    x2, *extra = merge(x2, o_a, o_b, o_c, pb, w["w_branch_a"], w["w_branch_b"], w["w_branch_c"], w["w_out"],
                       p["norm_post_mix"], col_gates=gate_col // d, tm=_row_tile(t, 256), side=side("merge"))
    keep("merge", extra)
    x2, *extra = ffn(x2, p["norm_pre_ffn"], w["ffn_w_gu"], w["ffn_w_down"], p["norm_post_ffn"],
                     tm=_row_tile(t, 512), tf=512, side=side("ffn"))
    keep("ffn", extra)

    rx_tail = pf.reshape(bsz, seq, -1)[:, seq - (CONV_W - 1):, IN_SEC_RX * d_a:(IN_SEC_RX + 1) * d_a]
    return x2, s_new, h_last.reshape(bsz, d_b), rx_tail, casts


def kernel(x_prompt, x_sample, state_hgrn, state_lru, state_conv, cache_mem_k, cache_mem_v, mem_prompt, norm_mem, mem_w_kv, hgrn_lower_bound, norm_pre_mix, w_in, b_gate, hgrn_out_norm, conv_w, conv_b, lru_wa, lru_ba, lru_wx, lru_bx, lru_lambda, w_branch_a, w_branch_b, w_branch_c, w_out, norm_post_mix, norm_pre_ffn, ffn_w_gu, ffn_w_down, norm_post_ffn):
    depth = w_in.shape[0]
    bp, sp, d = x_prompt.shape
    bs, ss, _ = x_sample.shape
    _, _, heads, dk, dv = state_hgrn.shape
    d_a = heads * dk
    d_b = state_lru.shape[-1]
    n_mem, mem_heads, hd = cache_mem_k.shape[2:]
    d_c = mem_heads * hd
    assert dk == dv and d_b == d_a and d_c == d_a and d == 2 * d_a
    assert b_gate.shape[1] == N_GATES * d

    sm = jax.nn.softmax(hgrn_lower_bound.astype(F32), axis=0)
    lbs = jnp.cumsum(sm, axis=0) - sm[0:1]

    xp = x_prompt.reshape(bp * sp, d)
    xs = x_sample.reshape(bs * ss, d)
    mem2 = mem_prompt.reshape(bp * n_mem, d)
    zero_state = (jnp.zeros((1, bp, heads, dk, dv), F32), jnp.zeros((1, bp, 1, d_b), F32),
                  jnp.zeros((1, bp, CONV_W - 1, d_b), F32), 0)
    cache_k = cache_mem_k.reshape(depth, bs, n_mem, d_c)
    cache_v = cache_mem_v.reshape(depth, bs, n_mem, d_c)
    lru_s4 = state_lru.reshape(depth, bs, 1, d_b)

    w_f32 = dict(w_in=w_in, w_branch_a=w_branch_a, w_branch_b=w_branch_b, w_branch_c=w_branch_c, w_out=w_out,
                 ffn_w_gu=ffn_w_gu, ffn_w_down=ffn_w_down, mem_w_kv=mem_w_kv)
    assert set(w_f32) == {name for names in SIDE_HOSTS.values() for name in names}
    w = {name: cast_bf16(a, 0) for name, a in w_f32.items()}

    outs = {k: [] for k in ("hg_p", "lru_p", "conv_p", "mk_p", "mv_p", "hg_s", "lru_s", "conv_s")}
    for l in range(depth):
        row = lambda a: a[l].reshape(1, -1)
        p = dict(
            mem_heads=mem_heads, mem_hd=hd,
            norm_pre_mix=row(norm_pre_mix), b_gate=row(b_gate),
            hgrn_out_norm=row(hgrn_out_norm), conv_w=conv_w[l], conv_b=row(conv_b),
            lru_wax=jnp.concatenate([lru_wa[l], lru_wx[l]], axis=-1).astype(BF16),
            lru_ba=row(lru_ba), lru_bx=row(lru_bx), lru_lambda=row(lru_lambda),
            norm_post_mix=row(norm_post_mix), norm_pre_ffn=row(norm_pre_ffn),
            norm_post_ffn=row(norm_post_ffn),
        )
        lb = lbs[l].reshape(1, -1)

        kv = norm_matmul(mem2, row(norm_mem), w["mem_w_kv"], tm=_row_tile(bp * n_mem, 512), tn=1024)
        kv4 = kv.reshape(1, bp, n_mem, 2 * d_c)
        next_w = (w_f32, l + 1) if l + 1 < depth else None
        xp, s1, h1, c1, w_next = _trunk_layer(xp, bp, sp, (kv4, kv4, 0, 0, 1), zero_state, lb, w, p, next_w)
        outs["hg_p"].append(s1); outs["lru_p"].append(h1); outs["conv_p"].append(c1)
        outs["mk_p"].append(kv4[0, :, :, :d_c].reshape(bp, n_mem, mem_heads, hd))
        outs["mv_p"].append(kv4[0, :, :, d_c:].reshape(bp, n_mem, mem_heads, hd))

        xs, s2, h2, c2, _ = _trunk_layer(xs, bs, ss, (cache_k, cache_v, l, 0, 0),
                                         (state_hgrn, lru_s4, state_conv, l), lb, w, p)
        outs["hg_s"].append(s2); outs["lru_s"].append(h2); outs["conv_s"].append(c2)
        w = w_next

    st = {k: jnp.stack(v) for k, v in outs.items()}
    return (xp.reshape(bp, sp, d), xs.reshape(bs, ss, d), st["hg_p"], st["lru_p"], st["conv_p"],
            st["mk_p"], st["mv_p"], st["hg_s"], st["lru_s"], st["conv_s"])
```

```python
import functools
import math

import numpy as np
import jax
import jax.numpy as jnp
from jax import lax
from jax.experimental import pallas as pl
from jax.experimental.pallas import tpu as pltpu

F32 = jnp.float32
BF16 = jnp.bfloat16

EPS = 1e-6
LRU_C = 8.0
CHUNK = 64
SUB = 16
HALF = 8
N_PIECES = 3
CONV_W = 4
CONV_PAD = 8
LRU_SEGMENTS = 4
N_GATES = 3

LANES = 128
BF16_SUBLANES = 16

V7X_VMEM_BYTES = 64 * 1024 * 1024
VMEM_LIMIT_CAP = 56 * 1024 * 1024


def _vmem_limit(estimate_bytes):
    return int(min(VMEM_LIMIT_CAP, max(16 * 1024 * 1024, estimate_bytes * 5 // 4)))


def _params(sem, vmem_estimate):
    return pltpu.CompilerParams(dimension_semantics=sem, vmem_limit_bytes=_vmem_limit(vmem_estimate))


def _rms(x, g):
    ms = jnp.mean(x * x, axis=-1, keepdims=True)
    return x * lax.rsqrt(ms + EPS) * g


def _dot(a, b):
    return jnp.dot(a, b, preferred_element_type=F32)


def _dot_nt(a, b):
    return lax.dot_general(a, b, (((1,), (1,)), ((), ())), preferred_element_type=F32)


def _dot_tn(a, b):
    return lax.dot_general(a, b, (((0,), (0,)), ((), ())), preferred_element_type=F32)


def _cast_kernel(w_ref, o_ref):
    o_ref[...] = w_ref[...].astype(o_ref.dtype)


CAST_BLOCK_BYTES = 4 * 1024 * 1024


def cast_bf16(w, layer):
    _, r, c = w.shape
    tr = r
    while tr * c * 4 > CAST_BLOCK_BYTES and tr % 32 == 0:
        tr //= 2
    return pl.pallas_call(
        _cast_kernel,
        grid=(r // tr,),
        in_specs=[pl.BlockSpec((None, tr, c), lambda i: (layer, i, 0))],
        out_specs=pl.BlockSpec((None, tr, c), lambda i: (0, i, 0)),
        out_shape=jax.ShapeDtypeStruct((1, r, c), BF16),
        compiler_params=_params(("parallel",), 2 * tr * c * 6),
        name="cast_bf16",
    )(w)


def _side_cast_specs(side, n_i, n_j):
    in_specs, out_specs, out_shapes, vmem = [], [], [], 0
    for w, layer in side:
        _, r, c = w.shape
        assert r % n_i == 0 and (r // n_i) % BF16_SUBLANES == 0
        br = r // n_i
        split = n_j > 1 and c % n_j == 0 and (c // n_j) % LANES == 0
        bc = c // n_j if split else c
        in_specs.append(pl.BlockSpec(
            (None, br, bc), lambda i, *rest, layer=layer, split=split: (layer, i, rest[0] if split else 0)))
        out_specs.append(pl.BlockSpec(
            (None, br, bc), lambda i, *rest, split=split: (0, i, rest[0] if split else 0)))
        out_shapes.append(jax.ShapeDtypeStruct((1, r, c), BF16))
        vmem += 2 * br * bc * 6
    return in_specs, out_specs, out_shapes, vmem


def _run_side_casts(side_in, side_out):
    for w_ref, o_ref in zip(side_in, side_out):
        o_ref[...] = w_ref[...].astype(o_ref.dtype)


HEAD_MERGE_ROWS = 128


def _head_merge_specs(caches, n_i, n_j):
    in_specs, out_specs, out_shapes, vmem = [], [], [], 0
    n_blocks = 0
    for c in caches:
        depth, bsz, n_mem, heads, hd = c.shape
        rows = min(HEAD_MERGE_ROWS, n_mem)
        assert n_mem % rows == 0
        per_b = n_mem // rows
        n_blocks = depth * bsz * per_b
        assert n_blocks <= n_i * n_j

        def where(i, j, bsz=bsz, per_b=per_b, n_blocks=n_blocks):
            g = jnp.minimum(i * n_j + j, n_blocks - 1)
            return g // (bsz * per_b), (g // per_b) % bsz, g % per_b

        in_specs.append(pl.BlockSpec((None, 1, rows, heads, hd), lambda i, j, where=where: (*where(i, j), 0, 0)))
        out_specs.append(pl.BlockSpec((None, 1, rows, heads * hd), lambda i, j, where=where: (*where(i, j), 0)))
        out_shapes.append(jax.ShapeDtypeStruct((depth, bsz, n_mem, heads * hd), c.dtype))
        vmem += 2 * rows * (8 * hd + heads * hd) * 4
    return in_specs, out_specs, out_shapes, vmem, n_blocks


def _run_head_merges(ins, outs, n_blocks):
    if not ins:
        return
    step = pl.program_id(0) * pl.num_programs(1) + pl.program_id(1)

    @pl.when(step < n_blocks)
    def _():
        for c_ref, o_ref in zip(ins, outs):
            heads, hd = c_ref.shape[2], c_ref.shape[3]
            for h in range(heads):
                o_ref[0, :, h * hd:(h + 1) * hd] = c_ref[0, :, h, :]


def _norm_matmul_kernel(x_ref, g_ref, w_ref, o_ref, h_ref):
    @pl.when(pl.program_id(1) == 0)
    def _():
        h_ref[...] = _rms(x_ref[...], g_ref[...]).astype(BF16)

    o_ref[...] = _dot(h_ref[...], w_ref[...]).astype(o_ref.dtype)


def norm_matmul(x, g, w, *, tm, tn, out_dtype=F32):
    t, d = x.shape
    n = w.shape[2]
    assert t % tm == 0 and n % tn == 0
    est = 2 * tm * d * 4 + 2 * d * tn * 2 + 2 * tm * tn * 4 + tm * d * 2
    return pl.pallas_call(
        _norm_matmul_kernel,
        grid=(t // tm, n // tn),
        in_specs=[
            pl.BlockSpec((tm, d), lambda i, j: (i, 0)),
            pl.BlockSpec((1, d), lambda i, j: (0, 0)),
            pl.BlockSpec((None, d, tn), lambda i, j: (0, 0, j)),
        ],
        out_specs=pl.BlockSpec((tm, tn), lambda i, j: (i, j)),
        out_shape=jax.ShapeDtypeStruct((t, n), out_dtype),
        scratch_shapes=[pltpu.VMEM((tm, d), BF16)],
        compiler_params=_params(("parallel", "arbitrary"), est),
        name="norm_matmul",
    )(x, g, w)


IN_F32_SECTIONS = 2
IN_SEC_LOGF, IN_SEC_RX, IN_SEC_Q, IN_SEC_V, IN_SEC_OG, IN_SEC_RY, IN_SEC_GATES = range(7)


def _in_proj_kernel(perm_ref, x_ref, g_ref, w_ref, lb_ref, bg_ref, *rest, tps, n_gate_secs, sub_rows, n_side):
    del perm_ref
    side_in, (of_ref, ob_ref), side_out, (h_ref,) = (
        rest[:n_side], rest[n_side:n_side + 2], rest[n_side + 2:2 * n_side + 2], rest[2 * n_side + 2:])
    _run_side_casts(side_in, side_out)
    j = pl.program_id(1)
    sec = j // tps

    @pl.when(j == 0)
    def _():
        h_ref[...] = _rms(x_ref[...], g_ref[...]).astype(BF16)

    tm = h_ref.shape[0]

    def run(out_ref, act):
        for r in range(tm // sub_rows):
            rs = slice(r * sub_rows, (r + 1) * sub_rows)
            out_ref[rs, :] = act(_dot(h_ref[rs, :], w_ref[...])).astype(out_ref.dtype)

    def log_forget(a):
        lb = lb_ref[...]
        return jnp.log(lb + (1.0 - lb) * jax.nn.sigmoid(a))

    sec_cq = IN_SEC_GATES + n_gate_secs
    pl.when(sec == IN_SEC_LOGF)(lambda: run(of_ref, log_forget))
    pl.when(sec == IN_SEC_RX)(lambda: run(of_ref, lambda a: a))
    pl.when((sec == IN_SEC_Q) | (sec == IN_SEC_OG))(lambda: run(ob_ref, jax.nn.silu))
    pl.when((sec == IN_SEC_V) | (sec == sec_cq))(lambda: run(ob_ref, lambda a: a))
    pl.when(sec == IN_SEC_RY)(lambda: run(ob_ref, jax.nn.gelu))
    pl.when((sec >= IN_SEC_GATES) & (sec < sec_cq))(
        lambda: run(ob_ref, lambda a: jax.nn.sigmoid(a + bg_ref[...])))


def in_proj(x, g, w, lb, bg, *, sec, tm, tn, side=()):
    t, d = x.shape
    n = w.shape[2]
    assert t % tm == 0 and sec % tn == 0 and n % sec == 0
    tps = sec // tn
    n_sec = n // sec
    n_gate_secs = bg.shape[1] // sec
    assert n_sec == IN_SEC_GATES + n_gate_secs + 1
    perm = jnp.asarray([1, 4, 0, 2, 3, 5] + list(range(7, 7 + n_gate_secs)) + [6], jnp.int32)
    nf = IN_F32_SECTIONS * tps
    sub_rows = min(tm, IN_PROJ_EPILOGUE_ROWS)
    grid = (t // tm, n // tn)
    side_in, side_out, side_shapes, side_vmem = _side_cast_specs(side, *grid)
    est = (2 * tm * d * 4 + 2 * d * tn * 2 + 2 * tm * tn * 4 + 2 * tm * tn * 2 + tm * d * 2
           + 6 * sub_rows * tn * 4 + side_vmem)
    kern = functools.partial(_in_proj_kernel, tps=tps, n_gate_secs=n_gate_secs, sub_rows=sub_rows,
                             n_side=len(side))
    grid_spec = pltpu.PrefetchScalarGridSpec(
        num_scalar_prefetch=1,
        grid=grid,
        in_specs=[
            pl.BlockSpec((tm, d), lambda i, j, perm: (i, 0)),
            pl.BlockSpec((1, d), lambda i, j, perm: (0, 0)),
            pl.BlockSpec((None, d, tn), lambda i, j, perm: (0, 0, perm[j // tps] * tps + j % tps)),
            pl.BlockSpec((1, tn), lambda i, j, perm: (0, jnp.minimum(j, tps - 1))),
            pl.BlockSpec((1, tn), lambda i, j, perm: (0, jnp.clip(j - IN_SEC_GATES * tps, 0,
                                                                   n_gate_secs * tps - 1))),
            *side_in,
        ],
        out_specs=[
            pl.BlockSpec((tm, tn), lambda i, j, perm: (i, jnp.minimum(j, nf - 1))),
            pl.BlockSpec((tm, tn), lambda i, j, perm: (i, jnp.maximum(j - nf, 0))),
            *side_out,
        ],
        scratch_shapes=[pltpu.VMEM((tm, d), BF16)],
    )
    return pl.pallas_call(
        kern,
        grid_spec=grid_spec,
        out_shape=[
            jax.ShapeDtypeStruct((t, IN_F32_SECTIONS * sec), F32),
            jax.ShapeDtypeStruct((t, n - IN_F32_SECTIONS * sec), BF16),
            *side_shapes,
        ],
        compiler_params=_params(("parallel", "arbitrary"), est),
        name="in_proj",
    )(perm, x, g, w, lb, bg, *[w_ for w_, _ in side])


def _hgrn_consts(chunk, dk):
    t = np.arange(chunk)[:, None]
    s = np.arange(chunk)[None, :]
    cum = np.concatenate([(s <= t).astype(np.float32)] * N_PIECES, axis=1)
    lane_blk = np.arange(HALF * dk)[:, None] // dk
    sel = (lane_blk == (np.arange(chunk)[None, :] % HALF)).astype(np.float32)
    return jnp.asarray(cum, BF16), jnp.asarray(sel, BF16)


def _hgrn_kernel(q_ref, lf_ref, v_ref, og_ref, gn_ref, s0_ref, cum_ref, sel_ref,
                 o_ref, sfin_ref, st_ref, b_ref, k_ref, u_ref, sb_ref, *, chunk, n_chunks, dk):
    l = pl.program_id(2)
    n_sub = chunk // SUB
    width = q_ref.shape[1]
    n_heads = width // dk
    heads = [slice(h * dk, (h + 1) * dk) for h in range(n_heads)]

    @pl.when(l == 0)
    def _():
        for h in range(n_heads):
            st_ref[h] = s0_ref[0, h].T

    q = q_ref[...].astype(F32)
    lf = lf_ref[...]
    kk = 1.0 - jnp.exp(lf)
    vb = v_ref[...]
    k_ref[...] = kk

    p0 = lf.astype(BF16)
    r1 = lf - p0.astype(F32)
    p1 = r1.astype(BF16)
    p2 = (r1 - p1.astype(F32)).astype(BF16)
    cum = cum_ref[...]
    b = jnp.concatenate(
        [_dot(cum, jnp.concatenate([p[c * chunk:(c + 1) * chunk, :] for p in (p0, p1, p2)], axis=0))
         for c in range(n_chunks)], axis=0)
    b_ref[...] = b
    tl = n_chunks * chunk

    def rows_of(ref, group, offset):
        return jnp.concatenate(
            [jnp.broadcast_to(ref[pl.ds(g * group + offset, 1), :], (group, width)) for g in range(tl // group)],
            axis=0)

    b_end = rows_of(b_ref, SUB, SUB - 1)
    b_mid = rows_of(b_ref, SUB, HALF - 1)
    b_last = rows_of(b_ref, chunk, chunk - 1)

    qe = (q * jnp.exp(b)).astype(BF16)
    k_dec = (kk * jnp.exp(b_end - b)).astype(BF16)
    k_end = (kk * jnp.exp(b_last - b)).astype(BF16)
    decay = [jnp.exp(b_ref[pl.ds((c + 1) * chunk - 1, 1), :]) for c in range(n_chunks)]
    q_dec = [(q * jnp.exp(jnp.minimum(b - rows_of(b_ref, chunk, (j + 1) * SUB - 1), 0.0))).astype(BF16)
             for j in range(n_sub - 1)]
    q_mid = (q * jnp.exp(jnp.minimum(b - b_mid, 0.0))).astype(BF16)
    k_mid = (kk * jnp.exp(jnp.minimum(b_mid - b, 0.0))).astype(BF16)

    w = [(q * jnp.exp(jnp.minimum(b - rows_of(b_ref, HALF, u), 0.0)) * rows_of(k_ref, HALF, u)).astype(BF16)
         for u in range(HALF)]
    sel = sel_ref[...]
    diag = [_dot(jnp.concatenate([wu[:, hc] for wu in w], axis=1), sel) for hc in heads]

    row = lax.broadcasted_iota(jnp.int32, (chunk, chunk), 0)
    col = lax.broadcasted_iota(jnp.int32, (chunk, chunk), 1)
    row_blk = row // SUB
    col_blk = col // SUB
    mid_mask = (col_blk == row_blk) & (row % SUB >= HALF) & (col % SUB < HALF)
    diag_mask = (col // HALF == row // HALF) & (col <= row)
    eye = (lax.broadcasted_iota(jnp.int32, (dk, dk), 0)
           == lax.broadcasted_iota(jnp.int32, (dk, dk), 1)).astype(F32).astype(BF16)

    chunks = [slice(c * chunk, (c + 1) * chunk) for c in range(n_chunks)]
    units = [(h, hc, c, rows) for h, hc in enumerate(heads) for c, rows in enumerate(chunks)]
    cross, v_t = [], []
    for h, hc, c, rows in units:
        qd = jnp.concatenate([qj[rows, hc] for qj in q_dec] + [q_mid[rows, hc]], axis=0)
        kd = jnp.concatenate([k_dec[rows, hc], k_mid[rows, hc]], axis=0)
        cross.append(_dot_nt(qd, kd))
        v_t.append(_dot_nt(eye, vb[rows, hc]).astype(BF16))

    intra = []
    for i, (h, hc, c, rows) in enumerate(units):
        m = cross[i]
        attn = jnp.where(mid_mask, m[(n_sub - 1) * chunk:, chunk:], 0.0)
        for j in range(n_sub - 1):
            attn = jnp.where((col_blk == j) & (row_blk > j), m[j * chunk:(j + 1) * chunk, :chunk], attn)
        attn = jnp.where(diag_mask, diag[h][rows, :], attn)
        intra.append(_dot(attn.astype(BF16), vb[rows, hc]))
        u_ref[i] = _dot(v_t[i], k_end[rows, hc])

    st = [st_ref[h] for h in range(n_heads)]
    for i, (h, hc, c, rows) in enumerate(units):
        sb_ref[i] = st[h].astype(BF16)
        st[h] = st[h] * decay[c][:, hc] + u_ref[i]
    for h in range(n_heads):
        st_ref[h] = st[h]

    inter = [_dot_nt(qe[rows, hc], sb_ref[i]) for i, (h, hc, c, rows) in enumerate(units)]
    gn = gn_ref[...]
    for h, hc in enumerate(heads):
        per_chunk = range(h * n_chunks, (h + 1) * n_chunks)
        o = jnp.concatenate([intra[i] + inter[i] for i in per_chunk], axis=0)
        ms = jnp.mean(o * o, axis=-1, keepdims=True)
        o = o * lax.rsqrt(ms + EPS) * gn[:, hc] * og_ref[:, hc].astype(F32)
        o_ref[:, hc] = o.astype(o_ref.dtype)

    @pl.when(l == pl.num_programs(2) - 1)
    def _():
        for h in range(n_heads):
            sfin_ref[0, h] = st[h].T


HGRN_UNITS_PER_STEP = 32


def hgrn2(pf, pb, gn, s0, layer, *, bsz, seq, heads, dk, tl):
    chunk = min(CHUNK, seq)
    assert seq % tl == 0 and tl % chunk == 0 and chunk % SUB == 0
    nl = seq // tl
    d_a = heads * dk
    n_chunks = tl // chunk
    hb = max(1, min(heads, HGRN_UNITS_PER_STEP // n_chunks))
    while heads % hb:
        hb -= 1
    n_groups = heads // hb
    width = hb * dk
    n_units = hb * n_chunks

    def sec(k):
        return pl.BlockSpec((tl, width), lambda b, h, l, k=k: (b * nl + l, k * n_groups + h))

    cum, sel = _hgrn_consts(chunk, dk)
    est = (2 * tl * width * (4 + 3 * 2) + 2 * tl * width * 2 + 6 * hb * dk * dk * 4 + 2 * tl * width * 4
           + n_units * dk * dk * 6 + 2 * (cum.size + sel.size) * 2 + 24 * tl * width * 4)
    kern = functools.partial(_hgrn_kernel, chunk=chunk, n_chunks=n_chunks, dk=dk)
    return pl.pallas_call(
        kern,
        grid=(bsz, n_groups, nl),
        in_specs=[
            sec(IN_SEC_Q - IN_F32_SECTIONS), sec(IN_SEC_LOGF),
            sec(IN_SEC_V - IN_F32_SECTIONS), sec(IN_SEC_OG - IN_F32_SECTIONS),
            pl.BlockSpec((1, width), lambda b, h, l: (0, h)),
            pl.BlockSpec((None, 1, hb, dk, dk), lambda b, h, l: (layer, b, h, 0, 0)),
            pl.BlockSpec(cum.shape, lambda b, h, l: (0, 0)),
            pl.BlockSpec(sel.shape, lambda b, h, l: (0, 0)),
        ],
        out_specs=[
            pl.BlockSpec((tl, width), lambda b, h, l: (b * nl + l, h)),
            pl.BlockSpec((1, hb, dk, dk), lambda b, h, l: (b, h, 0, 0)),
        ],
        out_shape=[
            jax.ShapeDtypeStruct((bsz * seq, d_a), BF16),
            jax.ShapeDtypeStruct((bsz, heads, dk, dk), F32),
        ],
        scratch_shapes=[
            pltpu.VMEM((hb, dk, dk), F32),
            pltpu.VMEM((tl, width), F32),
            pltpu.VMEM((tl, width), F32),
            pltpu.VMEM((n_units, dk, dk), F32),
            pltpu.VMEM((n_units, dk, dk), BF16),
        ],
        compiler_params=_params(("parallel", "parallel", "arbitrary"), est),
        name="hgrn2",
    )(pb, pf, pb, pb, gn, s0, cum, sel)


def _lru_kernel(rx_ref, gy_ref, cw_ref, cb_ref, wax_ref, ba_ref, bx_ref, lam_ref, h0_ref, buf_ref,
                o_ref, hlast_ref, xp_ref, a_ref, u_ref, hs_ref, ps_ref, h_ref, *, tl, n_blocks, bw):
    l = pl.program_id(1)
    keep = CONV_W - 1

    @pl.when(l == 0)
    def _():
        xp_ref[CONV_PAD - keep:CONV_PAD, :] = buf_ref[0]
        h_ref[...] = h0_ref[0]

    x = rx_ref[...]
    xp_ref[CONV_PAD:CONV_PAD + tl, :] = x
    cw = cw_ref[...]
    xc = xp_ref[CONV_PAD - keep:CONV_PAD - keep + tl, :] * cw[0:1, :]
    for j in range(1, CONV_W):
        xc = xc + xp_ref[CONV_PAD - keep + j:CONV_PAD - keep + j + tl, :] * cw[j:j + 1, :]
    xc = xc + cb_ref[...]
    xp_ref[CONV_PAD - keep:CONV_PAD, :] = xp_ref[CONV_PAD + tl - keep:CONV_PAD + tl, :]

    xcb = xc.astype(BF16)
    pre = [_dot(xcb[:, n * bw:(n + 1) * bw], wax_ref[n]) for n in range(n_blocks)]
    r = jax.nn.sigmoid(jnp.concatenate([pn[:, :bw] for pn in pre], axis=-1) + ba_ref[...])
    ig = jax.nn.sigmoid(jnp.concatenate([pn[:, bw:] for pn in pre], axis=-1) + bx_ref[...])
    lam = lam_ref[...]
    softplus_neg = jnp.maximum(-lam, 0.0) + jnp.log1p(jnp.exp(-jnp.abs(lam)))
    log_a = -LRU_C * r * softplus_neg
    a = jnp.exp(log_a)
    mult = jnp.sqrt(jnp.maximum(-jnp.tanh(log_a) * (a * a + 1.0), 0.0))
    a_ref[...] = a
    u_ref[...] = mult * (ig * xc)

    seg = tl // LRU_SEGMENTS
    one = jnp.ones_like(h_ref[...])

    def step(t, carry):
        hs, ps = carry
        new_h, new_p = [], []
        for s in range(LRU_SEGMENTS):
            r = s * seg + t
            a_t = a_ref[pl.ds(r, 1), :]
            h_s = a_t * hs[s] + u_ref[pl.ds(r, 1), :]
            hs_ref[pl.ds(r, 1), :] = h_s
            new_h.append(h_s)
            if s > 0:
                p_s = a_t * ps[s - 1]
                ps_ref[pl.ds(r, 1), :] = p_s
                new_p.append(p_s)
        return tuple(new_h), tuple(new_p)

    init = ((h_ref[...],) + (jnp.zeros_like(one),) * (LRU_SEGMENTS - 1), (one,) * (LRU_SEGMENTS - 1))
    hs, ps = lax.fori_loop(0, seg, step, init, unroll=min(seg, 4))
    h = hs[0]
    gy = gy_ref[...].astype(F32)
    o_ref[0:seg, :] = (hs_ref[0:seg, :] * gy[0:seg, :]).astype(o_ref.dtype)
    for s in range(1, LRU_SEGMENTS):
        rows = slice(s * seg, (s + 1) * seg)
        o_ref[rows, :] = ((hs_ref[rows, :] + ps_ref[rows, :] * h) * gy[rows, :]).astype(o_ref.dtype)
        h = hs[s] + ps[s - 1] * h
    h_ref[...] = h

    @pl.when(l == pl.num_programs(1) - 1)
    def _():
        hlast_ref[0] = h


def conv_lru(pf, pb, cw, cb, wax, ba, bx, lam, h0, buf, layer, *, bsz, seq, d, tl):
    assert seq % tl == 0 and tl >= CONV_W - 1 and tl % LRU_SEGMENTS == 0
    nl = seq // tl
    n_blocks, bw = wax.shape[0], wax.shape[1]
    vec = pl.BlockSpec((1, d), lambda b, l: (0, 0))
    est = 2 * tl * d * (4 + 2) + 2 * tl * d * 2 + (3 * tl + CONV_PAD) * d * 4 + 8 * tl * d * 4
    kern = functools.partial(_lru_kernel, tl=tl, n_blocks=n_blocks, bw=bw)
    return pl.pallas_call(
        kern,
        grid=(bsz, nl),
        in_specs=[
            pl.BlockSpec((tl, d), lambda b, l: (b * nl + l, IN_SEC_RX)),
            pl.BlockSpec((tl, d), lambda b, l: (b * nl + l, IN_SEC_RY - IN_F32_SECTIONS)),
            pl.BlockSpec((CONV_W, d), lambda b, l: (0, 0)),
            vec,
            pl.BlockSpec((n_blocks, bw, 2 * bw), lambda b, l: (0, 0, 0)),
            vec, vec, vec,
            pl.BlockSpec((None, 1, 1, d), lambda b, l: (layer, b, 0, 0)),
            pl.BlockSpec((None, 1, CONV_W - 1, d), lambda b, l: (layer, b, 0, 0)),
        ],
        out_specs=[
            pl.BlockSpec((tl, d), lambda b, l: (b * nl + l, 0)),
            pl.BlockSpec((1, 1, d), lambda b, l: (b, 0, 0)),
        ],
        out_shape=[
            jax.ShapeDtypeStruct((bsz * seq, d), BF16),
            jax.ShapeDtypeStruct((bsz, 1, d), F32),
        ],
        scratch_shapes=[
            pltpu.VMEM((CONV_PAD + tl, d), F32),
            pltpu.VMEM((tl, d), F32),
            pltpu.VMEM((tl, d), F32),
            pltpu.VMEM((tl, d), F32),
            pltpu.VMEM((tl, d), F32),
            pltpu.VMEM((1, d), F32),
        ],
        compiler_params=_params(("parallel", "arbitrary"), est),
        name="conv_lru",
    )(pf, pb, cw, cb, wax, ba, bx, lam, h0, buf)


def _mem_attn_kernel(q_ref, k_ref, v_ref, o_ref, *, scale, heads, hd):
    cols = [slice(h * hd, (h + 1) * hd) for h in range(heads)]
    scores = [_dot_nt(q_ref[:, c], k_ref[0, :, c].astype(BF16)) * scale for c in cols]
    probs = [jnp.exp(s - jnp.max(s, axis=-1, keepdims=True)) for s in scores]
    outs = [_dot(p.astype(BF16), v_ref[0, :, c].astype(BF16)) for p, c in zip(probs, cols)]
    for p, o, c in zip(probs, outs, cols):
        o_ref[:, c] = (o / jnp.sum(p, axis=-1, keepdims=True)).astype(o_ref.dtype)


def mem_attn(pb, mem_k, mem_v, layer, *, bsz, seq, heads, hd, col0, k_col, v_col, tl):
    assert seq % tl == 0
    nl = seq // tl
    n_mem = mem_k.shape[2]
    d_c = heads * hd
    est = 4 * tl * d_c * 2 + 4 * n_mem * d_c * 4 + 6 * tl * n_mem * 4
    kern = functools.partial(_mem_attn_kernel, scale=1.0 / math.sqrt(hd), heads=heads, hd=hd)
    return pl.pallas_call(
        kern,
        grid=(bsz, nl),
        in_specs=[
            pl.BlockSpec((tl, d_c), lambda b, l: (b * nl + l, col0)),
            pl.BlockSpec((None, 1, n_mem, d_c), lambda b, l: (layer, b, 0, k_col)),
            pl.BlockSpec((None, 1, n_mem, d_c), lambda b, l: (layer, b, 0, v_col)),
        ],
        out_specs=pl.BlockSpec((tl, d_c), lambda b, l: (b * nl + l, 0)),
        out_shape=jax.ShapeDtypeStruct((bsz * seq, d_c), BF16),
        compiler_params=_params(("parallel", "parallel"), est),
        name="mem_attn",
    )(pb, mem_k, mem_v)


def _merge_kernel(x_ref, oa_ref, ob_ref, oc_ref, g0_ref, g1_ref, g2_ref,
                  wa_ref, wb_ref, wc_ref, wo_ref, gn_ref, *rest, n_side):
    side_in, y_ref, side_out = rest[:n_side], rest[n_side], rest[n_side + 1:]
    _run_side_casts(side_in, side_out)
    m = g0_ref[...].astype(F32) * _dot(oa_ref[...], wa_ref[...])
    m = m + g1_ref[...].astype(F32) * _dot(ob_ref[...], wb_ref[...])
    m = m + g2_ref[...].astype(F32) * _dot(oc_ref[...], wc_ref[...])
    z = _dot(m.astype(BF16), wo_ref[...])
    y_ref[...] = x_ref[...] + _rms(z, gn_ref[...])


def merge(x, oa, ob, oc, pb, wa, wb, wc, wo, gn, *, col_gates, tm, side=()):
    t, d = x.shape
    db = oa.shape[1]
    assert t % tm == 0
    row = lambda i: (i, 0)
    const = lambda i: (0, 0)
    wspec = lambda rows: pl.BlockSpec((None, rows, d), lambda i: (0, 0, 0), pipeline_mode=pl.Buffered(1))
    gate_specs = [pl.BlockSpec((tm, d), lambda i, k=k: (i, col_gates + k)) for k in range(N_GATES)]
    side_in, side_out, side_shapes, side_vmem = _side_cast_specs(side, t // tm, 1)
    est = (4 * tm * d * 4 + 6 * tm * db * 2 + 6 * tm * d * 2
           + 3 * db * d * 2 + d * d * 2 + 6 * tm * d * 4 + side_vmem)
    return pl.pallas_call(
        functools.partial(_merge_kernel, n_side=len(side)),
        grid=(t // tm,),
        in_specs=[
            pl.BlockSpec((tm, d), row),
            pl.BlockSpec((tm, db), row), pl.BlockSpec((tm, db), row), pl.BlockSpec((tm, db), row),
            *gate_specs,
            wspec(db), wspec(db), wspec(db), wspec(d),
            pl.BlockSpec((1, d), const),
            *side_in,
        ],
        out_specs=[pl.BlockSpec((tm, d), row), *side_out],
        out_shape=[jax.ShapeDtypeStruct((t, d), F32), *side_shapes],
        compiler_params=_params(("parallel",), est),
        name="merge",
    )(x, oa, ob, oc, pb, pb, pb, wa, wb, wc, wo, gn, *[w_ for w_, _ in side])


def _ffn_kernel(x_ref, gpre_ref, wg_ref, wu_ref, wd_ref, gpost_ref, *rest, n_side, n_caches, n_cache_blocks):
    n_extra = n_side + n_caches
    extra_in, y_ref, extra_out, (h_ref, acc_ref) = (
        rest[:n_extra], rest[n_extra], rest[n_extra + 1:2 * n_extra + 1], rest[2 * n_extra + 1:])
    _run_side_casts(extra_in[:n_side], extra_out[:n_side])
    _run_head_merges(extra_in[n_side:], extra_out[n_side:], n_cache_blocks)
    j = pl.program_id(1)

    @pl.when(j == 0)
    def _():
        h_ref[...] = _rms(x_ref[...], gpre_ref[...]).astype(BF16)
        acc_ref[...] = jnp.zeros_like(acc_ref)

    h = h_ref[...]
    gt = _dot(h, wg_ref[...])
    up = _dot(h, wu_ref[...])
    act = (jax.nn.silu(gt) * up).astype(BF16)
    acc_ref[...] += _dot(act, wd_ref[...])

    @pl.when(j == pl.num_programs(1) - 1)
    def _():
        y_ref[...] = x_ref[...] + _rms(acc_ref[...], gpost_ref[...])


def ffn(x, gpre, w_gu, w_down, gpost, *, tm, tf, side=(), caches=()):
    t, d = x.shape
    d_ff = w_down.shape[1]
    assert t % tm == 0 and d_ff % tf == 0
    nf = d_ff // tf
    grid = (t // tm, nf)
    side_in, side_out, side_shapes, side_vmem = _side_cast_specs(side, *grid)
    cache_in, cache_out, cache_shapes, cache_vmem, n_cache_blocks = _head_merge_specs(caches, *grid)
    side_in, side_out, side_shapes = side_in + cache_in, side_out + cache_out, side_shapes + cache_shapes
    est = (4 * tm * d * 4 + 2 * 3 * d * tf * 2 + tm * d * 2 + tm * d * 4 + 4 * tm * tf * 4
           + side_vmem + cache_vmem)
    return pl.pallas_call(
        functools.partial(_ffn_kernel, n_side=len(side), n_caches=len(caches), n_cache_blocks=n_cache_blocks),
        grid=grid,
        in_specs=[
            pl.BlockSpec((tm, d), lambda i, j: (i, 0)),
            pl.BlockSpec((1, d), lambda i, j: (0, 0)),
            pl.BlockSpec((None, d, tf), lambda i, j: (0, 0, j)),
            pl.BlockSpec((None, d, tf), lambda i, j: (0, 0, nf + j)),
            pl.BlockSpec((None, tf, d), lambda i, j: (0, j, 0)),
            pl.BlockSpec((1, d), lambda i, j: (0, 0)),
            *side_in,
        ],
        out_specs=[pl.BlockSpec((tm, d), lambda i, j: (i, 0)), *side_out],
        out_shape=[jax.ShapeDtypeStruct((t, d), F32), *side_shapes],
        scratch_shapes=[pltpu.VMEM((tm, d), BF16), pltpu.VMEM((tm, d), F32)],
        compiler_params=_params(("parallel", "arbitrary"), est),
        name="ffn",
    )(x, gpre, w_gu, w_gu, w_down, gpost, *[w_ for w_, _ in side], *caches)


def _row_tile(n, target):
    t = min(n, target)
    while n % t:
        t //= 2
    return t


IN_PROJ_ROWS, IN_PROJ_COLS, IN_PROJ_EPILOGUE_ROWS = 1024, 1024, 256
HGRN_ROWS = 2048
LRU_ROWS = 512
MEM_ATTN_ROWS = 1024
MERGE_ROWS = 256
FFN_ROWS, FFN_COLS = 512, 512
MEM_KV_ROWS, MEM_KV_COLS = 512, 1024


SIDE_HOSTS = {
    "in_proj": ("w_in",),
    "merge": ("w_branch_a", "w_branch_b", "w_branch_c", "w_out", "mem_w_kv"),
    "ffn": ("ffn_w_gu", "ffn_w_down"),
}


def _trunk_layer(x2, bsz, seq, mem, state, lb, w, p, next_w=None, caches=()):
    t, d = x2.shape
    s_hg, h_lru, conv_buf, state_layer = state
    mem_k, mem_v, mem_layer, k_col, v_col = mem
    heads, dk = s_hg.shape[2], s_hg.shape[3]
    d_a = heads * dk
    d_b = h_lru.shape[-1]
    mem_heads, hd = p["mem_heads"], p["mem_hd"]
    if seq < CONV_W - 1:
        raise NotImplementedError("sequence shorter than the conv history")

    def side(host):
        return [] if next_w is None else [(next_w[0][name], next_w[1]) for name in SIDE_HOSTS[host]]

    casts = {}

    def keep(host, outs):
        casts.update(zip(SIDE_HOSTS[host], outs))

    pf, pb, *extra = in_proj(x2, p["norm_pre_mix"], w["w_in"], lb, p["b_gate"], sec=d_a,
                             tm=_row_tile(t, IN_PROJ_ROWS), tn=IN_PROJ_COLS, side=side("in_proj"))
    keep("in_proj", extra)
    n_bf_sections = pb.shape[1] // d_a

    o_a, s_new = hgrn2(pf, pb, p["hgrn_out_norm"], s_hg, state_layer, bsz=bsz, seq=seq, heads=heads, dk=dk,
                       tl=_row_tile(seq, HGRN_ROWS))
    o_b, h_last = conv_lru(pf, pb, p["conv_w"], p["conv_b"], p["lru_wax"], p["lru_ba"], p["lru_bx"],
                           p["lru_lambda"], h_lru, conv_buf, state_layer,
                           bsz=bsz, seq=seq, d=d_b, tl=_row_tile(seq, LRU_ROWS))
    o_c = mem_attn(pb, mem_k, mem_v, mem_layer, bsz=bsz, seq=seq, heads=mem_heads, hd=hd,
                   col0=n_bf_sections - 1, k_col=k_col, v_col=v_col, tl=_row_tile(seq, MEM_ATTN_ROWS))

    gate_col = (IN_SEC_GATES - IN_F32_SECTIONS) * d_a
    assert gate_col % d == 0
    x2, *extra = merge(x2, o_a, o_b, o_c, pb, w["w_branch_a"], w["w_branch_b"], w["w_branch_c"], w["w_out"],
                       p["norm_post_mix"], col_gates=gate_col // d, tm=_row_tile(t, MERGE_ROWS),
                       side=side("merge"))
    keep("merge", extra)
    x2, *extra = ffn(x2, p["norm_pre_ffn"], w["ffn_w_gu"], w["ffn_w_down"], p["norm_post_ffn"],
                     tm=_row_tile(t, FFN_ROWS), tf=FFN_COLS, side=side("ffn"), caches=caches)
    n_casts = len(side("ffn"))
    keep("ffn", extra[:n_casts])
    merged_caches = extra[n_casts:]

    rx_tail = pf.reshape(bsz, seq, -1)[:, seq - (CONV_W - 1):, IN_SEC_RX * d_a:(IN_SEC_RX + 1) * d_a]
    return x2, s_new, h_last.reshape(bsz, d_b), rx_tail, casts, merged_caches


def kernel(x_prompt, x_sample, state_hgrn, state_lru, state_conv, cache_mem_k, cache_mem_v, mem_prompt, norm_mem, mem_w_kv, hgrn_lower_bound, norm_pre_mix, w_in, b_gate, hgrn_out_norm, conv_w, conv_b, lru_wa, lru_ba, lru_wx, lru_bx, lru_lambda, w_branch_a, w_branch_b, w_branch_c, w_out, norm_post_mix, norm_pre_ffn, ffn_w_gu, ffn_w_down, norm_post_ffn):
    depth = w_in.shape[0]
    bp, sp, d = x_prompt.shape
    bs, ss, _ = x_sample.shape
    _, _, heads, dk, dv = state_hgrn.shape
    d_a = heads * dk
    d_b = state_lru.shape[-1]
    n_mem, mem_heads, hd = cache_mem_k.shape[2:]
    d_c = mem_heads * hd
    assert dk == dv and d_b == d_a and d_c == d_a and d == 2 * d_a
    assert b_gate.shape[1] == N_GATES * d

    sm = jax.nn.softmax(hgrn_lower_bound.astype(F32), axis=0)
    lbs = jnp.cumsum(sm, axis=0) - sm[0:1]

    xp = x_prompt.reshape(bp * sp, d)
    xs = x_sample.reshape(bs * ss, d)
    mem2 = mem_prompt.reshape(bp * n_mem, d)
    zero_state = (jnp.zeros((1, bp, heads, dk, dv), F32), jnp.zeros((1, bp, 1, d_b), F32),
                  jnp.zeros((1, bp, CONV_W - 1, d_b), F32), 0)
    lru_s4 = state_lru.reshape(depth, bs, 1, d_b)

    w_f32 = dict(w_in=w_in, w_branch_a=w_branch_a, w_branch_b=w_branch_b, w_branch_c=w_branch_c, w_out=w_out,
                 ffn_w_gu=ffn_w_gu, ffn_w_down=ffn_w_down, mem_w_kv=mem_w_kv)
    assert set(w_f32) == {name for names in SIDE_HOSTS.values() for name in names}
    w = {name: cast_bf16(a, 0) for name, a in w_f32.items()}

    outs = {k: [] for k in ("hg_p", "lru_p", "conv_p", "mk_p", "mv_p", "hg_s", "lru_s", "conv_s")}
    for l in range(depth):
        row = lambda a: a[l].reshape(1, -1)
        p = dict(
            mem_heads=mem_heads, mem_hd=hd,
            norm_pre_mix=row(norm_pre_mix), b_gate=row(b_gate),
            hgrn_out_norm=row(hgrn_out_norm), conv_w=conv_w[l], conv_b=row(conv_b),
            lru_wax=jnp.concatenate([lru_wa[l], lru_wx[l]], axis=-1).astype(BF16),
            lru_ba=row(lru_ba), lru_bx=row(lru_bx), lru_lambda=row(lru_lambda),
            norm_post_mix=row(norm_post_mix), norm_pre_ffn=row(norm_pre_ffn),
            norm_post_ffn=row(norm_post_ffn),
        )
        lb = lbs[l].reshape(1, -1)

        kv = norm_matmul(mem2, row(norm_mem), w["mem_w_kv"], tm=_row_tile(bp * n_mem, MEM_KV_ROWS),
                         tn=MEM_KV_COLS)
        kv4 = kv.reshape(1, bp, n_mem, 2 * d_c)
        next_w = (w_f32, l + 1) if l + 1 < depth else None
        caches = (cache_mem_k, cache_mem_v) if l == 0 else ()
        xp, s1, h1, c1, w_next, merged = _trunk_layer(xp, bp, sp, (kv4, kv4, 0, 0, 1), zero_state, lb, w, p,
                                                      next_w, caches)
        if l == 0:
            cache_k, cache_v = merged
        outs["hg_p"].append(s1); outs["lru_p"].append(h1); outs["conv_p"].append(c1)
        outs["mk_p"].append(kv4[0, :, :, :d_c].reshape(bp, n_mem, mem_heads, hd))
        outs["mv_p"].append(kv4[0, :, :, d_c:].reshape(bp, n_mem, mem_heads, hd))

        xs, s2, h2, c2, _, _ = _trunk_layer(xs, bs, ss, (cache_k, cache_v, l, 0, 0),
                                            (state_hgrn, lru_s4, state_conv, l), lb, w, p)
        outs["hg_s"].append(s2); outs["lru_s"].append(h2); outs["conv_s"].append(c2)
        w = w_next

    st = {k: jnp.stack(v) for k, v in outs.items()}
    return (xp.reshape(bp, sp, d), xs.reshape(bs, ss, d), st["hg_p"], st["lru_p"], st["conv_p"],
            st["mk_p"], st["mv_p"], st["hg_s"], st["lru_s"], st["conv_s"])
```

```python
import functools
import math

import numpy as np
import jax
import jax.numpy as jnp
from jax import lax
from jax.experimental import pallas as pl
from jax.experimental.pallas import tpu as pltpu

F32 = jnp.float32
BF16 = jnp.bfloat16

EPS = 1e-6
LRU_C = 8.0
CHUNK = 64
SUB = 16
HALF = 8
N_PIECES = 3
CONV_W = 4
CONV_PAD = 8
LRU_SEGMENTS = 4
N_GATES = 3

LANES = 128
BF16_SUBLANES = 16

V7X_VMEM_BYTES = 64 * 1024 * 1024
VMEM_LIMIT_CAP = 56 * 1024 * 1024


def _vmem_limit(estimate_bytes):
    return int(min(VMEM_LIMIT_CAP, max(16 * 1024 * 1024, estimate_bytes * 5 // 4)))


def _params(sem, vmem_estimate):
    return pltpu.CompilerParams(dimension_semantics=sem, vmem_limit_bytes=_vmem_limit(vmem_estimate))


def _rms(x, g):
    ms = jnp.mean(x * x, axis=-1, keepdims=True)
    return x * lax.rsqrt(ms + EPS) * g


def _dot(a, b):
    return jnp.dot(a, b, preferred_element_type=F32)


def _dot_nt(a, b):
    return lax.dot_general(a, b, (((1,), (1,)), ((), ())), preferred_element_type=F32)


def _dot_tn(a, b):
    return lax.dot_general(a, b, (((0,), (0,)), ((), ())), preferred_element_type=F32)


def _cast_kernel(w_ref, o_ref):
    o_ref[...] = w_ref[...].astype(o_ref.dtype)


CAST_BLOCK_BYTES = 4 * 1024 * 1024


def cast_bf16(w, layer):
    _, r, c = w.shape
    tr = r
    while tr * c * 4 > CAST_BLOCK_BYTES and tr % 32 == 0:
        tr //= 2
    return pl.pallas_call(
        _cast_kernel,
        grid=(r // tr,),
        in_specs=[pl.BlockSpec((None, tr, c), lambda i: (layer, i, 0))],
        out_specs=pl.BlockSpec((None, tr, c), lambda i: (0, i, 0)),
        out_shape=jax.ShapeDtypeStruct((1, r, c), BF16),
        compiler_params=_params(("parallel",), 2 * tr * c * 6),
        name="cast_bf16",
    )(w)


def _flat_step(grid, ids):
    g = 0
    for n, i in zip(grid, ids):
        g = g * n + i
    return g


def _side_cast_specs(side, grid, flat=False):
    in_specs, out_specs, out_shapes, vmem = [], [], [], 0
    n_i = math.prod(grid) if flat else grid[0]
    n_j = 1 if flat or len(grid) < 2 else grid[1]
    for w, layer in side:
        _, r, c = w.shape
        assert r % n_i == 0 and (r // n_i) % BF16_SUBLANES == 0
        br = r // n_i
        split = n_j > 1 and c % n_j == 0 and (c // n_j) % LANES == 0
        bc = c // n_j if split else c

        def block(ids, split=split):
            return (_flat_step(grid, ids), 0) if flat else (ids[0], ids[1] if split else 0)

        in_specs.append(pl.BlockSpec((None, br, bc), lambda *ids, layer=layer, block=block: (layer, *block(ids))))
        out_specs.append(pl.BlockSpec((None, br, bc), lambda *ids, block=block: (0, *block(ids))))
        out_shapes.append(jax.ShapeDtypeStruct((1, r, c), BF16))
        vmem += 2 * br * bc * 6
    return in_specs, out_specs, out_shapes, vmem


def _run_side_casts(side_in, side_out):
    for w_ref, o_ref in zip(side_in, side_out):
        o_ref[...] = w_ref[...].astype(o_ref.dtype)


HEAD_MERGE_ROWS = 128


def _head_merge_blocks(cache):
    depth, bsz, n_mem = cache.shape[:3]
    return depth * bsz * (n_mem // min(HEAD_MERGE_ROWS, n_mem))


def _head_merge_specs(caches, grid):
    in_specs, out_specs, out_shapes, vmem = [], [], [], 0
    for c in caches:
        depth, bsz, n_mem, heads, hd = c.shape
        rows = min(HEAD_MERGE_ROWS, n_mem)
        assert n_mem % rows == 0
        per_b = n_mem // rows
        n_blocks = _head_merge_blocks(c)
        assert n_blocks <= math.prod(grid)

        def where(ids, bsz=bsz, per_b=per_b, n_blocks=n_blocks):
            g = jnp.minimum(_flat_step(grid, ids), n_blocks - 1)
            return g // (bsz * per_b), (g // per_b) % bsz, g % per_b

        in_specs.append(pl.BlockSpec((None, 1, rows, heads, hd), lambda *ids, where=where: (*where(ids), 0, 0)))
        out_specs.append(pl.BlockSpec((None, 1, rows, heads * hd), lambda *ids, where=where: (*where(ids), 0)))
        out_shapes.append(jax.ShapeDtypeStruct((depth, bsz, n_mem, heads * hd), c.dtype))
        vmem += 2 * rows * (8 * hd + heads * hd) * 4
    return in_specs, out_specs, out_shapes, vmem


def _run_head_merges(ins, outs, n_blocks, n_axes):
    if not ins:
        return
    step = _flat_step([pl.num_programs(a) for a in range(n_axes)], [pl.program_id(a) for a in range(n_axes)])

    @pl.when(step < n_blocks)
    def _():
        for c_ref, o_ref in zip(ins, outs):
            heads, hd = c_ref.shape[2], c_ref.shape[3]
            for h in range(heads):
                o_ref[0, :, h * hd:(h + 1) * hd] = c_ref[0, :, h, :]


def _norm_matmul_kernel(x_ref, g_ref, w_ref, o_ref, h_ref):
    @pl.when(pl.program_id(1) == 0)
    def _():
        h_ref[...] = _rms(x_ref[...], g_ref[...]).astype(BF16)

    o_ref[...] = _dot(h_ref[...], w_ref[...]).astype(o_ref.dtype)


def norm_matmul(x, g, w, *, tm, tn, out_dtype=F32):
    t, d = x.shape
    n = w.shape[2]
    assert t % tm == 0 and n % tn == 0
    est = 2 * tm * d * 4 + 2 * d * tn * 2 + 2 * tm * tn * 4 + tm * d * 2
    return pl.pallas_call(
        _norm_matmul_kernel,
        grid=(t // tm, n // tn),
        in_specs=[
            pl.BlockSpec((tm, d), lambda i, j: (i, 0)),
            pl.BlockSpec((1, d), lambda i, j: (0, 0)),
            pl.BlockSpec((None, d, tn), lambda i, j: (0, 0, j)),
        ],
        out_specs=pl.BlockSpec((tm, tn), lambda i, j: (i, j)),
        out_shape=jax.ShapeDtypeStruct((t, n), out_dtype),
        scratch_shapes=[pltpu.VMEM((tm, d), BF16)],
        compiler_params=_params(("parallel", "arbitrary"), est),
        name="norm_matmul",
    )(x, g, w)


IN_F32_SECTIONS = 2
IN_SEC_LOGF, IN_SEC_RX, IN_SEC_Q, IN_SEC_V, IN_SEC_OG, IN_SEC_RY, IN_SEC_GATES = range(7)


def _in_proj_kernel(perm_ref, x_ref, g_ref, w_ref, lb_ref, bg_ref, *rest, tps, n_gate_secs, sub_rows, n_side):
    del perm_ref
    side_in, (of_ref, ob_ref), side_out, (h_ref,) = (
        rest[:n_side], rest[n_side:n_side + 2], rest[n_side + 2:2 * n_side + 2], rest[2 * n_side + 2:])
    _run_side_casts(side_in, side_out)
    j = pl.program_id(1)
    sec = j // tps

    @pl.when(j == 0)
    def _():
        h_ref[...] = _rms(x_ref[...], g_ref[...]).astype(BF16)

    tm = h_ref.shape[0]

    def run(out_ref, act):
        for r in range(tm // sub_rows):
            rs = slice(r * sub_rows, (r + 1) * sub_rows)
            out_ref[rs, :] = act(_dot(h_ref[rs, :], w_ref[...])).astype(out_ref.dtype)

    def log_forget(a):
        lb = lb_ref[...]
        return jnp.log(lb + (1.0 - lb) * jax.nn.sigmoid(a))

    sec_cq = IN_SEC_GATES + n_gate_secs
    pl.when(sec == IN_SEC_LOGF)(lambda: run(of_ref, log_forget))
    pl.when(sec == IN_SEC_RX)(lambda: run(of_ref, lambda a: a))
    pl.when((sec == IN_SEC_Q) | (sec == IN_SEC_OG))(lambda: run(ob_ref, jax.nn.silu))
    pl.when((sec == IN_SEC_V) | (sec == sec_cq))(lambda: run(ob_ref, lambda a: a))
    pl.when(sec == IN_SEC_RY)(lambda: run(ob_ref, jax.nn.gelu))
    pl.when((sec >= IN_SEC_GATES) & (sec < sec_cq))(
        lambda: run(ob_ref, lambda a: jax.nn.sigmoid(a + bg_ref[...])))


def in_proj(x, g, w, lb, bg, *, sec, tm, tn, side=()):
    t, d = x.shape
    n = w.shape[2]
    assert t % tm == 0 and sec % tn == 0 and n % sec == 0
    tps = sec // tn
    n_sec = n // sec
    n_gate_secs = bg.shape[1] // sec
    assert n_sec == IN_SEC_GATES + n_gate_secs + 1
    perm = jnp.asarray([1, 4, 0, 2, 3, 5] + list(range(7, 7 + n_gate_secs)) + [6], jnp.int32)
    nf = IN_F32_SECTIONS * tps
    sub_rows = min(tm, IN_PROJ_EPILOGUE_ROWS)
    grid = (t // tm, n // tn)
    side_in, side_out, side_shapes, side_vmem = _side_cast_specs(side, grid)
    est = (2 * tm * d * 4 + 2 * d * tn * 2 + 2 * tm * tn * 4 + 2 * tm * tn * 2 + tm * d * 2
           + 6 * sub_rows * tn * 4 + side_vmem)
    kern = functools.partial(_in_proj_kernel, tps=tps, n_gate_secs=n_gate_secs, sub_rows=sub_rows,
                             n_side=len(side))
    grid_spec = pltpu.PrefetchScalarGridSpec(
        num_scalar_prefetch=1,
        grid=grid,
        in_specs=[
            pl.BlockSpec((tm, d), lambda i, j, perm: (i, 0)),
            pl.BlockSpec((1, d), lambda i, j, perm: (0, 0)),
            pl.BlockSpec((None, d, tn), lambda i, j, perm: (0, 0, perm[j // tps] * tps + j % tps)),
            pl.BlockSpec((1, tn), lambda i, j, perm: (0, jnp.minimum(j, tps - 1))),
            pl.BlockSpec((1, tn), lambda i, j, perm: (0, jnp.clip(j - IN_SEC_GATES * tps, 0,
                                                                   n_gate_secs * tps - 1))),
            *side_in,
        ],
        out_specs=[
            pl.BlockSpec((tm, tn), lambda i, j, perm: (i, jnp.minimum(j, nf - 1))),
            pl.BlockSpec((tm, tn), lambda i, j, perm: (i, jnp.maximum(j - nf, 0))),
            *side_out,
        ],
        scratch_shapes=[pltpu.VMEM((tm, d), BF16)],
    )
    return pl.pallas_call(
        kern,
        grid_spec=grid_spec,
        out_shape=[
            jax.ShapeDtypeStruct((t, IN_F32_SECTIONS * sec), F32),
            jax.ShapeDtypeStruct((t, n - IN_F32_SECTIONS * sec), BF16),
            *side_shapes,
        ],
        compiler_params=_params(("parallel", "arbitrary"), est),
        name="in_proj",
    )(perm, x, g, w, lb, bg, *[w_ for w_, _ in side])


def _hgrn_consts(chunk, dk):
    t = np.arange(chunk)[:, None]
    s = np.arange(chunk)[None, :]
    cum = np.concatenate([(s <= t).astype(np.float32)] * N_PIECES, axis=1)
    lane_blk = np.arange(HALF * dk)[:, None] // dk
    sel = (lane_blk == (np.arange(chunk)[None, :] % HALF)).astype(np.float32)
    return jnp.asarray(cum, BF16), jnp.asarray(sel, BF16)


def _hgrn_kernel(q_ref, lf_ref, v_ref, og_ref, gn_ref, s0_ref, cum_ref, sel_ref, *rest,
                 chunk, n_chunks, dk, n_caches, n_cache_blocks):
    cache_in, (o_ref, sfin_ref), cache_out, (st_ref, b_ref, k_ref, u_ref, sb_ref) = (
        rest[:n_caches], rest[n_caches:n_caches + 2], rest[n_caches + 2:2 * n_caches + 2],
        rest[2 * n_caches + 2:])
    _run_head_merges(cache_in, cache_out, n_cache_blocks, n_axes=3)
    l = pl.program_id(2)
    n_sub = chunk // SUB
    width = q_ref.shape[1]
    n_heads = width // dk
    heads = [slice(h * dk, (h + 1) * dk) for h in range(n_heads)]

    @pl.when(l == 0)
    def _():
        for h in range(n_heads):
            st_ref[h] = s0_ref[0, h].T

    q = q_ref[...].astype(F32)
    lf = lf_ref[...]
    kk = 1.0 - jnp.exp(lf)
    vb = v_ref[...]
    k_ref[...] = kk

    p0 = lf.astype(BF16)
    r1 = lf - p0.astype(F32)
    p1 = r1.astype(BF16)
    p2 = (r1 - p1.astype(F32)).astype(BF16)
    cum = cum_ref[...]
    b = jnp.concatenate(
        [_dot(cum, jnp.concatenate([p[c * chunk:(c + 1) * chunk, :] for p in (p0, p1, p2)], axis=0))
         for c in range(n_chunks)], axis=0)
    b_ref[...] = b
    tl = n_chunks * chunk

    def rows_of(ref, group, offset):
        return jnp.concatenate(
            [jnp.broadcast_to(ref[pl.ds(g * group + offset, 1), :], (group, width)) for g in range(tl // group)],
            axis=0)

    b_end = rows_of(b_ref, SUB, SUB - 1)
    b_mid = rows_of(b_ref, SUB, HALF - 1)
    b_last = rows_of(b_ref, chunk, chunk - 1)

    qe = (q * jnp.exp(b)).astype(BF16)
    k_dec = (kk * jnp.exp(b_end - b)).astype(BF16)
    k_end = (kk * jnp.exp(b_last - b)).astype(BF16)
    decay = [jnp.exp(b_ref[pl.ds((c + 1) * chunk - 1, 1), :]) for c in range(n_chunks)]
    q_dec = [(q * jnp.exp(jnp.minimum(b - rows_of(b_ref, chunk, (j + 1) * SUB - 1), 0.0))).astype(BF16)
             for j in range(n_sub - 1)]
    q_mid = (q * jnp.exp(jnp.minimum(b - b_mid, 0.0))).astype(BF16)
    k_mid = (kk * jnp.exp(jnp.minimum(b_mid - b, 0.0))).astype(BF16)

    w = [(q * jnp.exp(jnp.minimum(b - rows_of(b_ref, HALF, u), 0.0)) * rows_of(k_ref, HALF, u)).astype(BF16)
         for u in range(HALF)]
    sel = sel_ref[...]
    diag = [_dot(jnp.concatenate([wu[:, hc] for wu in w], axis=1), sel) for hc in heads]

    row = lax.broadcasted_iota(jnp.int32, (chunk, chunk), 0)
    col = lax.broadcasted_iota(jnp.int32, (chunk, chunk), 1)
    row_blk = row // SUB
    col_blk = col // SUB
    mid_mask = (col_blk == row_blk) & (row % SUB >= HALF) & (col % SUB < HALF)
    diag_mask = (col // HALF == row // HALF) & (col <= row)
    eye = (lax.broadcasted_iota(jnp.int32, (dk, dk), 0)
           == lax.broadcasted_iota(jnp.int32, (dk, dk), 1)).astype(F32).astype(BF16)

    chunks = [slice(c * chunk, (c + 1) * chunk) for c in range(n_chunks)]
    units = [(h, hc, c, rows) for h, hc in enumerate(heads) for c, rows in enumerate(chunks)]
    cross, v_t = [], []
    for h, hc, c, rows in units:
        qd = jnp.concatenate([qj[rows, hc] for qj in q_dec] + [q_mid[rows, hc]], axis=0)
        kd = jnp.concatenate([k_dec[rows, hc], k_mid[rows, hc]], axis=0)
        cross.append(_dot_nt(qd, kd))
        v_t.append(_dot_nt(eye, vb[rows, hc]).astype(BF16))

    intra = []
    for i, (h, hc, c, rows) in enumerate(units):
        m = cross[i]
        attn = jnp.where(mid_mask, m[(n_sub - 1) * chunk:, chunk:], 0.0)
        for j in range(n_sub - 1):
            attn = jnp.where((col_blk == j) & (row_blk > j), m[j * chunk:(j + 1) * chunk, :chunk], attn)
        attn = jnp.where(diag_mask, diag[h][rows, :], attn)
        intra.append(_dot(attn.astype(BF16), vb[rows, hc]))
        u_ref[i] = _dot(v_t[i], k_end[rows, hc])

    st = [st_ref[h] for h in range(n_heads)]
    for i, (h, hc, c, rows) in enumerate(units):
        sb_ref[i] = st[h].astype(BF16)
        st[h] = st[h] * decay[c][:, hc] + u_ref[i]
    for h in range(n_heads):
        st_ref[h] = st[h]

    inter = [_dot_nt(qe[rows, hc], sb_ref[i]) for i, (h, hc, c, rows) in enumerate(units)]
    gn = gn_ref[...]
    for h, hc in enumerate(heads):
        per_chunk = range(h * n_chunks, (h + 1) * n_chunks)
        o = jnp.concatenate([intra[i] + inter[i] for i in per_chunk], axis=0)
        ms = jnp.mean(o * o, axis=-1, keepdims=True)
        o = o * lax.rsqrt(ms + EPS) * gn[:, hc] * og_ref[:, hc].astype(F32)
        o_ref[:, hc] = o.astype(o_ref.dtype)

    @pl.when(l == pl.num_programs(2) - 1)
    def _():
        for h in range(n_heads):
            sfin_ref[0, h] = st[h].T


HGRN_UNITS_PER_STEP = 32


def hgrn2(pf, pb, gn, s0, layer, *, bsz, seq, heads, dk, tl, caches=()):
    chunk = min(CHUNK, seq)
    assert seq % tl == 0 and tl % chunk == 0 and chunk % SUB == 0
    nl = seq // tl
    d_a = heads * dk
    n_chunks = tl // chunk
    hb = max(1, min(heads, HGRN_UNITS_PER_STEP // n_chunks))
    while heads % hb:
        hb -= 1
    n_groups = heads // hb
    width = hb * dk
    n_units = hb * n_chunks

    def sec(k):
        return pl.BlockSpec((tl, width), lambda b, h, l, k=k: (b * nl + l, k * n_groups + h))

    cum, sel = _hgrn_consts(chunk, dk)
    grid = (bsz, n_groups, nl)
    if caches and _head_merge_blocks(caches[0]) > math.prod(grid):
        outs = hgrn2(pf, pb, gn, s0, layer, bsz=bsz, seq=seq, heads=heads, dk=dk, tl=tl)
        return (*outs, *[c.reshape(*c.shape[:3], -1) for c in caches])
    cache_in, cache_out, cache_shapes, cache_vmem = _head_merge_specs(caches, grid)
    est = (2 * tl * width * (4 + 3 * 2) + 2 * tl * width * 2 + 6 * hb * dk * dk * 4 + 2 * tl * width * 4
           + n_units * dk * dk * 6 + 2 * (cum.size + sel.size) * 2 + 24 * tl * width * 4 + cache_vmem)
    kern = functools.partial(_hgrn_kernel, chunk=chunk, n_chunks=n_chunks, dk=dk, n_caches=len(caches),
                             n_cache_blocks=_head_merge_blocks(caches[0]) if caches else 0)
    return pl.pallas_call(
        kern,
        grid=grid,
        in_specs=[
            sec(IN_SEC_Q - IN_F32_SECTIONS), sec(IN_SEC_LOGF),
            sec(IN_SEC_V - IN_F32_SECTIONS), sec(IN_SEC_OG - IN_F32_SECTIONS),
            pl.BlockSpec((1, width), lambda b, h, l: (0, h)),
            pl.BlockSpec((None, 1, hb, dk, dk), lambda b, h, l: (layer, b, h, 0, 0)),
            pl.BlockSpec(cum.shape, lambda b, h, l: (0, 0)),
            pl.BlockSpec(sel.shape, lambda b, h, l: (0, 0)),
            *cache_in,
        ],
        out_specs=[
            pl.BlockSpec((tl, width), lambda b, h, l: (b * nl + l, h)),
            pl.BlockSpec((1, hb, dk, dk), lambda b, h, l: (b, h, 0, 0)),
            *cache_out,
        ],
        out_shape=[
            jax.ShapeDtypeStruct((bsz * seq, d_a), BF16),
            jax.ShapeDtypeStruct((bsz, heads, dk, dk), F32),
            *cache_shapes,
        ],
        scratch_shapes=[
            pltpu.VMEM((hb, dk, dk), F32),
            pltpu.VMEM((tl, width), F32),
            pltpu.VMEM((tl, width), F32),
            pltpu.VMEM((n_units, dk, dk), F32),
            pltpu.VMEM((n_units, dk, dk), BF16),
        ],
        compiler_params=_params(("parallel", "parallel", "arbitrary"), est),
        name="hgrn2",
    )(pb, pf, pb, pb, gn, s0, cum, sel, *caches)


def _lru_kernel(rx_ref, gy_ref, cw_ref, cb_ref, wax_ref, ba_ref, bx_ref, lam_ref, h0_ref, buf_ref,
                *rest, tl, n_blocks, bw, n_side):
    side_in, (o_ref, hlast_ref), side_out, (xp_ref, a_ref, u_ref, hs_ref, ps_ref, h_ref) = (
        rest[:n_side], rest[n_side:n_side + 2], rest[n_side + 2:2 * n_side + 2], rest[2 * n_side + 2:])
    _run_side_casts(side_in, side_out)
    l = pl.program_id(1)
    keep = CONV_W - 1

    @pl.when(l == 0)
    def _():
        xp_ref[CONV_PAD - keep:CONV_PAD, :] = buf_ref[0]
        h_ref[...] = h0_ref[0]

    x = rx_ref[...]
    xp_ref[CONV_PAD:CONV_PAD + tl, :] = x
    cw = cw_ref[...]
    xc = xp_ref[CONV_PAD - keep:CONV_PAD - keep + tl, :] * cw[0:1, :]
    for j in range(1, CONV_W):
        xc = xc + xp_ref[CONV_PAD - keep + j:CONV_PAD - keep + j + tl, :] * cw[j:j + 1, :]
    xc = xc + cb_ref[...]
    xp_ref[CONV_PAD - keep:CONV_PAD, :] = xp_ref[CONV_PAD + tl - keep:CONV_PAD + tl, :]

    xcb = xc.astype(BF16)
    pre = [_dot(xcb[:, n * bw:(n + 1) * bw], wax_ref[n]) for n in range(n_blocks)]
    r = jax.nn.sigmoid(jnp.concatenate([pn[:, :bw] for pn in pre], axis=-1) + ba_ref[...])
    ig = jax.nn.sigmoid(jnp.concatenate([pn[:, bw:] for pn in pre], axis=-1) + bx_ref[...])
    lam = lam_ref[...]
    softplus_neg = jnp.maximum(-lam, 0.0) + jnp.log1p(jnp.exp(-jnp.abs(lam)))
    log_a = -LRU_C * r * softplus_neg
    a = jnp.exp(log_a)
    mult = jnp.sqrt(jnp.maximum(-jnp.tanh(log_a) * (a * a + 1.0), 0.0))
    a_ref[...] = a
    u_ref[...] = mult * (ig * xc)

    seg = tl // LRU_SEGMENTS
    one = jnp.ones_like(h_ref[...])

    def step(t, carry):
        hs, ps = carry
        new_h, new_p = [], []
        for s in range(LRU_SEGMENTS):
            r = s * seg + t
            a_t = a_ref[pl.ds(r, 1), :]
            h_s = a_t * hs[s] + u_ref[pl.ds(r, 1), :]
            hs_ref[pl.ds(r, 1), :] = h_s
            new_h.append(h_s)
            if s > 0:
                p_s = a_t * ps[s - 1]
                ps_ref[pl.ds(r, 1), :] = p_s
                new_p.append(p_s)
        return tuple(new_h), tuple(new_p)

    init = ((h_ref[...],) + (jnp.zeros_like(one),) * (LRU_SEGMENTS - 1), (one,) * (LRU_SEGMENTS - 1))
    hs, ps = lax.fori_loop(0, seg, step, init, unroll=min(seg, 4))
    h = hs[0]
    gy = gy_ref[...].astype(F32)
    o_ref[0:seg, :] = (hs_ref[0:seg, :] * gy[0:seg, :]).astype(o_ref.dtype)
    for s in range(1, LRU_SEGMENTS):
        rows = slice(s * seg, (s + 1) * seg)
        o_ref[rows, :] = ((hs_ref[rows, :] + ps_ref[rows, :] * h) * gy[rows, :]).astype(o_ref.dtype)
        h = hs[s] + ps[s - 1] * h
    h_ref[...] = h

    @pl.when(l == pl.num_programs(1) - 1)
    def _():
        hlast_ref[0] = h


def conv_lru(pf, pb, cw, cb, wax, ba, bx, lam, h0, buf, layer, *, bsz, seq, d, tl, side=()):
    assert seq % tl == 0 and tl >= CONV_W - 1 and tl % LRU_SEGMENTS == 0
    nl = seq // tl
    n_blocks, bw = wax.shape[0], wax.shape[1]
    vec = pl.BlockSpec((1, d), lambda b, l: (0, 0))
    side_in, side_out, side_shapes, side_vmem = _side_cast_specs(side, (bsz, nl), flat=True)
    est = (2 * tl * d * (4 + 2) + 2 * tl * d * 2 + (3 * tl + CONV_PAD) * d * 4 + 8 * tl * d * 4
           + side_vmem)
    kern = functools.partial(_lru_kernel, tl=tl, n_blocks=n_blocks, bw=bw, n_side=len(side))
    return pl.pallas_call(
        kern,
        grid=(bsz, nl),
        in_specs=[
            pl.BlockSpec((tl, d), lambda b, l: (b * nl + l, IN_SEC_RX)),
            pl.BlockSpec((tl, d), lambda b, l: (b * nl + l, IN_SEC_RY - IN_F32_SECTIONS)),
            pl.BlockSpec((CONV_W, d), lambda b, l: (0, 0)),
            vec,
            pl.BlockSpec((n_blocks, bw, 2 * bw), lambda b, l: (0, 0, 0)),
            vec, vec, vec,
            pl.BlockSpec((None, 1, 1, d), lambda b, l: (layer, b, 0, 0)),
            pl.BlockSpec((None, 1, CONV_W - 1, d), lambda b, l: (layer, b, 0, 0)),
            *side_in,
        ],
        out_specs=[
            pl.BlockSpec((tl, d), lambda b, l: (b * nl + l, 0)),
            pl.BlockSpec((1, 1, d), lambda b, l: (b, 0, 0)),
            *side_out,
        ],
        out_shape=[
            jax.ShapeDtypeStruct((bsz * seq, d), BF16),
            jax.ShapeDtypeStruct((bsz, 1, d), F32),
            *side_shapes,
        ],
        scratch_shapes=[
            pltpu.VMEM((CONV_PAD + tl, d), F32),
            pltpu.VMEM((tl, d), F32),
            pltpu.VMEM((tl, d), F32),
            pltpu.VMEM((tl, d), F32),
            pltpu.VMEM((tl, d), F32),
            pltpu.VMEM((1, d), F32),
        ],
        compiler_params=_params(("parallel", "arbitrary"), est),
        name="conv_lru",
    )(pf, pb, cw, cb, wax, ba, bx, lam, h0, buf, *[w_ for w_, _ in side])


def _mem_attn_kernel(q_ref, k_ref, v_ref, o_ref, *, scale, heads, hd):
    cols = [slice(h * hd, (h + 1) * hd) for h in range(heads)]
    scores = [_dot_nt(q_ref[:, c], k_ref[0, :, c].astype(BF16)) * scale for c in cols]
    probs = [jnp.exp(s - jnp.max(s, axis=-1, keepdims=True)) for s in scores]
    outs = [_dot(p.astype(BF16), v_ref[0, :, c].astype(BF16)) for p, c in zip(probs, cols)]
    for p, o, c in zip(probs, outs, cols):
        o_ref[:, c] = (o / jnp.sum(p, axis=-1, keepdims=True)).astype(o_ref.dtype)


def mem_attn(pb, mem_k, mem_v, layer, *, bsz, seq, heads, hd, col0, k_col, v_col, tl):
    assert seq % tl == 0
    nl = seq // tl
    n_mem = mem_k.shape[2]
    d_c = heads * hd
    est = 4 * tl * d_c * 2 + 4 * n_mem * d_c * 4 + 6 * tl * n_mem * 4
    kern = functools.partial(_mem_attn_kernel, scale=1.0 / math.sqrt(hd), heads=heads, hd=hd)
    return pl.pallas_call(
        kern,
        grid=(bsz, nl),
        in_specs=[
            pl.BlockSpec((tl, d_c), lambda b, l: (b * nl + l, col0)),
            pl.BlockSpec((None, 1, n_mem, d_c), lambda b, l: (layer, b, 0, k_col)),
            pl.BlockSpec((None, 1, n_mem, d_c), lambda b, l: (layer, b, 0, v_col)),
        ],
        out_specs=pl.BlockSpec((tl, d_c), lambda b, l: (b * nl + l, 0)),
        out_shape=jax.ShapeDtypeStruct((bsz * seq, d_c), BF16),
        compiler_params=_params(("parallel", "parallel"), est),
        name="mem_attn",
    )(pb, mem_k, mem_v)


def _merge_kernel(x_ref, oa_ref, ob_ref, oc_ref, g0_ref, g1_ref, g2_ref,
                  wa_ref, wb_ref, wc_ref, wo_ref, gn_ref, *rest, n_side):
    side_in, y_ref, side_out = rest[:n_side], rest[n_side], rest[n_side + 1:]
    _run_side_casts(side_in, side_out)
    m = g0_ref[...].astype(F32) * _dot(oa_ref[...], wa_ref[...])
    m = m + g1_ref[...].astype(F32) * _dot(ob_ref[...], wb_ref[...])
    m = m + g2_ref[...].astype(F32) * _dot(oc_ref[...], wc_ref[...])
    z = _dot(m.astype(BF16), wo_ref[...])
    y_ref[...] = x_ref[...] + _rms(z, gn_ref[...])


def merge(x, oa, ob, oc, pb, wa, wb, wc, wo, gn, *, col_gates, tm, side=()):
    t, d = x.shape
    db = oa.shape[1]
    assert t % tm == 0
    row = lambda i: (i, 0)
    const = lambda i: (0, 0)
    wspec = lambda rows: pl.BlockSpec((None, rows, d), lambda i: (0, 0, 0), pipeline_mode=pl.Buffered(1))
    gate_specs = [pl.BlockSpec((tm, d), lambda i, k=k: (i, col_gates + k)) for k in range(N_GATES)]
    side_in, side_out, side_shapes, side_vmem = _side_cast_specs(side, (t // tm,))
    est = (4 * tm * d * 4 + 6 * tm * db * 2 + 6 * tm * d * 2
           + 3 * db * d * 2 + d * d * 2 + 6 * tm * d * 4 + side_vmem)
    return pl.pallas_call(
        functools.partial(_merge_kernel, n_side=len(side)),
        grid=(t // tm,),
        in_specs=[
            pl.BlockSpec((tm, d), row),
            pl.BlockSpec((tm, db), row), pl.BlockSpec((tm, db), row), pl.BlockSpec((tm, db), row),
            *gate_specs,
            wspec(db), wspec(db), wspec(db), wspec(d),
            pl.BlockSpec((1, d), const),
            *side_in,
        ],
        out_specs=[pl.BlockSpec((tm, d), row), *side_out],
        out_shape=[jax.ShapeDtypeStruct((t, d), F32), *side_shapes],
        compiler_params=_params(("parallel",), est),
        name="merge",
    )(x, oa, ob, oc, pb, pb, pb, wa, wb, wc, wo, gn, *[w_ for w_, _ in side])


def _ffn_kernel(x_ref, gpre_ref, wg_ref, wu_ref, wd_ref, gpost_ref, y_ref, h_ref, acc_ref):
    j = pl.program_id(1)

    @pl.when(j == 0)
    def _():
        h_ref[...] = _rms(x_ref[...], gpre_ref[...]).astype(BF16)
        acc_ref[...] = jnp.zeros_like(acc_ref)

    h = h_ref[...]
    gt = _dot(h, wg_ref[...])
    up = _dot(h, wu_ref[...])
    act = (jax.nn.silu(gt) * up).astype(BF16)
    acc_ref[...] += _dot(act, wd_ref[...])

    @pl.when(j == pl.num_programs(1) - 1)
    def _():
        y_ref[...] = x_ref[...] + _rms(acc_ref[...], gpost_ref[...])


def ffn(x, gpre, w_gu, w_down, gpost, *, tm, tf):
    t, d = x.shape
    d_ff = w_down.shape[1]
    assert t % tm == 0 and d_ff % tf == 0
    nf = d_ff // tf
    est = 4 * tm * d * 4 + 2 * 3 * d * tf * 2 + tm * d * 2 + tm * d * 4 + 4 * tm * tf * 4
    return pl.pallas_call(
        _ffn_kernel,
        grid=(t // tm, nf),
        in_specs=[
            pl.BlockSpec((tm, d), lambda i, j: (i, 0)),
            pl.BlockSpec((1, d), lambda i, j: (0, 0)),
            pl.BlockSpec((None, d, tf), lambda i, j: (0, 0, j)),
            pl.BlockSpec((None, d, tf), lambda i, j: (0, 0, nf + j)),
            pl.BlockSpec((None, tf, d), lambda i, j: (0, j, 0)),
            pl.BlockSpec((1, d), lambda i, j: (0, 0)),
        ],
        out_specs=pl.BlockSpec((tm, d), lambda i, j: (i, 0)),
        out_shape=jax.ShapeDtypeStruct((t, d), F32),
        scratch_shapes=[pltpu.VMEM((tm, d), BF16), pltpu.VMEM((tm, d), F32)],
        compiler_params=_params(("parallel", "arbitrary"), est),
        name="ffn",
    )(x, gpre, w_gu, w_gu, w_down, gpost)


def _row_tile(n, target):
    t = min(n, target)
    while n % t:
        t //= 2
    return t


IN_PROJ_ROWS, IN_PROJ_COLS, IN_PROJ_EPILOGUE_ROWS = 1024, 1024, 256
HGRN_ROWS = 2048
LRU_ROWS = 512
MEM_ATTN_ROWS = 1024
MERGE_ROWS = 256
FFN_ROWS, FFN_COLS = 512, 512
MEM_KV_ROWS, MEM_KV_COLS = 512, 1024


SIDE_HOSTS = {
    "in_proj": ("w_in",),
    "merge": ("w_branch_a", "w_branch_b", "w_branch_c", "w_out", "mem_w_kv"),
    "conv_lru": ("ffn_w_gu", "ffn_w_down"),
}


def _trunk_layer(x2, bsz, seq, mem, state, lb, w, p, next_w=None, caches=()):
    t, d = x2.shape
    s_hg, h_lru, conv_buf, state_layer = state
    mem_k, mem_v, mem_layer, k_col, v_col = mem
    heads, dk = s_hg.shape[2], s_hg.shape[3]
    d_a = heads * dk
    d_b = h_lru.shape[-1]
    mem_heads, hd = p["mem_heads"], p["mem_hd"]
    if seq < CONV_W - 1:
        raise NotImplementedError("sequence shorter than the conv history")

    def side(host):
        return [] if next_w is None else [(next_w[0][name], next_w[1]) for name in SIDE_HOSTS[host]]

    casts = {}

    def keep(host, outs):
        casts.update(zip(SIDE_HOSTS[host], outs))

    pf, pb, *extra = in_proj(x2, p["norm_pre_mix"], w["w_in"], lb, p["b_gate"], sec=d_a,
                             tm=_row_tile(t, IN_PROJ_ROWS), tn=IN_PROJ_COLS, side=side("in_proj"))
    keep("in_proj", extra)
    n_bf_sections = pb.shape[1] // d_a

    o_a, s_new, *merged_caches = hgrn2(pf, pb, p["hgrn_out_norm"], s_hg, state_layer, bsz=bsz, seq=seq,
                                       heads=heads, dk=dk, tl=_row_tile(seq, HGRN_ROWS), caches=caches)
    o_b, h_last, *extra = conv_lru(pf, pb, p["conv_w"], p["conv_b"], p["lru_wax"], p["lru_ba"], p["lru_bx"],
                                   p["lru_lambda"], h_lru, conv_buf, state_layer, bsz=bsz, seq=seq, d=d_b,
                                   tl=_row_tile(seq, LRU_ROWS), side=side("conv_lru"))
    keep("conv_lru", extra)
    o_c = mem_attn(pb, mem_k, mem_v, mem_layer, bsz=bsz, seq=seq, heads=mem_heads, hd=hd,
                   col0=n_bf_sections - 1, k_col=k_col, v_col=v_col, tl=_row_tile(seq, MEM_ATTN_ROWS))

    gate_col = (IN_SEC_GATES - IN_F32_SECTIONS) * d_a
    assert gate_col % d == 0
    x2, *extra = merge(x2, o_a, o_b, o_c, pb, w["w_branch_a"], w["w_branch_b"], w["w_branch_c"], w["w_out"],
                       p["norm_post_mix"], col_gates=gate_col // d, tm=_row_tile(t, MERGE_ROWS),
                       side=side("merge"))
    keep("merge", extra)
    x2 = ffn(x2, p["norm_pre_ffn"], w["ffn_w_gu"], w["ffn_w_down"], p["norm_post_ffn"],
             tm=_row_tile(t, FFN_ROWS), tf=FFN_COLS)

    rx_tail = pf.reshape(bsz, seq, -1)[:, seq - (CONV_W - 1):, IN_SEC_RX * d_a:(IN_SEC_RX + 1) * d_a]
    return x2, s_new, h_last.reshape(bsz, d_b), rx_tail, casts, merged_caches


def kernel(x_prompt, x_sample, state_hgrn, state_lru, state_conv, cache_mem_k, cache_mem_v, mem_prompt, norm_mem, mem_w_kv, hgrn_lower_bound, norm_pre_mix, w_in, b_gate, hgrn_out_norm, conv_w, conv_b, lru_wa, lru_ba, lru_wx, lru_bx, lru_lambda, w_branch_a, w_branch_b, w_branch_c, w_out, norm_post_mix, norm_pre_ffn, ffn_w_gu, ffn_w_down, norm_post_ffn):
    depth = w_in.shape[0]
    bp, sp, d = x_prompt.shape
    bs, ss, _ = x_sample.shape
    _, _, heads, dk, dv = state_hgrn.shape
    d_a = heads * dk
    d_b = state_lru.shape[-1]
    n_mem, mem_heads, hd = cache_mem_k.shape[2:]
    d_c = mem_heads * hd
    assert dk == dv and d_b == d_a and d_c == d_a and d == 2 * d_a
    assert b_gate.shape[1] == N_GATES * d

    sm = jax.nn.softmax(hgrn_lower_bound.astype(F32), axis=0)
    lbs = jnp.cumsum(sm, axis=0) - sm[0:1]

    xp = x_prompt.reshape(bp * sp, d)
    xs = x_sample.reshape(bs * ss, d)
    mem2 = mem_prompt.reshape(bp * n_mem, d)
    zero_state = (jnp.zeros((1, bp, heads, dk, dv), F32), jnp.zeros((1, bp, 1, d_b), F32),
                  jnp.zeros((1, bp, CONV_W - 1, d_b), F32), 0)
    lru_s4 = state_lru.reshape(depth, bs, 1, d_b)

    w_f32 = dict(w_in=w_in, w_branch_a=w_branch_a, w_branch_b=w_branch_b, w_branch_c=w_branch_c, w_out=w_out,
                 ffn_w_gu=ffn_w_gu, ffn_w_down=ffn_w_down, mem_w_kv=mem_w_kv)
    assert set(w_f32) == {name for names in SIDE_HOSTS.values() for name in names}
    w = {name: cast_bf16(a, 0) for name, a in w_f32.items()}

    outs = {k: [] for k in ("hg_p", "lru_p", "conv_p", "mk_p", "mv_p", "hg_s", "lru_s", "conv_s")}
    for l in range(depth):
        row = lambda a: a[l].reshape(1, -1)
        p = dict(
            mem_heads=mem_heads, mem_hd=hd,
            norm_pre_mix=row(norm_pre_mix), b_gate=row(b_gate),
            hgrn_out_norm=row(hgrn_out_norm), conv_w=conv_w[l], conv_b=row(conv_b),
            lru_wax=jnp.concatenate([lru_wa[l], lru_wx[l]], axis=-1).astype(BF16),
            lru_ba=row(lru_ba), lru_bx=row(lru_bx), lru_lambda=row(lru_lambda),
            norm_post_mix=row(norm_post_mix), norm_pre_ffn=row(norm_pre_ffn),
            norm_post_ffn=row(norm_post_ffn),
        )
        lb = lbs[l].reshape(1, -1)

        kv = norm_matmul(mem2, row(norm_mem), w["mem_w_kv"], tm=_row_tile(bp * n_mem, MEM_KV_ROWS),
                         tn=MEM_KV_COLS)
        kv4 = kv.reshape(1, bp, n_mem, 2 * d_c)
        next_w = (w_f32, l + 1) if l + 1 < depth else None
        caches = (cache_mem_k, cache_mem_v) if l == 0 else ()
        xp, s1, h1, c1, w_next, merged = _trunk_layer(xp, bp, sp, (kv4, kv4, 0, 0, 1), zero_state, lb, w, p,
                                                      next_w, caches)
        if l == 0:
            cache_k, cache_v = merged
        outs["hg_p"].append(s1); outs["lru_p"].append(h1); outs["conv_p"].append(c1)
        outs["mk_p"].append(kv4[0, :, :, :d_c].reshape(bp, n_mem, mem_heads, hd))
        outs["mv_p"].append(kv4[0, :, :, d_c:].reshape(bp, n_mem, mem_heads, hd))

        xs, s2, h2, c2, _, _ = _trunk_layer(xs, bs, ss, (cache_k, cache_v, l, 0, 0),
                                            (state_hgrn, lru_s4, state_conv, l), lb, w, p)
        outs["hg_s"].append(s2); outs["lru_s"].append(h2); outs["conv_s"].append(c2)
        w = w_next

    st = {k: jnp.stack(v) for k, v in outs.items()}
    return (xp.reshape(bp, sp, d), xs.reshape(bs, ss, d), st["hg_p"], st["lru_p"], st["conv_p"],
            st["mk_p"], st["mv_p"], st["hg_s"], st["lru_s"], st["conv_s"])
```

```python
import functools
import math

import numpy as np
import jax
import jax.numpy as jnp
from jax import lax
from jax.experimental import pallas as pl
from jax.experimental.pallas import tpu as pltpu

F32 = jnp.float32
BF16 = jnp.bfloat16

EPS = 1e-6
LRU_C = 8.0
CHUNK = 64
SUB = 16
HALF = 8
N_PIECES = 3
CONV_W = 4
CONV_PAD = 8
LRU_SEGMENTS = 4
N_GATES = 3

LANES = 128
BF16_SUBLANES = 16

V7X_VMEM_BYTES = 64 * 1024 * 1024
VMEM_LIMIT_CAP = 56 * 1024 * 1024


def _vmem_limit(estimate_bytes):
    return int(min(VMEM_LIMIT_CAP, max(16 * 1024 * 1024, estimate_bytes * 5 // 4)))


def _params(sem, vmem_estimate):
    return pltpu.CompilerParams(dimension_semantics=sem, vmem_limit_bytes=_vmem_limit(vmem_estimate))


def _rms(x, g):
    ms = jnp.mean(x * x, axis=-1, keepdims=True)
    return x * lax.rsqrt(ms + EPS) * g


def _dot(a, b):
    return jnp.dot(a, b, preferred_element_type=F32)


def _dot_nt(a, b):
    return lax.dot_general(a, b, (((1,), (1,)), ((), ())), preferred_element_type=F32)


def _dot_tn(a, b):
    return lax.dot_general(a, b, (((0,), (0,)), ((), ())), preferred_element_type=F32)


def _cast_kernel(w_ref, o_ref):
    o_ref[...] = w_ref[...].astype(o_ref.dtype)


CAST_BLOCK_BYTES = 4 * 1024 * 1024


def cast_bf16(w, layer):
    _, r, c = w.shape
    tr = r
    while tr * c * 4 > CAST_BLOCK_BYTES and tr % 32 == 0:
        tr //= 2
    return pl.pallas_call(
        _cast_kernel,
        grid=(r // tr,),
        in_specs=[pl.BlockSpec((None, tr, c), lambda i: (layer, i, 0))],
        out_specs=pl.BlockSpec((None, tr, c), lambda i: (0, i, 0)),
        out_shape=jax.ShapeDtypeStruct((1, r, c), BF16),
        compiler_params=_params(("parallel",), 2 * tr * c * 6),
        name="cast_bf16",
    )(w)


def _flat_step(grid, ids):
    g = 0
    for n, i in zip(grid, ids):
        g = g * n + i
    return g


def _side_cast_specs(side, grid, flat=False):
    in_specs, out_specs, out_shapes, vmem = [], [], [], 0
    n_i = math.prod(grid) if flat else grid[0]
    n_j = 1 if flat or len(grid) < 2 else grid[1]
    for w, layer in side:
        _, r, c = w.shape
        assert r % n_i == 0 and (r // n_i) % BF16_SUBLANES == 0
        br = r // n_i
        split = n_j > 1 and c % n_j == 0 and (c // n_j) % LANES == 0
        bc = c // n_j if split else c

        def block(ids, split=split):
            return (_flat_step(grid, ids), 0) if flat else (ids[0], ids[1] if split else 0)

        in_specs.append(pl.BlockSpec((None, br, bc), lambda *ids, layer=layer, block=block: (layer, *block(ids))))
        out_specs.append(pl.BlockSpec((None, br, bc), lambda *ids, block=block: (0, *block(ids))))
        out_shapes.append(jax.ShapeDtypeStruct((1, r, c), BF16))
        vmem += 2 * br * bc * 6
    return in_specs, out_specs, out_shapes, vmem


def _run_side_casts(side_in, side_out):
    for w_ref, o_ref in zip(side_in, side_out):
        o_ref[...] = w_ref[...].astype(o_ref.dtype)


HEAD_MERGE_ROWS = 128


def _head_merge_blocks(cache):
    depth, bsz, n_mem = cache.shape[:3]
    return depth * bsz * (n_mem // min(HEAD_MERGE_ROWS, n_mem))


def _head_merge_specs(caches, grid):
    in_specs, out_specs, out_shapes, vmem = [], [], [], 0
    for c in caches:
        depth, bsz, n_mem, heads, hd = c.shape
        rows = min(HEAD_MERGE_ROWS, n_mem)
        assert n_mem % rows == 0
        per_b = n_mem // rows
        n_blocks = _head_merge_blocks(c)
        assert n_blocks <= math.prod(grid)

        def where(ids, bsz=bsz, per_b=per_b, n_blocks=n_blocks):
            g = jnp.minimum(_flat_step(grid, ids), n_blocks - 1)
            return g // (bsz * per_b), (g // per_b) % bsz, g % per_b

        in_specs.append(pl.BlockSpec((None, 1, rows, heads, hd), lambda *ids, where=where: (*where(ids), 0, 0)))
        out_specs.append(pl.BlockSpec((None, 1, rows, heads * hd), lambda *ids, where=where: (*where(ids), 0)))
        out_shapes.append(jax.ShapeDtypeStruct((depth, bsz, n_mem, heads * hd), c.dtype))
        vmem += 2 * rows * (8 * hd + heads * hd) * 4
    return in_specs, out_specs, out_shapes, vmem


def _run_head_merges(ins, outs, n_blocks, n_axes):
    if not ins:
        return
    step = _flat_step([pl.num_programs(a) for a in range(n_axes)], [pl.program_id(a) for a in range(n_axes)])

    @pl.when(step < n_blocks)
    def _():
        for c_ref, o_ref in zip(ins, outs):
            heads, hd = c_ref.shape[2], c_ref.shape[3]
            for h in range(heads):
                o_ref[0, :, h * hd:(h + 1) * hd] = c_ref[0, :, h, :]


def _norm_matmul_kernel(x_ref, g_ref, w_ref, o_ref, h_ref):
    @pl.when(pl.program_id(1) == 0)
    def _():
        h_ref[...] = _rms(x_ref[...], g_ref[...]).astype(BF16)

    o_ref[...] = _dot(h_ref[...], w_ref[...]).astype(o_ref.dtype)


def norm_matmul(x, g, w, *, tm, tn, out_dtype=F32):
    t, d = x.shape
    n = w.shape[2]
    assert t % tm == 0 and n % tn == 0
    est = 2 * tm * d * 4 + 2 * d * tn * 2 + 2 * tm * tn * 4 + tm * d * 2
    return pl.pallas_call(
        _norm_matmul_kernel,
        grid=(t // tm, n // tn),
        in_specs=[
            pl.BlockSpec((tm, d), lambda i, j: (i, 0)),
            pl.BlockSpec((1, d), lambda i, j: (0, 0)),
            pl.BlockSpec((None, d, tn), lambda i, j: (0, 0, j)),
        ],
        out_specs=pl.BlockSpec((tm, tn), lambda i, j: (i, j)),
        out_shape=jax.ShapeDtypeStruct((t, n), out_dtype),
        scratch_shapes=[pltpu.VMEM((tm, d), BF16)],
        compiler_params=_params(("parallel", "arbitrary"), est),
        name="norm_matmul",
    )(x, g, w)


IN_F32_SECTIONS = 2
IN_SEC_LOGF, IN_SEC_RX, IN_SEC_Q, IN_SEC_V, IN_SEC_OG, IN_SEC_RY, IN_SEC_GATES = range(7)


def _in_proj_kernel(perm_ref, x_ref, g_ref, w_ref, lb_ref, bg_ref, *rest, tps, n_gate_secs, sub_rows, n_side):
    del perm_ref
    side_in, (of_ref, ob_ref), side_out, (h_ref,) = (
        rest[:n_side], rest[n_side:n_side + 2], rest[n_side + 2:2 * n_side + 2], rest[2 * n_side + 2:])
    _run_side_casts(side_in, side_out)
    j = pl.program_id(1)
    sec = j // tps

    @pl.when(j == 0)
    def _():
        h_ref[...] = _rms(x_ref[...], g_ref[...]).astype(BF16)

    tm = h_ref.shape[0]

    def run(out_ref, act):
        for r in range(tm // sub_rows):
            rs = slice(r * sub_rows, (r + 1) * sub_rows)
            out_ref[rs, :] = act(_dot(h_ref[rs, :], w_ref[...])).astype(out_ref.dtype)

    def log_forget(a):
        lb = lb_ref[...]
        return jnp.log(lb + (1.0 - lb) * jax.nn.sigmoid(a))

    sec_cq = IN_SEC_GATES + n_gate_secs
    pl.when(sec == IN_SEC_LOGF)(lambda: run(of_ref, log_forget))
    pl.when(sec == IN_SEC_RX)(lambda: run(of_ref, lambda a: a))
    pl.when((sec == IN_SEC_Q) | (sec == IN_SEC_OG))(lambda: run(ob_ref, jax.nn.silu))
    pl.when((sec == IN_SEC_V) | (sec == sec_cq))(lambda: run(ob_ref, lambda a: a))
    pl.when(sec == IN_SEC_RY)(lambda: run(ob_ref, jax.nn.gelu))
    pl.when((sec >= IN_SEC_GATES) & (sec < sec_cq))(
        lambda: run(ob_ref, lambda a: jax.nn.sigmoid(a + bg_ref[...])))


def in_proj(x, g, w, lb, bg, *, sec, tm, tn, side=()):
    t, d = x.shape
    n = w.shape[2]
    assert t % tm == 0 and sec % tn == 0 and n % sec == 0
    tps = sec // tn
    n_sec = n // sec
    n_gate_secs = bg.shape[1] // sec
    assert n_sec == IN_SEC_GATES + n_gate_secs + 1
    perm = jnp.asarray([1, 4, 0, 2, 3, 5] + list(range(7, 7 + n_gate_secs)) + [6], jnp.int32)
    nf = IN_F32_SECTIONS * tps
    sub_rows = min(tm, IN_PROJ_EPILOGUE_ROWS)
    grid = (t // tm, n // tn)
    side_in, side_out, side_shapes, side_vmem = _side_cast_specs(side, grid)
    est = (2 * tm * d * 4 + 2 * d * tn * 2 + 2 * tm * tn * 4 + 2 * tm * tn * 2 + tm * d * 2
           + 6 * sub_rows * tn * 4 + side_vmem)
    kern = functools.partial(_in_proj_kernel, tps=tps, n_gate_secs=n_gate_secs, sub_rows=sub_rows,
                             n_side=len(side))
    grid_spec = pltpu.PrefetchScalarGridSpec(
        num_scalar_prefetch=1,
        grid=grid,
        in_specs=[
            pl.BlockSpec((tm, d), lambda i, j, perm: (i, 0)),
            pl.BlockSpec((1, d), lambda i, j, perm: (0, 0)),
            pl.BlockSpec((None, d, tn), lambda i, j, perm: (0, 0, perm[j // tps] * tps + j % tps)),
            pl.BlockSpec((1, tn), lambda i, j, perm: (0, jnp.minimum(j, tps - 1))),
            pl.BlockSpec((1, tn), lambda i, j, perm: (0, jnp.clip(j - IN_SEC_GATES * tps, 0,
                                                                   n_gate_secs * tps - 1))),
            *side_in,
        ],
        out_specs=[
            pl.BlockSpec((tm, tn), lambda i, j, perm: (i, jnp.minimum(j, nf - 1))),
            pl.BlockSpec((tm, tn), lambda i, j, perm: (i, jnp.maximum(j - nf, 0))),
            *side_out,
        ],
        scratch_shapes=[pltpu.VMEM((tm, d), BF16)],
    )
    return pl.pallas_call(
        kern,
        grid_spec=grid_spec,
        out_shape=[
            jax.ShapeDtypeStruct((t, IN_F32_SECTIONS * sec), F32),
            jax.ShapeDtypeStruct((t, n - IN_F32_SECTIONS * sec), BF16),
            *side_shapes,
        ],
        compiler_params=_params(("parallel", "arbitrary"), est),
        name="in_proj",
    )(perm, x, g, w, lb, bg, *[w_ for w_, _ in side])


def _hgrn_consts(chunk, dk):
    t = np.arange(chunk)[:, None]
    s = np.arange(chunk)[None, :]
    cum = np.concatenate([(s <= t).astype(np.float32)] * N_PIECES, axis=1)
    lane_blk = np.arange(HALF * dk)[:, None] // dk
    sel = (lane_blk == (np.arange(chunk)[None, :] % HALF)).astype(np.float32)
    return jnp.asarray(cum, BF16), jnp.asarray(sel, BF16)


def _hgrn_kernel(q_ref, lf_ref, v_ref, og_ref, gn_ref, s0_ref, cum_ref, sel_ref, *rest,
                 chunk, n_chunks, dk, n_caches, n_cache_blocks, n_side):
    n_extra = n_caches + n_side
    extra_in, (o_ref, sfin_ref), extra_out, (st_ref, b_ref, k_ref, u_ref, sb_ref) = (
        rest[:n_extra], rest[n_extra:n_extra + 2], rest[n_extra + 2:2 * n_extra + 2], rest[2 * n_extra + 2:])
    _run_head_merges(extra_in[:n_caches], extra_out[:n_caches], n_cache_blocks, n_axes=3)
    _run_side_casts(extra_in[n_caches:], extra_out[n_caches:])
    l = pl.program_id(2)
    n_sub = chunk // SUB
    width = q_ref.shape[1]
    n_heads = width // dk
    heads = [slice(h * dk, (h + 1) * dk) for h in range(n_heads)]

    @pl.when(l == 0)
    def _():
        for h in range(n_heads):
            st_ref[h] = s0_ref[0, h].T

    q = q_ref[...].astype(F32)
    lf = lf_ref[...]
    kk = 1.0 - jnp.exp(lf)
    vb = v_ref[...]
    k_ref[...] = kk

    p0 = lf.astype(BF16)
    r1 = lf - p0.astype(F32)
    p1 = r1.astype(BF16)
    p2 = (r1 - p1.astype(F32)).astype(BF16)
    cum = cum_ref[...]
    b = jnp.concatenate(
        [_dot(cum, jnp.concatenate([p[c * chunk:(c + 1) * chunk, :] for p in (p0, p1, p2)], axis=0))
         for c in range(n_chunks)], axis=0)
    b_ref[...] = b
    tl = n_chunks * chunk

    def rows_of(ref, group, offset):
        return jnp.concatenate(
            [jnp.broadcast_to(ref[pl.ds(g * group + offset, 1), :], (group, width)) for g in range(tl // group)],
            axis=0)

    b_end = rows_of(b_ref, SUB, SUB - 1)
    b_mid = rows_of(b_ref, SUB, HALF - 1)
    b_last = rows_of(b_ref, chunk, chunk - 1)

    qe = (q * jnp.exp(b)).astype(BF16)
    k_dec = (kk * jnp.exp(b_end - b)).astype(BF16)
    k_end = (kk * jnp.exp(b_last - b)).astype(BF16)
    decay = [jnp.exp(b_ref[pl.ds((c + 1) * chunk - 1, 1), :]) for c in range(n_chunks)]
    q_dec = [(q * jnp.exp(jnp.minimum(b - rows_of(b_ref, chunk, (j + 1) * SUB - 1), 0.0))).astype(BF16)
             for j in range(n_sub - 1)]
    q_mid = (q * jnp.exp(jnp.minimum(b - b_mid, 0.0))).astype(BF16)
    k_mid = (kk * jnp.exp(jnp.minimum(b_mid - b, 0.0))).astype(BF16)

    w = [(q * jnp.exp(jnp.minimum(b - rows_of(b_ref, HALF, u), 0.0)) * rows_of(k_ref, HALF, u)).astype(BF16)
         for u in range(HALF)]
    sel = sel_ref[...]
    diag = [_dot(jnp.concatenate([wu[:, hc] for wu in w], axis=1), sel) for hc in heads]

    row = lax.broadcasted_iota(jnp.int32, (chunk, chunk), 0)
    col = lax.broadcasted_iota(jnp.int32, (chunk, chunk), 1)
    row_blk = row // SUB
    col_blk = col // SUB
    mid_mask = (col_blk == row_blk) & (row % SUB >= HALF) & (col % SUB < HALF)
    diag_mask = (col // HALF == row // HALF) & (col <= row)
    eye = (lax.broadcasted_iota(jnp.int32, (dk, dk), 0)
           == lax.broadcasted_iota(jnp.int32, (dk, dk), 1)).astype(F32).astype(BF16)

    chunks = [slice(c * chunk, (c + 1) * chunk) for c in range(n_chunks)]
    units = [(h, hc, c, rows) for h, hc in enumerate(heads) for c, rows in enumerate(chunks)]
    cross, v_t = [], []
    for h, hc, c, rows in units:
        qd = jnp.concatenate([qj[rows, hc] for qj in q_dec] + [q_mid[rows, hc]], axis=0)
        kd = jnp.concatenate([k_dec[rows, hc], k_mid[rows, hc]], axis=0)
        cross.append(_dot_nt(qd, kd))
        v_t.append(_dot_nt(eye, vb[rows, hc]).astype(BF16))

    intra = []
    for i, (h, hc, c, rows) in enumerate(units):
        m = cross[i]
        attn = jnp.where(mid_mask, m[(n_sub - 1) * chunk:, chunk:], 0.0)
        for j in range(n_sub - 1):
            attn = jnp.where((col_blk == j) & (row_blk > j), m[j * chunk:(j + 1) * chunk, :chunk], attn)
        attn = jnp.where(diag_mask, diag[h][rows, :], attn)
        intra.append(_dot(attn.astype(BF16), vb[rows, hc]))
        u_ref[i] = _dot(v_t[i], k_end[rows, hc])

    st = [st_ref[h] for h in range(n_heads)]
    for i, (h, hc, c, rows) in enumerate(units):
        sb_ref[i] = st[h].astype(BF16)
        st[h] = st[h] * decay[c][:, hc] + u_ref[i]
    for h in range(n_heads):
        st_ref[h] = st[h]

    inter = [_dot_nt(qe[rows, hc], sb_ref[i]) for i, (h, hc, c, rows) in enumerate(units)]
    gn = gn_ref[...]
    for h, hc in enumerate(heads):
        per_chunk = range(h * n_chunks, (h + 1) * n_chunks)
        o = jnp.concatenate([intra[i] + inter[i] for i in per_chunk], axis=0)
        ms = jnp.mean(o * o, axis=-1, keepdims=True)
        o = o * lax.rsqrt(ms + EPS) * gn[:, hc] * og_ref[:, hc].astype(F32)
        o_ref[:, hc] = o.astype(o_ref.dtype)

    @pl.when(l == pl.num_programs(2) - 1)
    def _():
        for h in range(n_heads):
            sfin_ref[0, h] = st[h].T


HGRN_UNITS_PER_STEP = 32


def hgrn2(pf, pb, gn, s0, layer, *, bsz, seq, heads, dk, tl, caches=(), side=()):
    chunk = min(CHUNK, seq)
    assert seq % tl == 0 and tl % chunk == 0 and chunk % SUB == 0
    nl = seq // tl
    d_a = heads * dk
    n_chunks = tl // chunk
    hb = max(1, min(heads, HGRN_UNITS_PER_STEP // n_chunks))
    while heads % hb:
        hb -= 1
    n_groups = heads // hb
    width = hb * dk
    n_units = hb * n_chunks

    def sec(k):
        return pl.BlockSpec((tl, width), lambda b, h, l, k=k: (b * nl + l, k * n_groups + h))

    cum, sel = _hgrn_consts(chunk, dk)
    grid = (bsz, n_groups, nl)
    if caches and _head_merge_blocks(caches[0]) > math.prod(grid):
        o_a, s_fin, *casts = hgrn2(pf, pb, gn, s0, layer, bsz=bsz, seq=seq, heads=heads, dk=dk, tl=tl, side=side)
        return (o_a, s_fin, *[c.reshape(*c.shape[:3], -1) for c in caches], *casts)
    cache_in, cache_out, cache_shapes, cache_vmem = _head_merge_specs(caches, grid)
    side_in, side_out, side_shapes, side_vmem = _side_cast_specs(side, grid, flat=True)
    cache_in, cache_out, cache_shapes = cache_in + side_in, cache_out + side_out, cache_shapes + side_shapes
    est = (2 * tl * width * (4 + 3 * 2) + 2 * tl * width * 2 + 6 * hb * dk * dk * 4 + 2 * tl * width * 4
           + n_units * dk * dk * 6 + 2 * (cum.size + sel.size) * 2 + 24 * tl * width * 4
           + cache_vmem + side_vmem)
    kern = functools.partial(_hgrn_kernel, chunk=chunk, n_chunks=n_chunks, dk=dk, n_caches=len(caches),
                             n_cache_blocks=_head_merge_blocks(caches[0]) if caches else 0, n_side=len(side))
    return pl.pallas_call(
        kern,
        grid=grid,
        in_specs=[
            sec(IN_SEC_Q - IN_F32_SECTIONS), sec(IN_SEC_LOGF),
            sec(IN_SEC_V - IN_F32_SECTIONS), sec(IN_SEC_OG - IN_F32_SECTIONS),
            pl.BlockSpec((1, width), lambda b, h, l: (0, h)),
            pl.BlockSpec((None, 1, hb, dk, dk), lambda b, h, l: (layer, b, h, 0, 0)),
            pl.BlockSpec(cum.shape, lambda b, h, l: (0, 0)),
            pl.BlockSpec(sel.shape, lambda b, h, l: (0, 0)),
            *cache_in,
        ],
        out_specs=[
            pl.BlockSpec((tl, width), lambda b, h, l: (b * nl + l, h)),
            pl.BlockSpec((1, hb, dk, dk), lambda b, h, l: (b, h, 0, 0)),
            *cache_out,
        ],
        out_shape=[
            jax.ShapeDtypeStruct((bsz * seq, d_a), BF16),
            jax.ShapeDtypeStruct((bsz, heads, dk, dk), F32),
            *cache_shapes,
        ],
        scratch_shapes=[
            pltpu.VMEM((hb, dk, dk), F32),
            pltpu.VMEM((tl, width), F32),
            pltpu.VMEM((tl, width), F32),
            pltpu.VMEM((n_units, dk, dk), F32),
            pltpu.VMEM((n_units, dk, dk), BF16),
        ],
        compiler_params=_params(("parallel", "parallel", "arbitrary"), est),
        name="hgrn2",
    )(pb, pf, pb, pb, gn, s0, cum, sel, *caches, *[w_ for w_, _ in side])


def _lru_kernel(rx_ref, gy_ref, cw_ref, cb_ref, wax_ref, ba_ref, bx_ref, lam_ref, h0_ref, buf_ref,
                *rest, tl, n_blocks, bw, n_side):
    side_in, (o_ref, hlast_ref), side_out, (xp_ref, a_ref, u_ref, hs_ref, ps_ref, h_ref) = (
        rest[:n_side], rest[n_side:n_side + 2], rest[n_side + 2:2 * n_side + 2], rest[2 * n_side + 2:])
    _run_side_casts(side_in, side_out)
    l = pl.program_id(1)
    keep = CONV_W - 1

    @pl.when(l == 0)
    def _():
        xp_ref[CONV_PAD - keep:CONV_PAD, :] = buf_ref[0]
        h_ref[...] = h0_ref[0]

    x = rx_ref[...]
    xp_ref[CONV_PAD:CONV_PAD + tl, :] = x
    cw = cw_ref[...]
    xc = xp_ref[CONV_PAD - keep:CONV_PAD - keep + tl, :] * cw[0:1, :]
    for j in range(1, CONV_W):
        xc = xc + xp_ref[CONV_PAD - keep + j:CONV_PAD - keep + j + tl, :] * cw[j:j + 1, :]
    xc = xc + cb_ref[...]
    xp_ref[CONV_PAD - keep:CONV_PAD, :] = xp_ref[CONV_PAD + tl - keep:CONV_PAD + tl, :]

    xcb = xc.astype(BF16)
    pre = [_dot(xcb[:, n * bw:(n + 1) * bw], wax_ref[n]) for n in range(n_blocks)]
    r = jax.nn.sigmoid(jnp.concatenate([pn[:, :bw] for pn in pre], axis=-1) + ba_ref[...])
    ig = jax.nn.sigmoid(jnp.concatenate([pn[:, bw:] for pn in pre], axis=-1) + bx_ref[...])
    lam = lam_ref[...]
    softplus_neg = jnp.maximum(-lam, 0.0) + jnp.log1p(jnp.exp(-jnp.abs(lam)))
    log_a = -LRU_C * r * softplus_neg
    a = jnp.exp(log_a)
    mult = jnp.sqrt(jnp.maximum(-jnp.tanh(log_a) * (a * a + 1.0), 0.0))
    a_ref[...] = a
    u_ref[...] = mult * (ig * xc)

    seg = tl // LRU_SEGMENTS
    one = jnp.ones_like(h_ref[...])

    def step(t, carry):
        hs, ps = carry
        new_h, new_p = [], []
        for s in range(LRU_SEGMENTS):
            r = s * seg + t
            a_t = a_ref[pl.ds(r, 1), :]
            h_s = a_t * hs[s] + u_ref[pl.ds(r, 1), :]
            hs_ref[pl.ds(r, 1), :] = h_s
            new_h.append(h_s)
            if s > 0:
                p_s = a_t * ps[s - 1]
                ps_ref[pl.ds(r, 1), :] = p_s
                new_p.append(p_s)
        return tuple(new_h), tuple(new_p)

    init = ((h_ref[...],) + (jnp.zeros_like(one),) * (LRU_SEGMENTS - 1), (one,) * (LRU_SEGMENTS - 1))
    hs, ps = lax.fori_loop(0, seg, step, init, unroll=min(seg, 4))
    h = hs[0]
    gy = gy_ref[...].astype(F32)
    o_ref[0:seg, :] = (hs_ref[0:seg, :] * gy[0:seg, :]).astype(o_ref.dtype)
    for s in range(1, LRU_SEGMENTS):
        rows = slice(s * seg, (s + 1) * seg)
        o_ref[rows, :] = ((hs_ref[rows, :] + ps_ref[rows, :] * h) * gy[rows, :]).astype(o_ref.dtype)
        h = hs[s] + ps[s - 1] * h
    h_ref[...] = h

    @pl.when(l == pl.num_programs(1) - 1)
    def _():
        hlast_ref[0] = h


def conv_lru(pf, pb, cw, cb, wax, ba, bx, lam, h0, buf, layer, *, bsz, seq, d, tl, side=()):
    assert seq % tl == 0 and tl >= CONV_W - 1 and tl % LRU_SEGMENTS == 0
    nl = seq // tl
    n_blocks, bw = wax.shape[0], wax.shape[1]
    vec = pl.BlockSpec((1, d), lambda b, l: (0, 0))
    side_in, side_out, side_shapes, side_vmem = _side_cast_specs(side, (bsz, nl), flat=True)
    est = (2 * tl * d * (4 + 2) + 2 * tl * d * 2 + (3 * tl + CONV_PAD) * d * 4 + 8 * tl * d * 4
           + side_vmem)
    kern = functools.partial(_lru_kernel, tl=tl, n_blocks=n_blocks, bw=bw, n_side=len(side))
    return pl.pallas_call(
        kern,
        grid=(bsz, nl),
        in_specs=[
            pl.BlockSpec((tl, d), lambda b, l: (b * nl + l, IN_SEC_RX)),
            pl.BlockSpec((tl, d), lambda b, l: (b * nl + l, IN_SEC_RY - IN_F32_SECTIONS)),
            pl.BlockSpec((CONV_W, d), lambda b, l: (0, 0)),
            vec,
            pl.BlockSpec((n_blocks, bw, 2 * bw), lambda b, l: (0, 0, 0)),
            vec, vec, vec,
            pl.BlockSpec((None, 1, 1, d), lambda b, l: (layer, b, 0, 0)),
            pl.BlockSpec((None, 1, CONV_W - 1, d), lambda b, l: (layer, b, 0, 0)),
            *side_in,
        ],
        out_specs=[
            pl.BlockSpec((tl, d), lambda b, l: (b * nl + l, 0)),
            pl.BlockSpec((1, 1, d), lambda b, l: (b, 0, 0)),
            *side_out,
        ],
        out_shape=[
            jax.ShapeDtypeStruct((bsz * seq, d), BF16),
            jax.ShapeDtypeStruct((bsz, 1, d), F32),
            *side_shapes,
        ],
        scratch_shapes=[
            pltpu.VMEM((CONV_PAD + tl, d), F32),
            pltpu.VMEM((tl, d), F32),
            pltpu.VMEM((tl, d), F32),
            pltpu.VMEM((tl, d), F32),
            pltpu.VMEM((tl, d), F32),
            pltpu.VMEM((1, d), F32),
        ],
        compiler_params=_params(("parallel", "arbitrary"), est),
        name="conv_lru",
    )(pf, pb, cw, cb, wax, ba, bx, lam, h0, buf, *[w_ for w_, _ in side])


def _mem_attn_kernel(q_ref, k_ref, v_ref, o_ref, *, scale, heads, hd):
    cols = [slice(h * hd, (h + 1) * hd) for h in range(heads)]
    scores = [_dot_nt(q_ref[:, c], k_ref[0, :, c].astype(BF16)) * scale for c in cols]
    probs = [jnp.exp(s - jnp.max(s, axis=-1, keepdims=True)) for s in scores]
    outs = [_dot(p.astype(BF16), v_ref[0, :, c].astype(BF16)) for p, c in zip(probs, cols)]
    for p, o, c in zip(probs, outs, cols):
        o_ref[:, c] = (o / jnp.sum(p, axis=-1, keepdims=True)).astype(o_ref.dtype)


def mem_attn(pb, mem_k, mem_v, layer, *, bsz, seq, heads, hd, col0, k_col, v_col, tl):
    assert seq % tl == 0
    nl = seq // tl
    n_mem = mem_k.shape[2]
    d_c = heads * hd
    est = 4 * tl * d_c * 2 + 4 * n_mem * d_c * 4 + 6 * tl * n_mem * 4
    kern = functools.partial(_mem_attn_kernel, scale=1.0 / math.sqrt(hd), heads=heads, hd=hd)
    return pl.pallas_call(
        kern,
        grid=(bsz, nl),
        in_specs=[
            pl.BlockSpec((tl, d_c), lambda b, l: (b * nl + l, col0)),
            pl.BlockSpec((None, 1, n_mem, d_c), lambda b, l: (layer, b, 0, k_col)),
            pl.BlockSpec((None, 1, n_mem, d_c), lambda b, l: (layer, b, 0, v_col)),
        ],
        out_specs=pl.BlockSpec((tl, d_c), lambda b, l: (b * nl + l, 0)),
        out_shape=jax.ShapeDtypeStruct((bsz * seq, d_c), BF16),
        compiler_params=_params(("parallel", "parallel"), est),
        name="mem_attn",
    )(pb, mem_k, mem_v)


def _merge_kernel(x_ref, oa_ref, ob_ref, oc_ref, g0_ref, g1_ref, g2_ref,
                  wa_ref, wb_ref, wc_ref, wo_ref, gn_ref, *rest, n_side):
    side_in, y_ref, side_out = rest[:n_side], rest[n_side], rest[n_side + 1:]
    _run_side_casts(side_in, side_out)
    m = g0_ref[...].astype(F32) * _dot(oa_ref[...], wa_ref[...])
    m = m + g1_ref[...].astype(F32) * _dot(ob_ref[...], wb_ref[...])
    m = m + g2_ref[...].astype(F32) * _dot(oc_ref[...], wc_ref[...])
    z = _dot(m.astype(BF16), wo_ref[...])
    y_ref[...] = x_ref[...] + _rms(z, gn_ref[...])


def merge(x, oa, ob, oc, pb, wa, wb, wc, wo, gn, *, col_gates, tm, side=()):
    t, d = x.shape
    db = oa.shape[1]
    assert t % tm == 0
    row = lambda i: (i, 0)
    const = lambda i: (0, 0)
    wspec = lambda rows: pl.BlockSpec((None, rows, d), lambda i: (0, 0, 0), pipeline_mode=pl.Buffered(1))
    gate_specs = [pl.BlockSpec((tm, d), lambda i, k=k: (i, col_gates + k)) for k in range(N_GATES)]
    side_in, side_out, side_shapes, side_vmem = _side_cast_specs(side, (t // tm,))
    est = (4 * tm * d * 4 + 6 * tm * db * 2 + 6 * tm * d * 2
           + 3 * db * d * 2 + d * d * 2 + 6 * tm * d * 4 + side_vmem)
    return pl.pallas_call(
        functools.partial(_merge_kernel, n_side=len(side)),
        grid=(t // tm,),
        in_specs=[
            pl.BlockSpec((tm, d), row),
            pl.BlockSpec((tm, db), row), pl.BlockSpec((tm, db), row), pl.BlockSpec((tm, db), row),
            *gate_specs,
            wspec(db), wspec(db), wspec(db), wspec(d),
            pl.BlockSpec((1, d), const),
            *side_in,
        ],
        out_specs=[pl.BlockSpec((tm, d), row), *side_out],
        out_shape=[jax.ShapeDtypeStruct((t, d), F32), *side_shapes],
        compiler_params=_params(("parallel",), est),
        name="merge",
    )(x, oa, ob, oc, pb, pb, pb, wa, wb, wc, wo, gn, *[w_ for w_, _ in side])


def _ffn_kernel(x_ref, gpre_ref, wg_ref, wu_ref, wd_ref, gpost_ref, y_ref, h_ref, acc_ref):
    j = pl.program_id(1)

    @pl.when(j == 0)
    def _():
        h_ref[...] = _rms(x_ref[...], gpre_ref[...]).astype(BF16)
        acc_ref[...] = jnp.zeros_like(acc_ref)

    h = h_ref[...]
    gt = _dot(h, wg_ref[...])
    up = _dot(h, wu_ref[...])
    act = (jax.nn.silu(gt) * up).astype(BF16)
    acc_ref[...] += _dot(act, wd_ref[...])

    @pl.when(j == pl.num_programs(1) - 1)
    def _():
        y_ref[...] = x_ref[...] + _rms(acc_ref[...], gpost_ref[...])


def ffn(x, gpre, w_gu, w_down, gpost, *, tm, tf):
    t, d = x.shape
    d_ff = w_down.shape[1]
    assert t % tm == 0 and d_ff % tf == 0
    nf = d_ff // tf
    est = 4 * tm * d * 4 + 2 * 3 * d * tf * 2 + tm * d * 2 + tm * d * 4 + 4 * tm * tf * 4
    return pl.pallas_call(
        _ffn_kernel,
        grid=(t // tm, nf),
        in_specs=[
            pl.BlockSpec((tm, d), lambda i, j: (i, 0)),
            pl.BlockSpec((1, d), lambda i, j: (0, 0)),
            pl.BlockSpec((None, d, tf), lambda i, j: (0, 0, j)),
            pl.BlockSpec((None, d, tf), lambda i, j: (0, 0, nf + j)),
            pl.BlockSpec((None, tf, d), lambda i, j: (0, j, 0)),
            pl.BlockSpec((1, d), lambda i, j: (0, 0)),
        ],
        out_specs=pl.BlockSpec((tm, d), lambda i, j: (i, 0)),
        out_shape=jax.ShapeDtypeStruct((t, d), F32),
        scratch_shapes=[pltpu.VMEM((tm, d), BF16), pltpu.VMEM((tm, d), F32)],
        compiler_params=_params(("parallel", "arbitrary"), est),
        name="ffn",
    )(x, gpre, w_gu, w_gu, w_down, gpost)


def _row_tile(n, target):
    t = min(n, target)
    while n % t:
        t //= 2
    return t


IN_PROJ_ROWS, IN_PROJ_COLS, IN_PROJ_EPILOGUE_ROWS = 1024, 1024, 256
HGRN_ROWS = 2048
LRU_ROWS = 512
MEM_ATTN_ROWS = 1024
MERGE_ROWS = 256
FFN_ROWS, FFN_COLS = 512, 512
MEM_KV_ROWS, MEM_KV_COLS = 512, 1024


NEXT_LAYER_HOSTS = {
    "in_proj": ("w_in",),
    "conv_lru": ("ffn_w_gu", "ffn_w_down"),
    "merge": ("w_branch_a", "w_branch_b", "w_branch_c", "w_out", "mem_w_kv"),
}
SAME_LAYER_HOSTS = {
    "hgrn2": ("w_branch_a", "w_branch_b", "w_branch_c", "w_out", "mem_w_kv"),
    "conv_lru": ("ffn_w_gu", "ffn_w_down"),
}


def _trunk_layer(x2, bsz, seq, mem, state, lb, w, p, jobs=None, caches=()):
    t, d = x2.shape
    s_hg, h_lru, conv_buf, state_layer = state
    heads, dk = s_hg.shape[2], s_hg.shape[3]
    d_a = heads * dk
    d_b = h_lru.shape[-1]
    mem_heads, hd = p["mem_heads"], p["mem_hd"]
    if seq < CONV_W - 1:
        raise NotImplementedError("sequence shorter than the conv history")
    jobs = jobs or {}
    w = dict(w)
    w_next = {}

    def side(host):
        return [(a, layer) for _, a, layer, _ in jobs.get(host, ())]

    def keep(host, outs):
        for (name, _, _, for_next), o in zip(jobs.get(host, ()), outs):
            (w_next if for_next else w)[name] = o

    pf, pb, *extra = in_proj(x2, p["norm_pre_mix"], w["w_in"], lb, p["b_gate"], sec=d_a,
                             tm=_row_tile(t, IN_PROJ_ROWS), tn=IN_PROJ_COLS, side=side("in_proj"))
    keep("in_proj", extra)
    n_bf_sections = pb.shape[1] // d_a

    o_a, s_new, *extra = hgrn2(pf, pb, p["hgrn_out_norm"], s_hg, state_layer, bsz=bsz, seq=seq, heads=heads,
                               dk=dk, tl=_row_tile(seq, HGRN_ROWS), caches=caches, side=side("hgrn2"))
    merged_caches = extra[:len(caches)]
    keep("hgrn2", extra[len(caches):])
    o_b, h_last, *extra = conv_lru(pf, pb, p["conv_w"], p["conv_b"], p["lru_wax"], p["lru_ba"], p["lru_bx"],
                                   p["lru_lambda"], h_lru, conv_buf, state_layer, bsz=bsz, seq=seq, d=d_b,
                                   tl=_row_tile(seq, LRU_ROWS), side=side("conv_lru"))
    keep("conv_lru", extra)
    mem_k, mem_v, mem_layer, k_col, v_col = mem(w)
    o_c = mem_attn(pb, mem_k, mem_v, mem_layer, bsz=bsz, seq=seq, heads=mem_heads, hd=hd,
                   col0=n_bf_sections - 1, k_col=k_col, v_col=v_col, tl=_row_tile(seq, MEM_ATTN_ROWS))

    gate_col = (IN_SEC_GATES - IN_F32_SECTIONS) * d_a
    assert gate_col % d == 0
    x2, *extra = merge(x2, o_a, o_b, o_c, pb, w["w_branch_a"], w["w_branch_b"], w["w_branch_c"], w["w_out"],
                       p["norm_post_mix"], col_gates=gate_col // d, tm=_row_tile(t, MERGE_ROWS),
                       side=side("merge"))
    keep("merge", extra)
    x2 = ffn(x2, p["norm_pre_ffn"], w["ffn_w_gu"], w["ffn_w_down"], p["norm_post_ffn"],
             tm=_row_tile(t, FFN_ROWS), tf=FFN_COLS)

    rx_tail = pf.reshape(bsz, seq, -1)[:, seq - (CONV_W - 1):, IN_SEC_RX * d_a:(IN_SEC_RX + 1) * d_a]
    return x2, s_new, h_last.reshape(bsz, d_b), rx_tail, w, w_next, merged_caches


def kernel(x_prompt, x_sample, state_hgrn, state_lru, state_conv, cache_mem_k, cache_mem_v, mem_prompt, norm_mem, mem_w_kv, hgrn_lower_bound, norm_pre_mix, w_in, b_gate, hgrn_out_norm, conv_w, conv_b, lru_wa, lru_ba, lru_wx, lru_bx, lru_lambda, w_branch_a, w_branch_b, w_branch_c, w_out, norm_post_mix, norm_pre_ffn, ffn_w_gu, ffn_w_down, norm_post_ffn):
    depth = w_in.shape[0]
    bp, sp, d = x_prompt.shape
    bs, ss, _ = x_sample.shape
    _, _, heads, dk, dv = state_hgrn.shape
    d_a = heads * dk
    d_b = state_lru.shape[-1]
    n_mem, mem_heads, hd = cache_mem_k.shape[2:]
    d_c = mem_heads * hd
    assert dk == dv and d_b == d_a and d_c == d_a and d == 2 * d_a
    assert b_gate.shape[1] == N_GATES * d

    sm = jax.nn.softmax(hgrn_lower_bound.astype(F32), axis=0)
    lbs = jnp.cumsum(sm, axis=0) - sm[0:1]

    xp = x_prompt.reshape(bp * sp, d)
    xs = x_sample.reshape(bs * ss, d)
    mem2 = mem_prompt.reshape(bp * n_mem, d)
    zero_state = (jnp.zeros((1, bp, heads, dk, dv), F32), jnp.zeros((1, bp, 1, d_b), F32),
                  jnp.zeros((1, bp, CONV_W - 1, d_b), F32), 0)
    lru_s4 = state_lru.reshape(depth, bs, 1, d_b)

    w_f32 = dict(w_in=w_in, w_branch_a=w_branch_a, w_branch_b=w_branch_b, w_branch_c=w_branch_c, w_out=w_out,
                 ffn_w_gu=ffn_w_gu, ffn_w_down=ffn_w_down, mem_w_kv=mem_w_kv)
    assert set(w_f32) == {name for names in NEXT_LAYER_HOSTS.values() for name in names}
    assert set(w_f32) == {"w_in"} | {name for names in SAME_LAYER_HOSTS.values() for name in names}
    w = {"w_in": cast_bf16(w_in, 0)}

    outs = {k: [] for k in ("hg_p", "lru_p", "conv_p", "mk_p", "mv_p", "hg_s", "lru_s", "conv_s")}
    for l in range(depth):
        row = lambda a: a[l].reshape(1, -1)
        p = dict(
            mem_heads=mem_heads, mem_hd=hd,
            norm_pre_mix=row(norm_pre_mix), b_gate=row(b_gate),
            hgrn_out_norm=row(hgrn_out_norm), conv_w=conv_w[l], conv_b=row(conv_b),
            lru_wax=jnp.concatenate([lru_wa[l], lru_wx[l]], axis=-1).astype(BF16),
            lru_ba=row(lru_ba), lru_bx=row(lru_bx), lru_lambda=row(lru_lambda),
            norm_post_mix=row(norm_post_mix), norm_pre_ffn=row(norm_pre_ffn),
            norm_post_ffn=row(norm_post_ffn),
        )
        lb = lbs[l].reshape(1, -1)

        prompt_kv = []

        def prompt_mem(w_now):
            kv = norm_matmul(mem2, row(norm_mem), w_now["mem_w_kv"], tm=_row_tile(bp * n_mem, MEM_KV_ROWS),
                             tn=MEM_KV_COLS)
            prompt_kv.append(kv.reshape(1, bp, n_mem, 2 * d_c))
            return prompt_kv[0], prompt_kv[0], 0, 0, 1

        jobs = {}
        if l == 0:
            for host, names in SAME_LAYER_HOSTS.items():
                jobs.setdefault(host, []).extend((name, w_f32[name], 0, False) for name in names)
        if l + 1 < depth:
            for host, names in NEXT_LAYER_HOSTS.items():
                jobs.setdefault(host, []).extend((name, w_f32[name], l + 1, True) for name in names)
        caches = (cache_mem_k, cache_mem_v) if l == 0 else ()
        xp, s1, h1, c1, w, w_next, merged = _trunk_layer(xp, bp, sp, prompt_mem, zero_state, lb, w, p,
                                                         jobs, caches)
        if l == 0:
            cache_k, cache_v = merged
        kv4 = prompt_kv[0]
        outs["hg_p"].append(s1); outs["lru_p"].append(h1); outs["conv_p"].append(c1)
        outs["mk_p"].append(kv4[0, :, :, :d_c].reshape(bp, n_mem, mem_heads, hd))
        outs["mv_p"].append(kv4[0, :, :, d_c:].reshape(bp, n_mem, mem_heads, hd))

        xs, s2, h2, c2, _, _, _ = _trunk_layer(xs, bs, ss, lambda _: (cache_k, cache_v, l, 0, 0),
                                               (state_hgrn, lru_s4, state_conv, l), lb, w, p)
        outs["hg_s"].append(s2); outs["lru_s"].append(h2); outs["conv_s"].append(c2)
        w = w_next

    st = {k: jnp.stack(v) for k, v in outs.items()}
    return (xp.reshape(bp, sp, d), xs.reshape(bs, ss, d), st["hg_p"], st["lru_p"], st["conv_p"],
            st["mk_p"], st["mv_p"], st["hg_s"], st["lru_s"], st["conv_s"])
```

```python
import functools
import math

import numpy as np
import jax
import jax.numpy as jnp
from jax import lax
from jax.experimental import pallas as pl
from jax.experimental.pallas import tpu as pltpu

F32 = jnp.float32
BF16 = jnp.bfloat16

EPS = 1e-6
LRU_C = 8.0
CHUNK = 64
SUB = 16
HALF = 8
N_PIECES = 3
CONV_W = 4
CONV_PAD = 8
LRU_SEGMENTS = 4
N_GATES = 3

LANES = 128
F32_SUBLANES = 8
BF16_SUBLANES = 16

V7X_VMEM_BYTES = 64 * 1024 * 1024
VMEM_LIMIT_CAP = V7X_VMEM_BYTES - 8 * 1024 * 1024


def _vmem_limit(estimate_bytes):
    return int(min(VMEM_LIMIT_CAP, max(16 * 1024 * 1024, estimate_bytes * 5 // 4)))


def _params(sem, vmem_estimate):
    return pltpu.CompilerParams(dimension_semantics=sem, vmem_limit_bytes=_vmem_limit(vmem_estimate))


def _rms(x, g):
    ms = jnp.mean(x * x, axis=-1, keepdims=True)
    return x * lax.rsqrt(ms + EPS) * g


def _dot(a, b):
    return jnp.dot(a, b, preferred_element_type=F32)


def _dot_nt(a, b):
    return lax.dot_general(a, b, (((1,), (1,)), ((), ())), preferred_element_type=F32)


def _cast_kernel(w_ref, o_ref):
    o_ref[...] = w_ref[...].astype(o_ref.dtype)


CAST_BLOCK_BYTES = 4 * 1024 * 1024


def cast_bf16(w, layer):
    _, r, c = w.shape
    tr = r
    while tr * c * 4 > CAST_BLOCK_BYTES and tr % (2 * BF16_SUBLANES) == 0:
        tr //= 2
    return pl.pallas_call(
        _cast_kernel,
        grid=(r // tr,),
        in_specs=[pl.BlockSpec((None, tr, c), lambda i: (layer, i, 0))],
        out_specs=pl.BlockSpec((None, tr, c), lambda i: (0, i, 0)),
        out_shape=jax.ShapeDtypeStruct((1, r, c), BF16),
        compiler_params=_params(("parallel",), 2 * tr * c * 6),
        name="cast_bf16",
    )(w)


def _flat_step(grid, ids):
    g = 0
    for n, i in zip(grid, ids):
        g = g * n + i
    return g


def _side_cast_specs(side, grid, flat=False):
    in_specs, out_specs, out_shapes, vmem = [], [], [], 0
    n_i = math.prod(grid) if flat else grid[0]
    n_j = 1 if flat or len(grid) < 2 else grid[1]
    for w, layer in side:
        _, r, c = w.shape
        assert r % n_i == 0 and (r // n_i) % BF16_SUBLANES == 0
        br = r // n_i
        split = n_j > 1 and c % n_j == 0 and (c // n_j) % LANES == 0
        bc = c // n_j if split else c

        def block(ids, split=split):
            return (_flat_step(grid, ids), 0) if flat else (ids[0], ids[1] if split else 0)

        in_specs.append(pl.BlockSpec((None, br, bc), lambda *ids, layer=layer, block=block: (layer, *block(ids))))
        out_specs.append(pl.BlockSpec((None, br, bc), lambda *ids, block=block: (0, *block(ids))))
        out_shapes.append(jax.ShapeDtypeStruct((1, r, c), BF16))
        vmem += 2 * br * bc * 6
    return in_specs, out_specs, out_shapes, vmem


def _run_side_casts(side_in, side_out):
    for w_ref, o_ref in zip(side_in, side_out):
        o_ref[...] = w_ref[...].astype(o_ref.dtype)


HEAD_MERGE_ROWS = 128


def _head_merge_blocks(cache):
    depth, bsz, n_mem = cache.shape[:3]
    return depth * bsz * (n_mem // min(HEAD_MERGE_ROWS, n_mem))


def _head_merge_specs(caches, grid):
    in_specs, out_specs, out_shapes, vmem = [], [], [], 0
    for c in caches:
        depth, bsz, n_mem, heads, hd = c.shape
        rows = min(HEAD_MERGE_ROWS, n_mem)
        assert n_mem % rows == 0
        per_b = n_mem // rows
        n_blocks = _head_merge_blocks(c)
        assert n_blocks <= math.prod(grid)

        def where(ids, bsz=bsz, per_b=per_b, n_blocks=n_blocks):
            g = jnp.minimum(_flat_step(grid, ids), n_blocks - 1)
            return g // (bsz * per_b), (g // per_b) % bsz, g % per_b

        in_specs.append(pl.BlockSpec((None, 1, rows, heads, hd), lambda *ids, where=where: (*where(ids), 0, 0)))
        out_specs.append(pl.BlockSpec((None, 1, rows, heads * hd), lambda *ids, where=where: (*where(ids), 0)))
        out_shapes.append(jax.ShapeDtypeStruct((depth, bsz, n_mem, heads * hd), c.dtype))
        vmem += 2 * rows * (F32_SUBLANES * hd + heads * hd) * 4
    return in_specs, out_specs, out_shapes, vmem


def _run_head_merges(ins, outs, n_blocks, n_axes):
    if not ins:
        return
    step = _flat_step([pl.num_programs(a) for a in range(n_axes)], [pl.program_id(a) for a in range(n_axes)])

    @pl.when(step < n_blocks)
    def _():
        for c_ref, o_ref in zip(ins, outs):
            heads, hd = c_ref.shape[2], c_ref.shape[3]
            for h in range(heads):
                o_ref[0, :, h * hd:(h + 1) * hd] = c_ref[0, :, h, :]


def _norm_matmul_kernel(x_ref, g_ref, w_ref, o_ref, h_ref):
    @pl.when(pl.program_id(1) == 0)
    def _():
        h_ref[...] = _rms(x_ref[...], g_ref[...]).astype(BF16)

    o_ref[...] = _dot(h_ref[...], w_ref[...]).astype(o_ref.dtype)


def norm_matmul(x, g, w, *, tm, tn, out_dtype=F32):
    t, d = x.shape
    n = w.shape[2]
    assert t % tm == 0 and n % tn == 0
    est = 2 * tm * d * 4 + 2 * d * tn * 2 + 2 * tm * tn * 4 + tm * d * 2
    return pl.pallas_call(
        _norm_matmul_kernel,
        grid=(t // tm, n // tn),
        in_specs=[
            pl.BlockSpec((tm, d), lambda i, j: (i, 0)),
            pl.BlockSpec((1, d), lambda i, j: (0, 0)),
            pl.BlockSpec((None, d, tn), lambda i, j: (0, 0, j)),
        ],
        out_specs=pl.BlockSpec((tm, tn), lambda i, j: (i, j)),
        out_shape=jax.ShapeDtypeStruct((t, n), out_dtype),
        scratch_shapes=[pltpu.VMEM((tm, d), BF16)],
        compiler_params=_params(("parallel", "arbitrary"), est),
        name="norm_matmul",
    )(x, g, w)


IN_F32_SECTIONS = 2
IN_SEC_LOGF, IN_SEC_RX, IN_SEC_Q, IN_SEC_V, IN_SEC_OG, IN_SEC_RY, IN_SEC_GATES = range(7)


def _in_proj_kernel(perm_ref, x_ref, g_ref, w_ref, lb_ref, bg_ref, *rest, tps, n_gate_secs, sub_rows, n_side):
    del perm_ref
    side_in, (of_ref, ob_ref), side_out, (h_ref,) = (
        rest[:n_side], rest[n_side:n_side + 2], rest[n_side + 2:2 * n_side + 2], rest[2 * n_side + 2:])
    _run_side_casts(side_in, side_out)
    j = pl.program_id(1)
    sec = j // tps

    @pl.when(j == 0)
    def _():
        h_ref[...] = _rms(x_ref[...], g_ref[...]).astype(BF16)

    tm = h_ref.shape[0]

    def run(out_ref, act):
        for r in range(tm // sub_rows):
            rs = slice(r * sub_rows, (r + 1) * sub_rows)
            out_ref[rs, :] = act(_dot(h_ref[rs, :], w_ref[...])).astype(out_ref.dtype)

    def log_forget(a):
        lb = lb_ref[...]
        return jnp.log(lb + (1.0 - lb) * jax.nn.sigmoid(a))

    sec_cq = IN_SEC_GATES + n_gate_secs
    pl.when(sec == IN_SEC_LOGF)(lambda: run(of_ref, log_forget))
    pl.when(sec == IN_SEC_RX)(lambda: run(of_ref, lambda a: a))
    pl.when((sec == IN_SEC_Q) | (sec == IN_SEC_OG))(lambda: run(ob_ref, jax.nn.silu))
    pl.when((sec == IN_SEC_V) | (sec == sec_cq))(lambda: run(ob_ref, lambda a: a))
    pl.when(sec == IN_SEC_RY)(lambda: run(ob_ref, jax.nn.gelu))
    pl.when((sec >= IN_SEC_GATES) & (sec < sec_cq))(
        lambda: run(ob_ref, lambda a: jax.nn.sigmoid(a + bg_ref[...])))


def in_proj(x, g, w, lb, bg, *, sec, tm, tn, side=()):
    t, d = x.shape
    n = w.shape[2]
    assert t % tm == 0 and sec % tn == 0 and n % sec == 0
    tps = sec // tn
    n_sec = n // sec
    n_gate_secs = bg.shape[1] // sec
    assert n_sec == IN_SEC_GATES + n_gate_secs + 1
    perm = jnp.asarray([1, 4, 0, 2, 3, 5] + list(range(7, 7 + n_gate_secs)) + [6], jnp.int32)
    nf = IN_F32_SECTIONS * tps
    sub_rows = min(tm, IN_PROJ_EPILOGUE_ROWS)
    grid = (t // tm, n // tn)
    side_in, side_out, side_shapes, side_vmem = _side_cast_specs(side, grid)
    est = (2 * tm * d * 4 + 2 * d * tn * 2 + 2 * tm * tn * 4 + 2 * tm * tn * 2 + tm * d * 2
           + 6 * sub_rows * tn * 4 + side_vmem)
    kern = functools.partial(_in_proj_kernel, tps=tps, n_gate_secs=n_gate_secs, sub_rows=sub_rows,
                             n_side=len(side))
    grid_spec = pltpu.PrefetchScalarGridSpec(
        num_scalar_prefetch=1,
        grid=grid,
        in_specs=[
            pl.BlockSpec((tm, d), lambda i, j, perm: (i, 0)),
            pl.BlockSpec((1, d), lambda i, j, perm: (0, 0)),
            pl.BlockSpec((None, d, tn), lambda i, j, perm: (0, 0, perm[j // tps] * tps + j % tps)),
            pl.BlockSpec((1, tn), lambda i, j, perm: (0, jnp.minimum(j, tps - 1))),
            pl.BlockSpec((1, tn), lambda i, j, perm: (0, jnp.clip(j - IN_SEC_GATES * tps, 0,
                                                                   n_gate_secs * tps - 1))),
            *side_in,
        ],
        out_specs=[
            pl.BlockSpec((tm, tn), lambda i, j, perm: (i, jnp.minimum(j, nf - 1))),
            pl.BlockSpec((tm, tn), lambda i, j, perm: (i, jnp.maximum(j - nf, 0))),
            *side_out,
        ],
        scratch_shapes=[pltpu.VMEM((tm, d), BF16)],
    )
    return pl.pallas_call(
        kern,
        grid_spec=grid_spec,
        out_shape=[
            jax.ShapeDtypeStruct((t, IN_F32_SECTIONS * sec), F32),
            jax.ShapeDtypeStruct((t, n - IN_F32_SECTIONS * sec), BF16),
            *side_shapes,
        ],
        compiler_params=_params(("parallel", "arbitrary"), est),
        name="in_proj",
    )(perm, x, g, w, lb, bg, *[w_ for w_, _ in side])


def _hgrn_consts(chunk, dk):
    t = np.arange(chunk)[:, None]
    s = np.arange(chunk)[None, :]
    cum = np.concatenate([(s <= t).astype(np.float32)] * N_PIECES, axis=1)
    lane_blk = np.arange(HALF * dk)[:, None] // dk
    sel = (lane_blk == (np.arange(chunk)[None, :] % HALF)).astype(np.float32)
    return jnp.asarray(cum, BF16), jnp.asarray(sel, BF16)


def _hgrn_kernel(q_ref, lf_ref, v_ref, og_ref, gn_ref, s0_ref, cum_ref, sel_ref, *rest,
                 chunk, n_chunks, dk, n_caches, n_cache_blocks, n_side):
    n_extra = n_caches + n_side
    extra_in, (o_ref, sfin_ref), extra_out, (st_ref, b_ref, k_ref, u_ref, sb_ref) = (
        rest[:n_extra], rest[n_extra:n_extra + 2], rest[n_extra + 2:2 * n_extra + 2], rest[2 * n_extra + 2:])
    _run_head_merges(extra_in[:n_caches], extra_out[:n_caches], n_cache_blocks, n_axes=3)
    _run_side_casts(extra_in[n_caches:], extra_out[n_caches:])
    l = pl.program_id(2)
    n_sub = chunk // SUB
    width = q_ref.shape[1]
    n_heads = width // dk
    heads = [slice(h * dk, (h + 1) * dk) for h in range(n_heads)]

    @pl.when(l == 0)
    def _():
        for h in range(n_heads):
            st_ref[h] = s0_ref[0, h].T

    q = q_ref[...].astype(F32)
    lf = lf_ref[...]
    kk = 1.0 - jnp.exp(lf)
    vb = v_ref[...]
    k_ref[...] = kk

    p0 = lf.astype(BF16)
    r1 = lf - p0.astype(F32)
    p1 = r1.astype(BF16)
    p2 = (r1 - p1.astype(F32)).astype(BF16)
    cum = cum_ref[...]
    b = jnp.concatenate(
        [_dot(cum, jnp.concatenate([p[c * chunk:(c + 1) * chunk, :] for p in (p0, p1, p2)], axis=0))
         for c in range(n_chunks)], axis=0)
    b_ref[...] = b
    tl = n_chunks * chunk

    def rows_of(ref, group, offset):
        return jnp.concatenate(
            [jnp.broadcast_to(ref[pl.ds(g * group + offset, 1), :], (group, width)) for g in range(tl // group)],
            axis=0)

    b_end = rows_of(b_ref, SUB, SUB - 1)
    b_mid = rows_of(b_ref, SUB, HALF - 1)
    b_last = rows_of(b_ref, chunk, chunk - 1)

    qe = (q * jnp.exp(b)).astype(BF16)
    k_dec = (kk * jnp.exp(b_end - b)).astype(BF16)
    k_end = (kk * jnp.exp(b_last - b)).astype(BF16)
    decay = [jnp.exp(b_ref[pl.ds((c + 1) * chunk - 1, 1), :]) for c in range(n_chunks)]
    q_dec = [(q * jnp.exp(jnp.minimum(b - rows_of(b_ref, chunk, (j + 1) * SUB - 1), 0.0))).astype(BF16)
             for j in range(n_sub - 1)]
    q_mid = (q * jnp.exp(jnp.minimum(b - b_mid, 0.0))).astype(BF16)
    k_mid = (kk * jnp.exp(jnp.minimum(b_mid - b, 0.0))).astype(BF16)

    w = [(q * jnp.exp(jnp.minimum(b - rows_of(b_ref, HALF, u), 0.0)) * rows_of(k_ref, HALF, u)).astype(BF16)
         for u in range(HALF)]
    sel = sel_ref[...]
    diag = [_dot(jnp.concatenate([wu[:, hc] for wu in w], axis=1), sel) for hc in heads]

    row = lax.broadcasted_iota(jnp.int32, (chunk, chunk), 0)
    col = lax.broadcasted_iota(jnp.int32, (chunk, chunk), 1)
    row_blk = row // SUB
    col_blk = col // SUB
    mid_mask = (col_blk == row_blk) & (row % SUB >= HALF) & (col % SUB < HALF)
    diag_mask = (col // HALF == row // HALF) & (col <= row)
    eye = (lax.broadcasted_iota(jnp.int32, (dk, dk), 0)
           == lax.broadcasted_iota(jnp.int32, (dk, dk), 1)).astype(F32).astype(BF16)

    chunks = [slice(c * chunk, (c + 1) * chunk) for c in range(n_chunks)]
    units = [(h, hc, c, rows) for h, hc in enumerate(heads) for c, rows in enumerate(chunks)]
    cross, v_t = [], []
    for h, hc, c, rows in units:
        qd = jnp.concatenate([qj[rows, hc] for qj in q_dec] + [q_mid[rows, hc]], axis=0)
        kd = jnp.concatenate([k_dec[rows, hc], k_mid[rows, hc]], axis=0)
        cross.append(_dot_nt(qd, kd))
        v_t.append(_dot_nt(eye, vb[rows, hc]).astype(BF16))

    intra = []
    for i, (h, hc, c, rows) in enumerate(units):
        m = cross[i]
        attn = jnp.where(mid_mask, m[(n_sub - 1) * chunk:, chunk:], 0.0)
        for j in range(n_sub - 1):
            attn = jnp.where((col_blk == j) & (row_blk > j), m[j * chunk:(j + 1) * chunk, :chunk], attn)
        attn = jnp.where(diag_mask, diag[h][rows, :], attn)
        intra.append(_dot(attn.astype(BF16), vb[rows, hc]))
        u_ref[i] = _dot(v_t[i], k_end[rows, hc])

    st = [st_ref[h] for h in range(n_heads)]
    for i, (h, hc, c, rows) in enumerate(units):
        sb_ref[i] = st[h].astype(BF16)
        st[h] = st[h] * decay[c][:, hc] + u_ref[i]
    for h in range(n_heads):
        st_ref[h] = st[h]

    inter = [_dot_nt(qe[rows, hc], sb_ref[i]) for i, (h, hc, c, rows) in enumerate(units)]
    gn = gn_ref[...]
    for h, hc in enumerate(heads):
        per_chunk = range(h * n_chunks, (h + 1) * n_chunks)
        o = jnp.concatenate([intra[i] + inter[i] for i in per_chunk], axis=0)
        ms = jnp.mean(o * o, axis=-1, keepdims=True)
        o = o * lax.rsqrt(ms + EPS) * gn[:, hc] * og_ref[:, hc].astype(F32)
        o_ref[:, hc] = o.astype(o_ref.dtype)

    @pl.when(l == pl.num_programs(2) - 1)
    def _():
        for h in range(n_heads):
            sfin_ref[0, h] = st[h].T


HGRN_UNITS_PER_STEP = 32


def hgrn2(pf, pb, gn, s0, layer, *, bsz, seq, heads, dk, tl, caches=(), side=()):
    chunk = min(CHUNK, seq)
    assert seq % tl == 0 and tl % chunk == 0 and chunk % SUB == 0
    nl = seq // tl
    d_a = heads * dk
    n_chunks = tl // chunk
    hb = max(1, min(heads, HGRN_UNITS_PER_STEP // n_chunks))
    while heads % hb:
        hb -= 1
    n_groups = heads // hb
    width = hb * dk
    n_units = hb * n_chunks

    def sec(k):
        return pl.BlockSpec((tl, width), lambda b, h, l, k=k: (b * nl + l, k * n_groups + h))

    cum, sel = _hgrn_consts(chunk, dk)
    grid = (bsz, n_groups, nl)
    if caches and _head_merge_blocks(caches[0]) > math.prod(grid):
        o_a, s_fin, *casts = hgrn2(pf, pb, gn, s0, layer, bsz=bsz, seq=seq, heads=heads, dk=dk, tl=tl, side=side)
        return (o_a, s_fin, *[c.reshape(*c.shape[:3], -1) for c in caches], *casts)
    cache_in, cache_out, cache_shapes, cache_vmem = _head_merge_specs(caches, grid)
    side_in, side_out, side_shapes, side_vmem = _side_cast_specs(side, grid, flat=True)
    cache_in, cache_out, cache_shapes = cache_in + side_in, cache_out + side_out, cache_shapes + side_shapes
    est = (2 * tl * width * (4 + 3 * 2) + 2 * tl * width * 2 + 6 * hb * dk * dk * 4 + 2 * tl * width * 4
           + n_units * dk * dk * 6 + 2 * (cum.size + sel.size) * 2 + 24 * tl * width * 4
           + cache_vmem + side_vmem)
    kern = functools.partial(_hgrn_kernel, chunk=chunk, n_chunks=n_chunks, dk=dk, n_caches=len(caches),
                             n_cache_blocks=_head_merge_blocks(caches[0]) if caches else 0, n_side=len(side))
    return pl.pallas_call(
        kern,
        grid=grid,
        in_specs=[
            sec(IN_SEC_Q - IN_F32_SECTIONS), sec(IN_SEC_LOGF),
            sec(IN_SEC_V - IN_F32_SECTIONS), sec(IN_SEC_OG - IN_F32_SECTIONS),
            pl.BlockSpec((1, width), lambda b, h, l: (0, h)),
            pl.BlockSpec((None, 1, hb, dk, dk), lambda b, h, l: (layer, b, h, 0, 0)),
            pl.BlockSpec(cum.shape, lambda b, h, l: (0, 0)),
            pl.BlockSpec(sel.shape, lambda b, h, l: (0, 0)),
            *cache_in,
        ],
        out_specs=[
            pl.BlockSpec((tl, width), lambda b, h, l: (b * nl + l, h)),
            pl.BlockSpec((1, hb, dk, dk), lambda b, h, l: (b, h, 0, 0)),
            *cache_out,
        ],
        out_shape=[
            jax.ShapeDtypeStruct((bsz * seq, d_a), BF16),
            jax.ShapeDtypeStruct((bsz, heads, dk, dk), F32),
            *cache_shapes,
        ],
        scratch_shapes=[
            pltpu.VMEM((hb, dk, dk), F32),
            pltpu.VMEM((tl, width), F32),
            pltpu.VMEM((tl, width), F32),
            pltpu.VMEM((n_units, dk, dk), F32),
            pltpu.VMEM((n_units, dk, dk), BF16),
        ],
        compiler_params=_params(("parallel", "parallel", "arbitrary"), est),
        name="hgrn2",
    )(pb, pf, pb, pb, gn, s0, cum, sel, *caches, *[w_ for w_, _ in side])


def _lru_kernel(rx_ref, gy_ref, cw_ref, cb_ref, wax_ref, ba_ref, bx_ref, lam_ref, h0_ref, buf_ref,
                *rest, tl, n_blocks, bw, n_side):
    side_in, (o_ref, hlast_ref), side_out, (xp_ref, a_ref, u_ref, hs_ref, ps_ref, h_ref) = (
        rest[:n_side], rest[n_side:n_side + 2], rest[n_side + 2:2 * n_side + 2], rest[2 * n_side + 2:])
    _run_side_casts(side_in, side_out)
    l = pl.program_id(1)
    keep = CONV_W - 1

    @pl.when(l == 0)
    def _():
        xp_ref[CONV_PAD - keep:CONV_PAD, :] = buf_ref[0]
        h_ref[...] = h0_ref[0]

    x = rx_ref[...]
    xp_ref[CONV_PAD:CONV_PAD + tl, :] = x
    cw = cw_ref[...]
    xc = xp_ref[CONV_PAD - keep:CONV_PAD - keep + tl, :] * cw[0:1, :]
    for j in range(1, CONV_W):
        xc = xc + xp_ref[CONV_PAD - keep + j:CONV_PAD - keep + j + tl, :] * cw[j:j + 1, :]
    xc = xc + cb_ref[...]
    xp_ref[CONV_PAD - keep:CONV_PAD, :] = xp_ref[CONV_PAD + tl - keep:CONV_PAD + tl, :]

    xcb = xc.astype(BF16)
    pre = [_dot(xcb[:, n * bw:(n + 1) * bw], wax_ref[n]) for n in range(n_blocks)]
    r = jax.nn.sigmoid(jnp.concatenate([pn[:, :bw] for pn in pre], axis=-1) + ba_ref[...])
    ig = jax.nn.sigmoid(jnp.concatenate([pn[:, bw:] for pn in pre], axis=-1) + bx_ref[...])
    lam = lam_ref[...]
    softplus_neg = jnp.maximum(-lam, 0.0) + jnp.log1p(jnp.exp(-jnp.abs(lam)))
    log_a = -LRU_C * r * softplus_neg
    a = jnp.exp(log_a)
    mult = jnp.sqrt(jnp.maximum(-jnp.tanh(log_a) * (a * a + 1.0), 0.0))
    a_ref[...] = a
    u_ref[...] = mult * (ig * xc)

    seg = tl // LRU_SEGMENTS
    one = jnp.ones_like(h_ref[...])

    def step(t, carry):
        hs, ps = carry
        new_h, new_p = [], []
        for s in range(LRU_SEGMENTS):
            r = s * seg + t
            a_t = a_ref[pl.ds(r, 1), :]
            h_s = a_t * hs[s] + u_ref[pl.ds(r, 1), :]
            hs_ref[pl.ds(r, 1), :] = h_s
            new_h.append(h_s)
            if s > 0:
                p_s = a_t * ps[s - 1]
                ps_ref[pl.ds(r, 1), :] = p_s
                new_p.append(p_s)
        return tuple(new_h), tuple(new_p)

    init = ((h_ref[...],) + (jnp.zeros_like(one),) * (LRU_SEGMENTS - 1), (one,) * (LRU_SEGMENTS - 1))
    hs, ps = lax.fori_loop(0, seg, step, init, unroll=min(seg, 4))
    h = hs[0]
    gy = gy_ref[...].astype(F32)
    o_ref[0:seg, :] = (hs_ref[0:seg, :] * gy[0:seg, :]).astype(o_ref.dtype)
    for s in range(1, LRU_SEGMENTS):
        rows = slice(s * seg, (s + 1) * seg)
        o_ref[rows, :] = ((hs_ref[rows, :] + ps_ref[rows, :] * h) * gy[rows, :]).astype(o_ref.dtype)
        h = hs[s] + ps[s - 1] * h
    h_ref[...] = h

    @pl.when(l == pl.num_programs(1) - 1)
    def _():
        hlast_ref[0] = h


def conv_lru(pf, pb, cw, cb, wax, ba, bx, lam, h0, buf, layer, *, bsz, seq, d, tl, side=()):
    assert seq % tl == 0 and tl >= CONV_W - 1 and tl % LRU_SEGMENTS == 0
    nl = seq // tl
    n_blocks, bw = wax.shape[0], wax.shape[1]
    vec = pl.BlockSpec((1, d), lambda b, l: (0, 0))
    side_in, side_out, side_shapes, side_vmem = _side_cast_specs(side, (bsz, nl), flat=True)
    est = (2 * tl * d * (4 + 2) + 2 * tl * d * 2 + (3 * tl + CONV_PAD) * d * 4 + 8 * tl * d * 4
           + side_vmem)
    kern = functools.partial(_lru_kernel, tl=tl, n_blocks=n_blocks, bw=bw, n_side=len(side))
    return pl.pallas_call(
        kern,
        grid=(bsz, nl),
        in_specs=[
            pl.BlockSpec((tl, d), lambda b, l: (b * nl + l, IN_SEC_RX)),
            pl.BlockSpec((tl, d), lambda b, l: (b * nl + l, IN_SEC_RY - IN_F32_SECTIONS)),
            pl.BlockSpec((CONV_W, d), lambda b, l: (0, 0)),
            vec,
            pl.BlockSpec((n_blocks, bw, 2 * bw), lambda b, l: (0, 0, 0)),
            vec, vec, vec,
            pl.BlockSpec((None, 1, 1, d), lambda b, l: (layer, b, 0, 0)),
            pl.BlockSpec((None, 1, CONV_W - 1, d), lambda b, l: (layer, b, 0, 0)),
            *side_in,
        ],
        out_specs=[
            pl.BlockSpec((tl, d), lambda b, l: (b * nl + l, 0)),
            pl.BlockSpec((1, 1, d), lambda b, l: (b, 0, 0)),
            *side_out,
        ],
        out_shape=[
            jax.ShapeDtypeStruct((bsz * seq, d), BF16),
            jax.ShapeDtypeStruct((bsz, 1, d), F32),
            *side_shapes,
        ],
        scratch_shapes=[
            pltpu.VMEM((CONV_PAD + tl, d), F32),
            pltpu.VMEM((tl, d), F32),
            pltpu.VMEM((tl, d), F32),
            pltpu.VMEM((tl, d), F32),
            pltpu.VMEM((tl, d), F32),
            pltpu.VMEM((1, d), F32),
        ],
        compiler_params=_params(("parallel", "arbitrary"), est),
        name="conv_lru",
    )(pf, pb, cw, cb, wax, ba, bx, lam, h0, buf, *[w_ for w_, _ in side])


def _mem_attn_kernel(q_ref, k_ref, v_ref, o_ref, *, scale, heads, hd):
    cols = [slice(h * hd, (h + 1) * hd) for h in range(heads)]
    scores = [_dot_nt(q_ref[:, c], k_ref[0, :, c].astype(BF16)) * scale for c in cols]
    probs = [jnp.exp(s - jnp.max(s, axis=-1, keepdims=True)) for s in scores]
    outs = [_dot(p.astype(BF16), v_ref[0, :, c].astype(BF16)) for p, c in zip(probs, cols)]
    for p, o, c in zip(probs, outs, cols):
        o_ref[:, c] = (o / jnp.sum(p, axis=-1, keepdims=True)).astype(o_ref.dtype)


def mem_attn(pb, mem_k, mem_v, layer, *, bsz, seq, heads, hd, col0, k_col, v_col, tl):
    assert seq % tl == 0
    nl = seq // tl
    n_mem = mem_k.shape[2]
    d_c = heads * hd
    est = 4 * tl * d_c * 2 + 4 * n_mem * d_c * 4 + 6 * tl * n_mem * 4
    kern = functools.partial(_mem_attn_kernel, scale=1.0 / math.sqrt(hd), heads=heads, hd=hd)
    return pl.pallas_call(
        kern,
        grid=(bsz, nl),
        in_specs=[
            pl.BlockSpec((tl, d_c), lambda b, l: (b * nl + l, col0)),
            pl.BlockSpec((None, 1, n_mem, d_c), lambda b, l: (layer, b, 0, k_col)),
            pl.BlockSpec((None, 1, n_mem, d_c), lambda b, l: (layer, b, 0, v_col)),
        ],
        out_specs=pl.BlockSpec((tl, d_c), lambda b, l: (b * nl + l, 0)),
        out_shape=jax.ShapeDtypeStruct((bsz * seq, d_c), BF16),
        compiler_params=_params(("parallel", "parallel"), est),
        name="mem_attn",
    )(pb, mem_k, mem_v)


def _merge_kernel(x_ref, oa_ref, ob_ref, oc_ref, g0_ref, g1_ref, g2_ref,
                  wa_ref, wb_ref, wc_ref, wo_ref, gn_ref, *rest, n_side):
    side_in, y_ref, side_out = rest[:n_side], rest[n_side], rest[n_side + 1:]
    _run_side_casts(side_in, side_out)
    m = g0_ref[...].astype(F32) * _dot(oa_ref[...], wa_ref[...])
    m = m + g1_ref[...].astype(F32) * _dot(ob_ref[...], wb_ref[...])
    m = m + g2_ref[...].astype(F32) * _dot(oc_ref[...], wc_ref[...])
    z = _dot(m.astype(BF16), wo_ref[...])
    y_ref[...] = x_ref[...] + _rms(z, gn_ref[...])


def merge(x, oa, ob, oc, pb, wa, wb, wc, wo, gn, *, col_gates, tm, side=()):
    t, d = x.shape
    db = oa.shape[1]
    assert t % tm == 0
    row = lambda i: (i, 0)
    const = lambda i: (0, 0)
    wspec = lambda rows: pl.BlockSpec((None, rows, d), lambda i: (0, 0, 0), pipeline_mode=pl.Buffered(1))
    gate_specs = [pl.BlockSpec((tm, d), lambda i, k=k: (i, col_gates + k)) for k in range(N_GATES)]
    side_in, side_out, side_shapes, side_vmem = _side_cast_specs(side, (t // tm,))
    est = (4 * tm * d * 4 + 6 * tm * db * 2 + 6 * tm * d * 2
           + 3 * db * d * 2 + d * d * 2 + 6 * tm * d * 4 + side_vmem)
    return pl.pallas_call(
        functools.partial(_merge_kernel, n_side=len(side)),
        grid=(t // tm,),
        in_specs=[
            pl.BlockSpec((tm, d), row),
            pl.BlockSpec((tm, db), row), pl.BlockSpec((tm, db), row), pl.BlockSpec((tm, db), row),
            *gate_specs,
            wspec(db), wspec(db), wspec(db), wspec(d),
            pl.BlockSpec((1, d), const),
            *side_in,
        ],
        out_specs=[pl.BlockSpec((tm, d), row), *side_out],
        out_shape=[jax.ShapeDtypeStruct((t, d), F32), *side_shapes],
        compiler_params=_params(("parallel",), est),
        name="merge",
    )(x, oa, ob, oc, pb, pb, pb, wa, wb, wc, wo, gn, *[w_ for w_, _ in side])


def _ffn_kernel(x_ref, gpre_ref, wg_ref, wu_ref, wd_ref, gpost_ref, y_ref, h_ref, acc_ref):
    j = pl.program_id(1)

    @pl.when(j == 0)
    def _():
        h_ref[...] = _rms(x_ref[...], gpre_ref[...]).astype(BF16)
        acc_ref[...] = jnp.zeros_like(acc_ref)

    h = h_ref[...]
    gt = _dot(h, wg_ref[...])
    up = _dot(h, wu_ref[...])
    act = (jax.nn.silu(gt) * up).astype(BF16)
    acc_ref[...] += _dot(act, wd_ref[...])

    @pl.when(j == pl.num_programs(1) - 1)
    def _():
        y_ref[...] = x_ref[...] + _rms(acc_ref[...], gpost_ref[...])


def ffn(x, gpre, w_gu, w_down, gpost, *, tm, tf):
    t, d = x.shape
    d_ff = w_down.shape[1]
    assert t % tm == 0 and d_ff % tf == 0
    nf = d_ff // tf
    est = 4 * tm * d * 4 + 2 * 3 * d * tf * 2 + tm * d * 2 + tm * d * 4 + 4 * tm * tf * 4
    return pl.pallas_call(
        _ffn_kernel,
        grid=(t // tm, nf),
        in_specs=[
            pl.BlockSpec((tm, d), lambda i, j: (i, 0)),
            pl.BlockSpec((1, d), lambda i, j: (0, 0)),
            pl.BlockSpec((None, d, tf), lambda i, j: (0, 0, j)),
            pl.BlockSpec((None, d, tf), lambda i, j: (0, 0, nf + j)),
            pl.BlockSpec((None, tf, d), lambda i, j: (0, j, 0)),
            pl.BlockSpec((1, d), lambda i, j: (0, 0)),
        ],
        out_specs=pl.BlockSpec((tm, d), lambda i, j: (i, 0)),
        out_shape=jax.ShapeDtypeStruct((t, d), F32),
        scratch_shapes=[pltpu.VMEM((tm, d), BF16), pltpu.VMEM((tm, d), F32)],
        compiler_params=_params(("parallel", "arbitrary"), est),
        name="ffn",
    )(x, gpre, w_gu, w_gu, w_down, gpost)


def _row_tile(n, target):
    t = min(n, target)
    while n % t:
        t //= 2
    return t


IN_PROJ_ROWS, IN_PROJ_COLS, IN_PROJ_EPILOGUE_ROWS = 1024, 1024, 256
HGRN_ROWS = 2048
LRU_ROWS = 512
MEM_ATTN_ROWS = 1024
MERGE_ROWS = 256
FFN_ROWS, FFN_COLS = 512, 512
MEM_KV_ROWS, MEM_KV_COLS = 512, 1024


NEXT_LAYER_HOSTS = {
    "in_proj": ("w_in",),
    "conv_lru": ("ffn_w_gu", "ffn_w_down"),
    "merge": ("w_branch_a", "w_branch_b", "w_branch_c", "w_out", "mem_w_kv"),
}
SAME_LAYER_HOSTS = {
    "hgrn2": ("w_branch_a", "w_branch_b", "w_branch_c", "w_out", "mem_w_kv"),
    "conv_lru": ("ffn_w_gu", "ffn_w_down"),
}


def _trunk_layer(x2, bsz, seq, mem, state, lb, w, p, jobs=None, caches=()):
    t, d = x2.shape
    s_hg, h_lru, conv_buf, state_layer = state
    heads, dk = s_hg.shape[2], s_hg.shape[3]
    d_a = heads * dk
    d_b = h_lru.shape[-1]
    mem_heads, hd = p["mem_heads"], p["mem_hd"]
    if seq < CONV_W - 1:
        raise NotImplementedError("sequence shorter than the conv history")
    jobs = jobs or {}
    w = dict(w)
    w_next = {}

    def side(host):
        return [(a, layer) for _, a, layer, _ in jobs.get(host, ())]

    def keep(host, outs):
        for (name, _, _, for_next), o in zip(jobs.get(host, ()), outs):
            (w_next if for_next else w)[name] = o

    pf, pb, *extra = in_proj(x2, p["norm_pre_mix"], w["w_in"], lb, p["b_gate"], sec=d_a,
                             tm=_row_tile(t, IN_PROJ_ROWS), tn=IN_PROJ_COLS, side=side("in_proj"))
    keep("in_proj", extra)
    n_bf_sections = pb.shape[1] // d_a

    o_a, s_new, *extra = hgrn2(pf, pb, p["hgrn_out_norm"], s_hg, state_layer, bsz=bsz, seq=seq, heads=heads,
                               dk=dk, tl=_row_tile(seq, HGRN_ROWS), caches=caches, side=side("hgrn2"))
    merged_caches = extra[:len(caches)]
    keep("hgrn2", extra[len(caches):])
    o_b, h_last, *extra = conv_lru(pf, pb, p["conv_w"], p["conv_b"], p["lru_wax"], p["lru_ba"], p["lru_bx"],
                                   p["lru_lambda"], h_lru, conv_buf, state_layer, bsz=bsz, seq=seq, d=d_b,
                                   tl=_row_tile(seq, LRU_ROWS), side=side("conv_lru"))
    keep("conv_lru", extra)
    mem_k, mem_v, mem_layer, k_col, v_col = mem(w)
    o_c = mem_attn(pb, mem_k, mem_v, mem_layer, bsz=bsz, seq=seq, heads=mem_heads, hd=hd,
                   col0=n_bf_sections - 1, k_col=k_col, v_col=v_col, tl=_row_tile(seq, MEM_ATTN_ROWS))

    gate_col = (IN_SEC_GATES - IN_F32_SECTIONS) * d_a
    assert gate_col % d == 0
    x2, *extra = merge(x2, o_a, o_b, o_c, pb, w["w_branch_a"], w["w_branch_b"], w["w_branch_c"], w["w_out"],
                       p["norm_post_mix"], col_gates=gate_col // d, tm=_row_tile(t, MERGE_ROWS),
                       side=side("merge"))
    keep("merge", extra)
    x2 = ffn(x2, p["norm_pre_ffn"], w["ffn_w_gu"], w["ffn_w_down"], p["norm_post_ffn"],
             tm=_row_tile(t, FFN_ROWS), tf=FFN_COLS)

    rx_tail = pf.reshape(bsz, seq, -1)[:, seq - (CONV_W - 1):, IN_SEC_RX * d_a:(IN_SEC_RX + 1) * d_a]
    return x2, s_new, h_last.reshape(bsz, d_b), rx_tail, w, w_next, merged_caches


def kernel(x_prompt, x_sample, state_hgrn, state_lru, state_conv, cache_mem_k, cache_mem_v, mem_prompt, norm_mem, mem_w_kv, hgrn_lower_bound, norm_pre_mix, w_in, b_gate, hgrn_out_norm, conv_w, conv_b, lru_wa, lru_ba, lru_wx, lru_bx, lru_lambda, w_branch_a, w_branch_b, w_branch_c, w_out, norm_post_mix, norm_pre_ffn, ffn_w_gu, ffn_w_down, norm_post_ffn):
    depth = w_in.shape[0]
    bp, sp, d = x_prompt.shape
    bs, ss, _ = x_sample.shape
    _, _, heads, dk, dv = state_hgrn.shape
    d_a = heads * dk
    d_b = state_lru.shape[-1]
    n_mem, mem_heads, hd = cache_mem_k.shape[2:]
    d_c = mem_heads * hd
    assert dk == dv and d_b == d_a and d_c == d_a and d == 2 * d_a
    assert b_gate.shape[1] == N_GATES * d

    sm = jax.nn.softmax(hgrn_lower_bound.astype(F32), axis=0)
    lbs = jnp.cumsum(sm, axis=0) - sm[0:1]

    xp = x_prompt.reshape(bp * sp, d)
    xs = x_sample.reshape(bs * ss, d)
    mem2 = mem_prompt.reshape(bp * n_mem, d)
    zero_state = (jnp.zeros((1, bp, heads, dk, dv), F32), jnp.zeros((1, bp, 1, d_b), F32),
                  jnp.zeros((1, bp, CONV_W - 1, d_b), F32), 0)
    lru_s4 = state_lru.reshape(depth, bs, 1, d_b)

    w_f32 = dict(w_in=w_in, w_branch_a=w_branch_a, w_branch_b=w_branch_b, w_branch_c=w_branch_c, w_out=w_out,
                 ffn_w_gu=ffn_w_gu, ffn_w_down=ffn_w_down, mem_w_kv=mem_w_kv)
    assert set(w_f32) == {name for names in NEXT_LAYER_HOSTS.values() for name in names}
    assert set(w_f32) == {"w_in"} | {name for names in SAME_LAYER_HOSTS.values() for name in names}
    w = {"w_in": cast_bf16(w_in, 0)}

    outs = {k: [] for k in ("hg_p", "lru_p", "conv_p", "mk_p", "mv_p", "hg_s", "lru_s", "conv_s")}
    for l in range(depth):
        row = lambda a: a[l].reshape(1, -1)
        p = dict(
            mem_heads=mem_heads, mem_hd=hd,
            norm_pre_mix=row(norm_pre_mix), b_gate=row(b_gate),
            hgrn_out_norm=row(hgrn_out_norm), conv_w=conv_w[l], conv_b=row(conv_b),
            lru_wax=jnp.concatenate([lru_wa[l], lru_wx[l]], axis=-1).astype(BF16),
            lru_ba=row(lru_ba), lru_bx=row(lru_bx), lru_lambda=row(lru_lambda),
            norm_post_mix=row(norm_post_mix), norm_pre_ffn=row(norm_pre_ffn),
            norm_post_ffn=row(norm_post_ffn),
        )
        lb = lbs[l].reshape(1, -1)

        prompt_kv = []

        def prompt_mem(w_now):
            kv = norm_matmul(mem2, row(norm_mem), w_now["mem_w_kv"], tm=_row_tile(bp * n_mem, MEM_KV_ROWS),
                             tn=MEM_KV_COLS)
            prompt_kv.append(kv.reshape(1, bp, n_mem, 2 * d_c))
            return prompt_kv[0], prompt_kv[0], 0, 0, 1

        jobs = {}
        if l == 0:
            for host, names in SAME_LAYER_HOSTS.items():
                jobs.setdefault(host, []).extend((name, w_f32[name], 0, False) for name in names)
        if l + 1 < depth:
            for host, names in NEXT_LAYER_HOSTS.items():
                jobs.setdefault(host, []).extend((name, w_f32[name], l + 1, True) for name in names)
        caches = (cache_mem_k, cache_mem_v) if l == 0 else ()
        xp, s1, h1, c1, w, w_next, merged = _trunk_layer(xp, bp, sp, prompt_mem, zero_state, lb, w, p,
                                                         jobs, caches)
        if l == 0:
            cache_k, cache_v = merged
        kv4 = prompt_kv[0]
        outs["hg_p"].append(s1); outs["lru_p"].append(h1); outs["conv_p"].append(c1)
        outs["mk_p"].append(kv4[0, :, :, :d_c].reshape(bp, n_mem, mem_heads, hd))
        outs["mv_p"].append(kv4[0, :, :, d_c:].reshape(bp, n_mem, mem_heads, hd))

        xs, s2, h2, c2, _, _, _ = _trunk_layer(xs, bs, ss, lambda _: (cache_k, cache_v, l, 0, 0),
                                               (state_hgrn, lru_s4, state_conv, l), lb, w, p)
        outs["hg_s"].append(s2); outs["lru_s"].append(h2); outs["conv_s"].append(c2)
        w = w_next

    st = {k: jnp.stack(v) for k, v in outs.items()}
    return (xp.reshape(bp, sp, d), xs.reshape(bs, ss, d), st["hg_p"], st["lru_p"], st["conv_p"],
            st["mk_p"], st["mv_p"], st["hg_s"], st["lru_s"], st["conv_s"])
```

```python
import functools
import math

import numpy as np
import jax
import jax.numpy as jnp
from jax import lax
from jax.experimental import pallas as pl
from jax.experimental.pallas import tpu as pltpu

F32 = jnp.float32
BF16 = jnp.bfloat16

EPS = 1e-6
LRU_C = 8.0
CHUNK = 64
SUB = 16
HALF = 8
N_PIECES = 3
CONV_W = 4
CONV_PAD = 8
LRU_SEGMENTS = 4
N_GATES = 3

LANES = 128
F32_SUBLANES = 8
BF16_SUBLANES = 16

V7X_VMEM_BYTES = 64 * 1024 * 1024
VMEM_LIMIT_CAP = V7X_VMEM_BYTES - 8 * 1024 * 1024


def _vmem_limit(estimate_bytes):
    return int(min(VMEM_LIMIT_CAP, max(16 * 1024 * 1024, estimate_bytes * 5 // 4)))


def _params(sem, vmem_estimate):
    return pltpu.CompilerParams(dimension_semantics=sem, vmem_limit_bytes=_vmem_limit(vmem_estimate))


def _rms(x, g):
    ms = jnp.mean(x * x, axis=-1, keepdims=True)
    return x * lax.rsqrt(ms + EPS) * g


def _dot(a, b):
    return jnp.dot(a, b, preferred_element_type=F32)


def _dot_nt(a, b):
    return lax.dot_general(a, b, (((1,), (1,)), ((), ())), preferred_element_type=F32)


def _cast_kernel(w_ref, o_ref):
    o_ref[...] = w_ref[...].astype(o_ref.dtype)


CAST_BLOCK_BYTES = 4 * 1024 * 1024


def cast_bf16(w, layer):
    _, r, c = w.shape
    tr = r
    while tr * c * 4 > CAST_BLOCK_BYTES and tr % (2 * BF16_SUBLANES) == 0:
        tr //= 2
    return pl.pallas_call(
        _cast_kernel,
        grid=(r // tr,),
        in_specs=[pl.BlockSpec((None, tr, c), lambda i: (layer, i, 0))],
        out_specs=pl.BlockSpec((None, tr, c), lambda i: (0, i, 0)),
        out_shape=jax.ShapeDtypeStruct((1, r, c), BF16),
        compiler_params=_params(("parallel",), 2 * tr * c * 6),
        name="cast_bf16",
    )(w)


def _flat_step(grid, ids):
    g = 0
    for n, i in zip(grid, ids):
        g = g * n + i
    return g


def _side_cast_specs(side, grid, flat=False):
    in_specs, out_specs, out_shapes, vmem = [], [], [], 0
    n_i = math.prod(grid) if flat else grid[0]
    n_j = 1 if flat or len(grid) < 2 else grid[1]
    for w, layer, col_tile in side:
        _, r, c = w.shape
        assert r % n_i == 0 and (r // n_i) % BF16_SUBLANES == 0
        br = r // n_i
        split = n_j > 1 and c % n_j == 0 and (c // n_j) % LANES == 0
        bc = c // n_j if split else c

        def block(ids, split=split):
            return (_flat_step(grid, ids), 0) if flat else (ids[0], ids[1] if split else 0)

        in_specs.append(pl.BlockSpec((None, br, bc), lambda *ids, layer=layer, block=block: (layer, *block(ids))))
        if col_tile is None:
            out_specs.append(pl.BlockSpec((None, br, bc), lambda *ids, block=block: (0, *block(ids))))
            out_shapes.append(jax.ShapeDtypeStruct((1, r, c), BF16))
        else:
            assert flat and c % col_tile == 0 and col_tile % LANES == 0
            out_specs.append(pl.BlockSpec((None, c // col_tile, br, col_tile),
                                          lambda *ids: (0, 0, _flat_step(grid, ids), 0)))
            out_shapes.append(jax.ShapeDtypeStruct((1, c // col_tile, r, col_tile), BF16))
        vmem += 2 * br * bc * 6
    return in_specs, out_specs, out_shapes, vmem


def _run_side_casts(side_in, side_out):
    for w_ref, o_ref in zip(side_in, side_out):
        if len(o_ref.shape) == 2:
            o_ref[...] = w_ref[...].astype(o_ref.dtype)
        else:
            n_tiles, _, ct = o_ref.shape
            for t in range(n_tiles):
                o_ref[t] = w_ref[:, t * ct:(t + 1) * ct].astype(o_ref.dtype)


HEAD_MERGE_ROWS = 128


def _head_merge_blocks(cache):
    depth, bsz, n_mem = cache.shape[:3]
    return depth * bsz * (n_mem // min(HEAD_MERGE_ROWS, n_mem))


def _head_merge_specs(caches, grid):
    in_specs, out_specs, out_shapes, vmem = [], [], [], 0
    for c in caches:
        depth, bsz, n_mem, heads, hd = c.shape
        rows = min(HEAD_MERGE_ROWS, n_mem)
        assert n_mem % rows == 0
        per_b = n_mem // rows
        n_blocks = _head_merge_blocks(c)
        assert n_blocks <= math.prod(grid)

        def where(ids, bsz=bsz, per_b=per_b, n_blocks=n_blocks):
            g = jnp.minimum(_flat_step(grid, ids), n_blocks - 1)
            return g // (bsz * per_b), (g // per_b) % bsz, g % per_b

        in_specs.append(pl.BlockSpec((None, 1, rows, heads, hd), lambda *ids, where=where: (*where(ids), 0, 0)))
        out_specs.append(pl.BlockSpec((None, 1, rows, heads * hd), lambda *ids, where=where: (*where(ids), 0)))
        out_shapes.append(jax.ShapeDtypeStruct((depth, bsz, n_mem, heads * hd), c.dtype))
        vmem += 2 * rows * (F32_SUBLANES * hd + heads * hd) * 4
    return in_specs, out_specs, out_shapes, vmem


def _run_head_merges(ins, outs, n_blocks, n_axes):
    if not ins:
        return
    step = _flat_step([pl.num_programs(a) for a in range(n_axes)], [pl.program_id(a) for a in range(n_axes)])

    @pl.when(step < n_blocks)
    def _():
        for c_ref, o_ref in zip(ins, outs):
            heads, hd = c_ref.shape[2], c_ref.shape[3]
            for h in range(heads):
                o_ref[0, :, h * hd:(h + 1) * hd] = c_ref[0, :, h, :]


def _norm_matmul_kernel(x_ref, g_ref, w_ref, o_ref, h_ref):
    @pl.when(pl.program_id(1) == 0)
    def _():
        h_ref[...] = _rms(x_ref[...], g_ref[...]).astype(BF16)

    o_ref[...] = _dot(h_ref[...], w_ref[...]).astype(o_ref.dtype)


def norm_matmul(x, g, w, *, tm, tn, out_dtype=F32):
    t, d = x.shape
    n = w.shape[2]
    assert t % tm == 0 and n % tn == 0
    est = 2 * tm * d * 4 + 2 * d * tn * 2 + 2 * tm * tn * 4 + tm * d * 2
    return pl.pallas_call(
        _norm_matmul_kernel,
        grid=(t // tm, n // tn),
        in_specs=[
            pl.BlockSpec((tm, d), lambda i, j: (i, 0)),
            pl.BlockSpec((1, d), lambda i, j: (0, 0)),
            pl.BlockSpec((None, d, tn), lambda i, j: (0, 0, j)),
        ],
        out_specs=pl.BlockSpec((tm, tn), lambda i, j: (i, j)),
        out_shape=jax.ShapeDtypeStruct((t, n), out_dtype),
        scratch_shapes=[pltpu.VMEM((tm, d), BF16)],
        compiler_params=_params(("parallel", "arbitrary"), est),
        name="norm_matmul",
    )(x, g, w)


IN_F32_SECTIONS = 2
IN_SEC_LOGF, IN_SEC_RX, IN_SEC_Q, IN_SEC_V, IN_SEC_OG, IN_SEC_RY, IN_SEC_GATES = range(7)


def _in_proj_kernel(perm_ref, x_ref, g_ref, w_ref, lb_ref, bg_ref, *rest, tps, n_gate_secs, sub_rows, n_side):
    del perm_ref
    side_in, (of_ref, ob_ref), side_out, (h_ref,) = (
        rest[:n_side], rest[n_side:n_side + 2], rest[n_side + 2:2 * n_side + 2], rest[2 * n_side + 2:])
    _run_side_casts(side_in, side_out)
    j = pl.program_id(1)
    sec = j // tps

    @pl.when(j == 0)
    def _():
        h_ref[...] = _rms(x_ref[...], g_ref[...]).astype(BF16)

    tm = h_ref.shape[0]

    def run(out_ref, act):
        for r in range(tm // sub_rows):
            rs = slice(r * sub_rows, (r + 1) * sub_rows)
            out_ref[rs, :] = act(_dot(h_ref[rs, :], w_ref[...])).astype(out_ref.dtype)

    def log_forget(a):
        lb = lb_ref[...]
        return jnp.log(lb + (1.0 - lb) * jax.nn.sigmoid(a))

    sec_cq = IN_SEC_GATES + n_gate_secs
    pl.when(sec == IN_SEC_LOGF)(lambda: run(of_ref, log_forget))
    pl.when(sec == IN_SEC_RX)(lambda: run(of_ref, lambda a: a))
    pl.when((sec == IN_SEC_Q) | (sec == IN_SEC_OG))(lambda: run(ob_ref, jax.nn.silu))
    pl.when((sec == IN_SEC_V) | (sec == sec_cq))(lambda: run(ob_ref, lambda a: a))
    pl.when(sec == IN_SEC_RY)(lambda: run(ob_ref, jax.nn.gelu))
    pl.when((sec >= IN_SEC_GATES) & (sec < sec_cq))(
        lambda: run(ob_ref, lambda a: jax.nn.sigmoid(a + bg_ref[...])))


def in_proj(x, g, w, lb, bg, *, sec, tm, tn, side=()):
    t, d = x.shape
    n = w.shape[2]
    assert t % tm == 0 and sec % tn == 0 and n % sec == 0
    tps = sec // tn
    n_sec = n // sec
    n_gate_secs = bg.shape[1] // sec
    assert n_sec == IN_SEC_GATES + n_gate_secs + 1
    perm = jnp.asarray([1, 4, 0, 2, 3, 5] + list(range(7, 7 + n_gate_secs)) + [6], jnp.int32)
    nf = IN_F32_SECTIONS * tps
    sub_rows = min(tm, IN_PROJ_EPILOGUE_ROWS)
    grid = (t // tm, n // tn)
    side_in, side_out, side_shapes, side_vmem = _side_cast_specs(side, grid)
    est = (2 * tm * d * 4 + 2 * d * tn * 2 + 2 * tm * tn * 4 + 2 * tm * tn * 2 + tm * d * 2
           + 6 * sub_rows * tn * 4 + side_vmem)
    kern = functools.partial(_in_proj_kernel, tps=tps, n_gate_secs=n_gate_secs, sub_rows=sub_rows,
                             n_side=len(side))
    grid_spec = pltpu.PrefetchScalarGridSpec(
        num_scalar_prefetch=1,
        grid=grid,
        in_specs=[
            pl.BlockSpec((tm, d), lambda i, j, perm: (i, 0)),
            pl.BlockSpec((1, d), lambda i, j, perm: (0, 0)),
            pl.BlockSpec((None, d, tn), lambda i, j, perm: (0, 0, perm[j // tps] * tps + j % tps)),
            pl.BlockSpec((1, tn), lambda i, j, perm: (0, jnp.minimum(j, tps - 1))),
            pl.BlockSpec((1, tn), lambda i, j, perm: (0, jnp.clip(j - IN_SEC_GATES * tps, 0,
                                                                   n_gate_secs * tps - 1))),
            *side_in,
        ],
        out_specs=[
            pl.BlockSpec((tm, tn), lambda i, j, perm: (i, jnp.minimum(j, nf - 1))),
            pl.BlockSpec((tm, tn), lambda i, j, perm: (i, jnp.maximum(j - nf, 0))),
            *side_out,
        ],
        scratch_shapes=[pltpu.VMEM((tm, d), BF16)],
    )
    return pl.pallas_call(
        kern,
        grid_spec=grid_spec,
        out_shape=[
            jax.ShapeDtypeStruct((t, IN_F32_SECTIONS * sec), F32),
            jax.ShapeDtypeStruct((t, n - IN_F32_SECTIONS * sec), BF16),
            *side_shapes,
        ],
        compiler_params=_params(("parallel", "arbitrary"), est),
        name="in_proj",
    )(perm, x, g, w, lb, bg, *[job[0] for job in side])


def _hgrn_consts(chunk, dk):
    t = np.arange(chunk)[:, None]
    s = np.arange(chunk)[None, :]
    cum = np.concatenate([(s <= t).astype(np.float32)] * N_PIECES, axis=1)
    lane_blk = np.arange(HALF * dk)[:, None] // dk
    sel = (lane_blk == (np.arange(chunk)[None, :] % HALF)).astype(np.float32)
    return jnp.asarray(cum, BF16), jnp.asarray(sel, BF16)


def _hgrn_kernel(q_ref, lf_ref, v_ref, og_ref, gn_ref, s0_ref, cum_ref, sel_ref, *rest,
                 chunk, n_chunks, dk, n_caches, n_cache_blocks, n_side):
    n_extra = n_caches + n_side
    extra_in, (o_ref, sfin_ref), extra_out, (st_ref, b_ref, k_ref, u_ref, sb_ref) = (
        rest[:n_extra], rest[n_extra:n_extra + 2], rest[n_extra + 2:2 * n_extra + 2], rest[2 * n_extra + 2:])
    _run_head_merges(extra_in[:n_caches], extra_out[:n_caches], n_cache_blocks, n_axes=3)
    _run_side_casts(extra_in[n_caches:], extra_out[n_caches:])
    l = pl.program_id(2)
    n_sub = chunk // SUB
    width = q_ref.shape[1]
    n_heads = width // dk
    heads = [slice(h * dk, (h + 1) * dk) for h in range(n_heads)]

    @pl.when(l == 0)
    def _():
        for h in range(n_heads):
            st_ref[h] = s0_ref[0, h].T

    q = q_ref[...].astype(F32)
    lf = lf_ref[...]
    kk = 1.0 - jnp.exp(lf)
    vb = v_ref[...]
    k_ref[...] = kk

    p0 = lf.astype(BF16)
    r1 = lf - p0.astype(F32)
    p1 = r1.astype(BF16)
    p2 = (r1 - p1.astype(F32)).astype(BF16)
    cum = cum_ref[...]
    b = jnp.concatenate(
        [_dot(cum, jnp.concatenate([p[c * chunk:(c + 1) * chunk, :] for p in (p0, p1, p2)], axis=0))
         for c in range(n_chunks)], axis=0)
    b_ref[...] = b
    tl = n_chunks * chunk

    def rows_of(ref, group, offset):
        return jnp.concatenate(
            [jnp.broadcast_to(ref[pl.ds(g * group + offset, 1), :], (group, width)) for g in range(tl // group)],
            axis=0)

    b_end = rows_of(b_ref, SUB, SUB - 1)
    b_mid = rows_of(b_ref, SUB, HALF - 1)
    b_last = rows_of(b_ref, chunk, chunk - 1)

    qe = (q * jnp.exp(b)).astype(BF16)
    k_dec = (kk * jnp.exp(b_end - b)).astype(BF16)
    k_end = (kk * jnp.exp(b_last - b)).astype(BF16)
    decay = [jnp.exp(b_ref[pl.ds((c + 1) * chunk - 1, 1), :]) for c in range(n_chunks)]
    q_dec = [(q * jnp.exp(jnp.minimum(b - rows_of(b_ref, chunk, (j + 1) * SUB - 1), 0.0))).astype(BF16)
             for j in range(n_sub - 1)]
    q_mid = (q * jnp.exp(jnp.minimum(b - b_mid, 0.0))).astype(BF16)
    k_mid = (kk * jnp.exp(jnp.minimum(b_mid - b, 0.0))).astype(BF16)

    w = [(q * jnp.exp(jnp.minimum(b - rows_of(b_ref, HALF, u), 0.0)) * rows_of(k_ref, HALF, u)).astype(BF16)
         for u in range(HALF)]
    sel = sel_ref[...]
    diag = [_dot(jnp.concatenate([wu[:, hc] for wu in w], axis=1), sel) for hc in heads]

    row = lax.broadcasted_iota(jnp.int32, (chunk, chunk), 0)
    col = lax.broadcasted_iota(jnp.int32, (chunk, chunk), 1)
    row_blk = row // SUB
    col_blk = col // SUB
    mid_mask = (col_blk == row_blk) & (row % SUB >= HALF) & (col % SUB < HALF)
    diag_mask = (col // HALF == row // HALF) & (col <= row)
    eye = (lax.broadcasted_iota(jnp.int32, (dk, dk), 0)
           == lax.broadcasted_iota(jnp.int32, (dk, dk), 1)).astype(F32).astype(BF16)

    chunks = [slice(c * chunk, (c + 1) * chunk) for c in range(n_chunks)]
    units = [(h, hc, c, rows) for h, hc in enumerate(heads) for c, rows in enumerate(chunks)]
    cross, v_t = [], []
    for h, hc, c, rows in units:
        qd = jnp.concatenate([qj[rows, hc] for qj in q_dec] + [q_mid[rows, hc]], axis=0)
        kd = jnp.concatenate([k_dec[rows, hc], k_mid[rows, hc]], axis=0)
        cross.append(_dot_nt(qd, kd))
        v_t.append(_dot_nt(eye, vb[rows, hc]).astype(BF16))

    intra = []
    for i, (h, hc, c, rows) in enumerate(units):
        m = cross[i]
        attn = jnp.where(mid_mask, m[(n_sub - 1) * chunk:, chunk:], 0.0)
        for j in range(n_sub - 1):
            attn = jnp.where((col_blk == j) & (row_blk > j), m[j * chunk:(j + 1) * chunk, :chunk], attn)
        attn = jnp.where(diag_mask, diag[h][rows, :], attn)
        intra.append(_dot(attn.astype(BF16), vb[rows, hc]))
        u_ref[i] = _dot(v_t[i], k_end[rows, hc])

    st = [st_ref[h] for h in range(n_heads)]
    for i, (h, hc, c, rows) in enumerate(units):
        sb_ref[i] = st[h].astype(BF16)
        st[h] = st[h] * decay[c][:, hc] + u_ref[i]
    for h in range(n_heads):
        st_ref[h] = st[h]

    inter = [_dot_nt(qe[rows, hc], sb_ref[i]) for i, (h, hc, c, rows) in enumerate(units)]
    gn = gn_ref[...]
    for h, hc in enumerate(heads):
        per_chunk = range(h * n_chunks, (h + 1) * n_chunks)
        o = jnp.concatenate([intra[i] + inter[i] for i in per_chunk], axis=0)
        ms = jnp.mean(o * o, axis=-1, keepdims=True)
        o = o * lax.rsqrt(ms + EPS) * gn[:, hc] * og_ref[:, hc].astype(F32)
        o_ref[:, hc] = o.astype(o_ref.dtype)

    @pl.when(l == pl.num_programs(2) - 1)
    def _():
        for h in range(n_heads):
            sfin_ref[0, h] = st[h].T


HGRN_UNITS_PER_STEP = 32


def hgrn2(pf, pb, gn, s0, layer, *, bsz, seq, heads, dk, tl, caches=(), side=()):
    chunk = min(CHUNK, seq)
    assert seq % tl == 0 and tl % chunk == 0 and chunk % SUB == 0
    nl = seq // tl
    d_a = heads * dk
    n_chunks = tl // chunk
    hb = max(1, min(heads, HGRN_UNITS_PER_STEP // n_chunks))
    while heads % hb:
        hb -= 1
    n_groups = heads // hb
    width = hb * dk
    n_units = hb * n_chunks

    def sec(k):
        return pl.BlockSpec((tl, width), lambda b, h, l, k=k: (b * nl + l, k * n_groups + h))

    cum, sel = _hgrn_consts(chunk, dk)
    grid = (bsz, n_groups, nl)
    if caches and _head_merge_blocks(caches[0]) > math.prod(grid):
        o_a, s_fin, *casts = hgrn2(pf, pb, gn, s0, layer, bsz=bsz, seq=seq, heads=heads, dk=dk, tl=tl, side=side)
        return (o_a, s_fin, *[c.reshape(*c.shape[:3], -1) for c in caches], *casts)
    cache_in, cache_out, cache_shapes, cache_vmem = _head_merge_specs(caches, grid)
    side_in, side_out, side_shapes, side_vmem = _side_cast_specs(side, grid, flat=True)
    cache_in, cache_out, cache_shapes = cache_in + side_in, cache_out + side_out, cache_shapes + side_shapes
    est = (2 * tl * width * (4 + 3 * 2) + 2 * tl * width * 2 + 6 * hb * dk * dk * 4 + 2 * tl * width * 4
           + n_units * dk * dk * 6 + 2 * (cum.size + sel.size) * 2 + 24 * tl * width * 4
           + cache_vmem + side_vmem)
    kern = functools.partial(_hgrn_kernel, chunk=chunk, n_chunks=n_chunks, dk=dk, n_caches=len(caches),
                             n_cache_blocks=_head_merge_blocks(caches[0]) if caches else 0, n_side=len(side))
    return pl.pallas_call(
        kern,
        grid=grid,
        in_specs=[
            sec(IN_SEC_Q - IN_F32_SECTIONS), sec(IN_SEC_LOGF),
            sec(IN_SEC_V - IN_F32_SECTIONS), sec(IN_SEC_OG - IN_F32_SECTIONS),
            pl.BlockSpec((1, width), lambda b, h, l: (0, h)),
            pl.BlockSpec((None, 1, hb, dk, dk), lambda b, h, l: (layer, b, h, 0, 0)),
            pl.BlockSpec(cum.shape, lambda b, h, l: (0, 0)),
            pl.BlockSpec(sel.shape, lambda b, h, l: (0, 0)),
            *cache_in,
        ],
        out_specs=[
            pl.BlockSpec((tl, width), lambda b, h, l: (b * nl + l, h)),
            pl.BlockSpec((1, hb, dk, dk), lambda b, h, l: (b, h, 0, 0)),
            *cache_out,
        ],
        out_shape=[
            jax.ShapeDtypeStruct((bsz * seq, d_a), BF16),
            jax.ShapeDtypeStruct((bsz, heads, dk, dk), F32),
            *cache_shapes,
        ],
        scratch_shapes=[
            pltpu.VMEM((hb, dk, dk), F32),
            pltpu.VMEM((tl, width), F32),
            pltpu.VMEM((tl, width), F32),
            pltpu.VMEM((n_units, dk, dk), F32),
            pltpu.VMEM((n_units, dk, dk), BF16),
        ],
        compiler_params=_params(("parallel", "parallel", "arbitrary"), est),
        name="hgrn2",
    )(pb, pf, pb, pb, gn, s0, cum, sel, *caches, *[job[0] for job in side])


def _lru_kernel(rx_ref, gy_ref, cw_ref, cb_ref, wax_ref, ba_ref, bx_ref, lam_ref, h0_ref, buf_ref,
                *rest, tl, n_blocks, bw, n_side):
    side_in, (o_ref, hlast_ref), side_out, (xp_ref, a_ref, u_ref, hs_ref, ps_ref, h_ref) = (
        rest[:n_side], rest[n_side:n_side + 2], rest[n_side + 2:2 * n_side + 2], rest[2 * n_side + 2:])
    _run_side_casts(side_in, side_out)
    l = pl.program_id(1)
    keep = CONV_W - 1

    @pl.when(l == 0)
    def _():
        xp_ref[CONV_PAD - keep:CONV_PAD, :] = buf_ref[0]
        h_ref[...] = h0_ref[0]

    x = rx_ref[...]
    xp_ref[CONV_PAD:CONV_PAD + tl, :] = x
    cw = cw_ref[...]
    xc = xp_ref[CONV_PAD - keep:CONV_PAD - keep + tl, :] * cw[0:1, :]
    for j in range(1, CONV_W):
        xc = xc + xp_ref[CONV_PAD - keep + j:CONV_PAD - keep + j + tl, :] * cw[j:j + 1, :]
    xc = xc + cb_ref[...]
    xp_ref[CONV_PAD - keep:CONV_PAD, :] = xp_ref[CONV_PAD + tl - keep:CONV_PAD + tl, :]

    xcb = xc.astype(BF16)
    pre = [_dot(xcb[:, n * bw:(n + 1) * bw], wax_ref[n]) for n in range(n_blocks)]
    r = jax.nn.sigmoid(jnp.concatenate([pn[:, :bw] for pn in pre], axis=-1) + ba_ref[...])
    ig = jax.nn.sigmoid(jnp.concatenate([pn[:, bw:] for pn in pre], axis=-1) + bx_ref[...])
    lam = lam_ref[...]
    softplus_neg = jnp.maximum(-lam, 0.0) + jnp.log1p(jnp.exp(-jnp.abs(lam)))
    log_a = -LRU_C * r * softplus_neg
    a = jnp.exp(log_a)
    mult = jnp.sqrt(jnp.maximum(-jnp.tanh(log_a) * (a * a + 1.0), 0.0))
    a_ref[...] = a
    u_ref[...] = mult * (ig * xc)

    seg = tl // LRU_SEGMENTS
    one = jnp.ones_like(h_ref[...])

    def step(t, carry):
        hs, ps = carry
        new_h, new_p = [], []
        for s in range(LRU_SEGMENTS):
            r = s * seg + t
            a_t = a_ref[pl.ds(r, 1), :]
            h_s = a_t * hs[s] + u_ref[pl.ds(r, 1), :]
            hs_ref[pl.ds(r, 1), :] = h_s
            new_h.append(h_s)
            if s > 0:
                p_s = a_t * ps[s - 1]
                ps_ref[pl.ds(r, 1), :] = p_s
                new_p.append(p_s)
        return tuple(new_h), tuple(new_p)

    init = ((h_ref[...],) + (jnp.zeros_like(one),) * (LRU_SEGMENTS - 1), (one,) * (LRU_SEGMENTS - 1))
    hs, ps = lax.fori_loop(0, seg, step, init, unroll=min(seg, 4))
    h = hs[0]
    gy = gy_ref[...].astype(F32)
    o_ref[0:seg, :] = (hs_ref[0:seg, :] * gy[0:seg, :]).astype(o_ref.dtype)
    for s in range(1, LRU_SEGMENTS):
        rows = slice(s * seg, (s + 1) * seg)
        o_ref[rows, :] = ((hs_ref[rows, :] + ps_ref[rows, :] * h) * gy[rows, :]).astype(o_ref.dtype)
        h = hs[s] + ps[s - 1] * h
    h_ref[...] = h

    @pl.when(l == pl.num_programs(1) - 1)
    def _():
        hlast_ref[0] = h


def conv_lru(pf, pb, cw, cb, wax, ba, bx, lam, h0, buf, layer, *, bsz, seq, d, tl, side=()):
    assert seq % tl == 0 and tl >= CONV_W - 1 and tl % LRU_SEGMENTS == 0
    nl = seq // tl
    n_blocks, bw = wax.shape[0], wax.shape[1]
    vec = pl.BlockSpec((1, d), lambda b, l: (0, 0))
    side_in, side_out, side_shapes, side_vmem = _side_cast_specs(side, (bsz, nl), flat=True)
    est = (2 * tl * d * (4 + 2) + 2 * tl * d * 2 + (3 * tl + CONV_PAD) * d * 4 + 8 * tl * d * 4
           + side_vmem)
    kern = functools.partial(_lru_kernel, tl=tl, n_blocks=n_blocks, bw=bw, n_side=len(side))
    return pl.pallas_call(
        kern,
        grid=(bsz, nl),
        in_specs=[
            pl.BlockSpec((tl, d), lambda b, l: (b * nl + l, IN_SEC_RX)),
            pl.BlockSpec((tl, d), lambda b, l: (b * nl + l, IN_SEC_RY - IN_F32_SECTIONS)),
            pl.BlockSpec((CONV_W, d), lambda b, l: (0, 0)),
            vec,
            pl.BlockSpec((n_blocks, bw, 2 * bw), lambda b, l: (0, 0, 0)),
            vec, vec, vec,
            pl.BlockSpec((None, 1, 1, d), lambda b, l: (layer, b, 0, 0)),
            pl.BlockSpec((None, 1, CONV_W - 1, d), lambda b, l: (layer, b, 0, 0)),
            *side_in,
        ],
        out_specs=[
            pl.BlockSpec((tl, d), lambda b, l: (b * nl + l, 0)),
            pl.BlockSpec((1, 1, d), lambda b, l: (b, 0, 0)),
            *side_out,
        ],
        out_shape=[
            jax.ShapeDtypeStruct((bsz * seq, d), BF16),
            jax.ShapeDtypeStruct((bsz, 1, d), F32),
            *side_shapes,
        ],
        scratch_shapes=[
            pltpu.VMEM((CONV_PAD + tl, d), F32),
            pltpu.VMEM((tl, d), F32),
            pltpu.VMEM((tl, d), F32),
            pltpu.VMEM((tl, d), F32),
            pltpu.VMEM((tl, d), F32),
            pltpu.VMEM((1, d), F32),
        ],
        compiler_params=_params(("parallel", "arbitrary"), est),
        name="conv_lru",
    )(pf, pb, cw, cb, wax, ba, bx, lam, h0, buf, *[job[0] for job in side])


def _mem_attn_kernel(q_ref, k_ref, v_ref, o_ref, *, scale, heads, hd):
    cols = [slice(h * hd, (h + 1) * hd) for h in range(heads)]
    scores = [_dot_nt(q_ref[:, c], k_ref[0, :, c].astype(BF16)) * scale for c in cols]
    probs = [jnp.exp(s - jnp.max(s, axis=-1, keepdims=True)) for s in scores]
    outs = [_dot(p.astype(BF16), v_ref[0, :, c].astype(BF16)) for p, c in zip(probs, cols)]
    for p, o, c in zip(probs, outs, cols):
        o_ref[:, c] = (o / jnp.sum(p, axis=-1, keepdims=True)).astype(o_ref.dtype)


def mem_attn(pb, mem_k, mem_v, layer, *, bsz, seq, heads, hd, col0, k_col, v_col, tl):
    assert seq % tl == 0
    nl = seq // tl
    n_mem = mem_k.shape[2]
    d_c = heads * hd
    est = 4 * tl * d_c * 2 + 4 * n_mem * d_c * 4 + 6 * tl * n_mem * 4
    kern = functools.partial(_mem_attn_kernel, scale=1.0 / math.sqrt(hd), heads=heads, hd=hd)
    return pl.pallas_call(
        kern,
        grid=(bsz, nl),
        in_specs=[
            pl.BlockSpec((tl, d_c), lambda b, l: (b * nl + l, col0)),
            pl.BlockSpec((None, 1, n_mem, d_c), lambda b, l: (layer, b, 0, k_col)),
            pl.BlockSpec((None, 1, n_mem, d_c), lambda b, l: (layer, b, 0, v_col)),
        ],
        out_specs=pl.BlockSpec((tl, d_c), lambda b, l: (b * nl + l, 0)),
        out_shape=jax.ShapeDtypeStruct((bsz * seq, d_c), BF16),
        compiler_params=_params(("parallel", "parallel"), est),
        name="mem_attn",
    )(pb, mem_k, mem_v)


def _merge_kernel(x_ref, oa_ref, ob_ref, oc_ref, g0_ref, g1_ref, g2_ref,
                  wa_ref, wb_ref, wc_ref, wo_ref, gn_ref, *rest, n_side):
    side_in, y_ref, side_out = rest[:n_side], rest[n_side], rest[n_side + 1:]
    _run_side_casts(side_in, side_out)
    m = g0_ref[...].astype(F32) * _dot(oa_ref[...], wa_ref[...])
    m = m + g1_ref[...].astype(F32) * _dot(ob_ref[...], wb_ref[...])
    m = m + g2_ref[...].astype(F32) * _dot(oc_ref[...], wc_ref[...])
    z = _dot(m.astype(BF16), wo_ref[...])
    y_ref[...] = x_ref[...] + _rms(z, gn_ref[...])


def merge(x, oa, ob, oc, pb, wa, wb, wc, wo, gn, *, col_gates, tm, side=()):
    t, d = x.shape
    db = oa.shape[1]
    assert t % tm == 0
    row = lambda i: (i, 0)
    const = lambda i: (0, 0)
    wspec = lambda rows: pl.BlockSpec((None, rows, d), lambda i: (0, 0, 0), pipeline_mode=pl.Buffered(1))
    gate_specs = [pl.BlockSpec((tm, d), lambda i, k=k: (i, col_gates + k)) for k in range(N_GATES)]
    side_in, side_out, side_shapes, side_vmem = _side_cast_specs(side, (t // tm,))
    est = (4 * tm * d * 4 + 6 * tm * db * 2 + 6 * tm * d * 2
           + 3 * db * d * 2 + d * d * 2 + 6 * tm * d * 4 + side_vmem)
    return pl.pallas_call(
        functools.partial(_merge_kernel, n_side=len(side)),
        grid=(t // tm,),
        in_specs=[
            pl.BlockSpec((tm, d), row),
            pl.BlockSpec((tm, db), row), pl.BlockSpec((tm, db), row), pl.BlockSpec((tm, db), row),
            *gate_specs,
            wspec(db), wspec(db), wspec(db), wspec(d),
            pl.BlockSpec((1, d), const),
            *side_in,
        ],
        out_specs=[pl.BlockSpec((tm, d), row), *side_out],
        out_shape=[jax.ShapeDtypeStruct((t, d), F32), *side_shapes],
        compiler_params=_params(("parallel",), est),
        name="merge",
    )(x, oa, ob, oc, pb, pb, pb, wa, wb, wc, wo, gn, *[job[0] for job in side])


def _ffn_kernel(x_ref, gpre_ref, wg_ref, wu_ref, wd_ref, gpost_ref, y_ref, h_ref, acc_ref):
    j = pl.program_id(1)

    @pl.when(j == 0)
    def _():
        h_ref[...] = _rms(x_ref[...], gpre_ref[...]).astype(BF16)
        acc_ref[...] = jnp.zeros_like(acc_ref)

    h = h_ref[...]
    gt = _dot(h, wg_ref[...])
    up = _dot(h, wu_ref[...])
    act = (jax.nn.silu(gt) * up).astype(BF16)
    acc_ref[...] += _dot(act, wd_ref[...])

    @pl.when(j == pl.num_programs(1) - 1)
    def _():
        y_ref[...] = x_ref[...] + _rms(acc_ref[...], gpost_ref[...])


def ffn(x, gpre, w_gu, w_down, gpost, *, tm, tf):
    t, d = x.shape
    d_ff = w_down.shape[1]
    assert t % tm == 0 and d_ff % tf == 0 and w_gu.shape[1:] == (2 * d_ff // tf, d, tf)
    nf = d_ff // tf
    est = 4 * tm * d * 4 + 2 * 3 * d * tf * 2 + tm * d * 2 + tm * d * 4 + 4 * tm * tf * 4
    return pl.pallas_call(
        _ffn_kernel,
        grid=(t // tm, nf),
        in_specs=[
            pl.BlockSpec((tm, d), lambda i, j: (i, 0)),
            pl.BlockSpec((1, d), lambda i, j: (0, 0)),
            pl.BlockSpec((None, None, d, tf), lambda i, j: (0, j, 0, 0)),
            pl.BlockSpec((None, None, d, tf), lambda i, j: (0, nf + j, 0, 0)),
            pl.BlockSpec((None, tf, d), lambda i, j: (0, j, 0)),
            pl.BlockSpec((1, d), lambda i, j: (0, 0)),
        ],
        out_specs=pl.BlockSpec((tm, d), lambda i, j: (i, 0)),
        out_shape=jax.ShapeDtypeStruct((t, d), F32),
        scratch_shapes=[pltpu.VMEM((tm, d), BF16), pltpu.VMEM((tm, d), F32)],
        compiler_params=_params(("parallel", "arbitrary"), est),
        name="ffn",
    )(x, gpre, w_gu, w_gu, w_down, gpost)


def _row_tile(n, target):
    t = min(n, target)
    while n % t:
        t //= 2
    return t


IN_PROJ_ROWS, IN_PROJ_COLS, IN_PROJ_EPILOGUE_ROWS = 1024, 1024, 256
HGRN_ROWS = 2048
LRU_ROWS = 512
MEM_ATTN_ROWS = 1024
MERGE_ROWS = 256
FFN_ROWS, FFN_COLS = 512, 512
MEM_KV_ROWS, MEM_KV_COLS = 512, 1024


NEXT_LAYER_HOSTS = {
    "in_proj": ("w_in",),
    "conv_lru": ("ffn_w_gu", "ffn_w_down"),
    "merge": ("w_branch_a", "w_branch_b", "w_branch_c", "w_out", "mem_w_kv"),
}
SAME_LAYER_HOSTS = {
    "hgrn2": ("w_branch_a", "w_branch_b", "w_branch_c", "w_out", "mem_w_kv"),
    "conv_lru": ("ffn_w_gu", "ffn_w_down"),
}


def _trunk_layer(x2, bsz, seq, mem, state, lb, w, p, jobs=None, caches=()):
    t, d = x2.shape
    s_hg, h_lru, conv_buf, state_layer = state
    heads, dk = s_hg.shape[2], s_hg.shape[3]
    d_a = heads * dk
    d_b = h_lru.shape[-1]
    mem_heads, hd = p["mem_heads"], p["mem_hd"]
    if seq < CONV_W - 1:
        raise NotImplementedError("sequence shorter than the conv history")
    jobs = jobs or {}
    w = dict(w)
    w_next = {}

    def side(host):
        return [(a, layer, FFN_COLS if name == "ffn_w_gu" else None) for name, a, layer, _ in jobs.get(host, ())]

    def keep(host, outs):
        for (name, _, _, for_next), o in zip(jobs.get(host, ()), outs):
            (w_next if for_next else w)[name] = o

    pf, pb, *extra = in_proj(x2, p["norm_pre_mix"], w["w_in"], lb, p["b_gate"], sec=d_a,
                             tm=_row_tile(t, IN_PROJ_ROWS), tn=IN_PROJ_COLS, side=side("in_proj"))
    keep("in_proj", extra)
    n_bf_sections = pb.shape[1] // d_a

    o_a, s_new, *extra = hgrn2(pf, pb, p["hgrn_out_norm"], s_hg, state_layer, bsz=bsz, seq=seq, heads=heads,
                               dk=dk, tl=_row_tile(seq, HGRN_ROWS), caches=caches, side=side("hgrn2"))
    merged_caches = extra[:len(caches)]
    keep("hgrn2", extra[len(caches):])
    o_b, h_last, *extra = conv_lru(pf, pb, p["conv_w"], p["conv_b"], p["lru_wax"], p["lru_ba"], p["lru_bx"],
                                   p["lru_lambda"], h_lru, conv_buf, state_layer, bsz=bsz, seq=seq, d=d_b,
                                   tl=_row_tile(seq, LRU_ROWS), side=side("conv_lru"))
    keep("conv_lru", extra)
    mem_k, mem_v, mem_layer, k_col, v_col = mem(w)
    o_c = mem_attn(pb, mem_k, mem_v, mem_layer, bsz=bsz, seq=seq, heads=mem_heads, hd=hd,
                   col0=n_bf_sections - 1, k_col=k_col, v_col=v_col, tl=_row_tile(seq, MEM_ATTN_ROWS))

    gate_col = (IN_SEC_GATES - IN_F32_SECTIONS) * d_a
    assert gate_col % d == 0
    x2, *extra = merge(x2, o_a, o_b, o_c, pb, w["w_branch_a"], w["w_branch_b"], w["w_branch_c"], w["w_out"],
                       p["norm_post_mix"], col_gates=gate_col // d, tm=_row_tile(t, MERGE_ROWS),
                       side=side("merge"))
    keep("merge", extra)
    x2 = ffn(x2, p["norm_pre_ffn"], w["ffn_w_gu"], w["ffn_w_down"], p["norm_post_ffn"],
             tm=_row_tile(t, FFN_ROWS), tf=FFN_COLS)

    rx_tail = pf.reshape(bsz, seq, -1)[:, seq - (CONV_W - 1):, IN_SEC_RX * d_a:(IN_SEC_RX + 1) * d_a]
    return x2, s_new, h_last.reshape(bsz, d_b), rx_tail, w, w_next, merged_caches


def kernel(x_prompt, x_sample, state_hgrn, state_lru, state_conv, cache_mem_k, cache_mem_v, mem_prompt, norm_mem, mem_w_kv, hgrn_lower_bound, norm_pre_mix, w_in, b_gate, hgrn_out_norm, conv_w, conv_b, lru_wa, lru_ba, lru_wx, lru_bx, lru_lambda, w_branch_a, w_branch_b, w_branch_c, w_out, norm_post_mix, norm_pre_ffn, ffn_w_gu, ffn_w_down, norm_post_ffn):
    depth = w_in.shape[0]
    bp, sp, d = x_prompt.shape
    bs, ss, _ = x_sample.shape
    _, _, heads, dk, dv = state_hgrn.shape
    d_a = heads * dk
    d_b = state_lru.shape[-1]
    n_mem, mem_heads, hd = cache_mem_k.shape[2:]
    d_c = mem_heads * hd
    assert dk == dv and d_b == d_a and d_c == d_a and d == 2 * d_a
    assert b_gate.shape[1] == N_GATES * d

    sm = jax.nn.softmax(hgrn_lower_bound.astype(F32), axis=0)
    lbs = jnp.cumsum(sm, axis=0) - sm[0:1]

    xp = x_prompt.reshape(bp * sp, d)
    xs = x_sample.reshape(bs * ss, d)
    mem2 = mem_prompt.reshape(bp * n_mem, d)
    zero_state = (jnp.zeros((1, bp, heads, dk, dv), F32), jnp.zeros((1, bp, 1, d_b), F32),
                  jnp.zeros((1, bp, CONV_W - 1, d_b), F32), 0)
    lru_s4 = state_lru.reshape(depth, bs, 1, d_b)

    w_f32 = dict(w_in=w_in, w_branch_a=w_branch_a, w_branch_b=w_branch_b, w_branch_c=w_branch_c, w_out=w_out,
                 ffn_w_gu=ffn_w_gu, ffn_w_down=ffn_w_down, mem_w_kv=mem_w_kv)
    assert set(w_f32) == {name for names in NEXT_LAYER_HOSTS.values() for name in names}
    assert set(w_f32) == {"w_in"} | {name for names in SAME_LAYER_HOSTS.values() for name in names}
    w = {"w_in": cast_bf16(w_in, 0)}

    outs = {k: [] for k in ("hg_p", "lru_p", "conv_p", "mk_p", "mv_p", "hg_s", "lru_s", "conv_s")}
    for l in range(depth):
        row = lambda a: a[l].reshape(1, -1)
        p = dict(
            mem_heads=mem_heads, mem_hd=hd,
            norm_pre_mix=row(norm_pre_mix), b_gate=row(b_gate),
            hgrn_out_norm=row(hgrn_out_norm), conv_w=conv_w[l], conv_b=row(conv_b),
            lru_wax=jnp.concatenate([lru_wa[l], lru_wx[l]], axis=-1).astype(BF16),
            lru_ba=row(lru_ba), lru_bx=row(lru_bx), lru_lambda=row(lru_lambda),
            norm_post_mix=row(norm_post_mix), norm_pre_ffn=row(norm_pre_ffn),
            norm_post_ffn=row(norm_post_ffn),
        )
        lb = lbs[l].reshape(1, -1)

        prompt_kv = []

        def prompt_mem(w_now):
            kv = norm_matmul(mem2, row(norm_mem), w_now["mem_w_kv"], tm=_row_tile(bp * n_mem, MEM_KV_ROWS),
                             tn=MEM_KV_COLS)
            prompt_kv.append(kv.reshape(1, bp, n_mem, 2 * d_c))
            return prompt_kv[0], prompt_kv[0], 0, 0, 1

        jobs = {}
        if l == 0:
            for host, names in SAME_LAYER_HOSTS.items():
                jobs.setdefault(host, []).extend((name, w_f32[name], 0, False) for name in names)
        if l + 1 < depth:
            for host, names in NEXT_LAYER_HOSTS.items():
                jobs.setdefault(host, []).extend((name, w_f32[name], l + 1, True) for name in names)
        caches = (cache_mem_k, cache_mem_v) if l == 0 else ()
        xp, s1, h1, c1, w, w_next, merged = _trunk_layer(xp, bp, sp, prompt_mem, zero_state, lb, w, p,
                                                         jobs, caches)
        if l == 0:
            cache_k, cache_v = merged
        kv4 = prompt_kv[0]
        outs["hg_p"].append(s1); outs["lru_p"].append(h1); outs["conv_p"].append(c1)
        outs["mk_p"].append(kv4[0, :, :, :d_c].reshape(bp, n_mem, mem_heads, hd))
        outs["mv_p"].append(kv4[0, :, :, d_c:].reshape(bp, n_mem, mem_heads, hd))

        xs, s2, h2, c2, _, _, _ = _trunk_layer(xs, bs, ss, lambda _: (cache_k, cache_v, l, 0, 0),
                                               (state_hgrn, lru_s4, state_conv, l), lb, w, p)
        outs["hg_s"].append(s2); outs["lru_s"].append(h2); outs["conv_s"].append(c2)
        w = w_next

    st = {k: jnp.stack(v) for k, v in outs.items()}
    return (xp.reshape(bp, sp, d), xs.reshape(bs, ss, d), st["hg_p"], st["lru_p"], st["conv_p"],
            st["mk_p"], st["mv_p"], st["hg_s"], st["lru_s"], st["conv_s"])
```

```python
import functools
import math

import numpy as np
import jax
import jax.numpy as jnp
from jax import lax
from jax.experimental import pallas as pl
from jax.experimental.pallas import tpu as pltpu

F32 = jnp.float32
BF16 = jnp.bfloat16

EPS = 1e-6
LRU_C = 8.0
CHUNK = 64
SUB = 16
HALF = 8
N_PIECES = 3
CONV_W = 4
CONV_PAD = 8
LRU_SEGMENTS = 4
N_GATES = 3

LANES = 128
F32_SUBLANES = 8
BF16_SUBLANES = 16

V7X_VMEM_BYTES = 64 * 1024 * 1024
VMEM_LIMIT_CAP = V7X_VMEM_BYTES - 8 * 1024 * 1024


def _vmem_limit(estimate_bytes):
    return int(min(VMEM_LIMIT_CAP, max(16 * 1024 * 1024, estimate_bytes * 5 // 4)))


def _params(sem, vmem_estimate):
    return pltpu.CompilerParams(dimension_semantics=sem, vmem_limit_bytes=_vmem_limit(vmem_estimate))


def _rms(x, g):
    ms = jnp.mean(x * x, axis=-1, keepdims=True)
    return x * lax.rsqrt(ms + EPS) * g


def _dot(a, b):
    return jnp.dot(a, b, preferred_element_type=F32)


def _dot_nt(a, b):
    return lax.dot_general(a, b, (((1,), (1,)), ((), ())), preferred_element_type=F32)


def _cast_kernel(w_ref, o_ref):
    o_ref[...] = w_ref[...].astype(o_ref.dtype)


CAST_BLOCK_BYTES = 4 * 1024 * 1024


def cast_bf16(w, layer):
    _, r, c = w.shape
    tr = r
    while tr * c * 4 > CAST_BLOCK_BYTES and tr % (2 * BF16_SUBLANES) == 0:
        tr //= 2
    return pl.pallas_call(
        _cast_kernel,
        grid=(r // tr,),
        in_specs=[pl.BlockSpec((None, tr, c), lambda i: (layer, i, 0))],
        out_specs=pl.BlockSpec((None, tr, c), lambda i: (0, i, 0)),
        out_shape=jax.ShapeDtypeStruct((1, r, c), BF16),
        compiler_params=_params(("parallel",), 2 * tr * c * 6),
        name="cast_bf16",
    )(w)


def _flat_step(grid, ids):
    g = 0
    for n, i in zip(grid, ids):
        g = g * n + i
    return g


def _side_cast_specs(side, grid, flat=False):
    in_specs, out_specs, out_shapes, vmem = [], [], [], 0
    n_i = math.prod(grid) if flat else grid[0]
    n_j = 1 if flat or len(grid) < 2 else grid[1]
    for w, layer in side:
        _, r, c = w.shape
        assert r % n_i == 0 and (r // n_i) % BF16_SUBLANES == 0
        br = r // n_i
        split = n_j > 1 and c % n_j == 0 and (c // n_j) % LANES == 0
        bc = c // n_j if split else c

        def block(ids, split=split):
            return (_flat_step(grid, ids), 0) if flat else (ids[0], ids[1] if split else 0)

        in_specs.append(pl.BlockSpec((None, br, bc), lambda *ids, layer=layer, block=block: (layer, *block(ids))))
        out_specs.append(pl.BlockSpec((None, br, bc), lambda *ids, block=block: (0, *block(ids))))
        out_shapes.append(jax.ShapeDtypeStruct((1, r, c), BF16))
        vmem += 2 * br * bc * 6
    return in_specs, out_specs, out_shapes, vmem


def _run_side_casts(side_in, side_out):
    for w_ref, o_ref in zip(side_in, side_out):
        o_ref[...] = w_ref[...].astype(o_ref.dtype)


HEAD_MERGE_ROWS = 128


def _head_merge_blocks(cache):
    depth, bsz, n_mem = cache.shape[:3]
    return depth * bsz * (n_mem // min(HEAD_MERGE_ROWS, n_mem))


def _head_merge_specs(caches, grid):
    in_specs, out_specs, out_shapes, vmem = [], [], [], 0
    for c in caches:
        depth, bsz, n_mem, heads, hd = c.shape
        rows = min(HEAD_MERGE_ROWS, n_mem)
        assert n_mem % rows == 0
        per_b = n_mem // rows
        n_blocks = _head_merge_blocks(c)
        assert n_blocks <= math.prod(grid)

        def where(ids, bsz=bsz, per_b=per_b, n_blocks=n_blocks):
            g = jnp.minimum(_flat_step(grid, ids), n_blocks - 1)
            return g // (bsz * per_b), (g // per_b) % bsz, g % per_b

        in_specs.append(pl.BlockSpec((None, 1, rows, heads, hd), lambda *ids, where=where: (*where(ids), 0, 0)))
        out_specs.append(pl.BlockSpec((None, 1, rows, heads * hd), lambda *ids, where=where: (*where(ids), 0)))
        out_shapes.append(jax.ShapeDtypeStruct((depth, bsz, n_mem, heads * hd), c.dtype))
        vmem += 2 * rows * (F32_SUBLANES * hd + heads * hd) * 4
    return in_specs, out_specs, out_shapes, vmem


def _run_head_merges(ins, outs, n_blocks, n_axes):
    if not ins:
        return
    step = _flat_step([pl.num_programs(a) for a in range(n_axes)], [pl.program_id(a) for a in range(n_axes)])

    @pl.when(step < n_blocks)
    def _():
        for c_ref, o_ref in zip(ins, outs):
            heads, hd = c_ref.shape[2], c_ref.shape[3]
            for h in range(heads):
                o_ref[0, :, h * hd:(h + 1) * hd] = c_ref[0, :, h, :]


def _norm_matmul_kernel(x_ref, g_ref, w_ref, o_ref, h_ref):
    @pl.when(pl.program_id(1) == 0)
    def _():
        h_ref[...] = _rms(x_ref[...], g_ref[...]).astype(BF16)

    o_ref[...] = _dot(h_ref[...], w_ref[...]).astype(o_ref.dtype)


def norm_matmul(x, g, w, *, tm, tn, out_dtype=F32):
    t, d = x.shape
    n = w.shape[2]
    assert t % tm == 0 and n % tn == 0
    est = 2 * tm * d * 4 + 2 * d * tn * 2 + 2 * tm * tn * 4 + tm * d * 2
    return pl.pallas_call(
        _norm_matmul_kernel,
        grid=(t // tm, n // tn),
        in_specs=[
            pl.BlockSpec((tm, d), lambda i, j: (i, 0)),
            pl.BlockSpec((1, d), lambda i, j: (0, 0)),
            pl.BlockSpec((None, d, tn), lambda i, j: (0, 0, j)),
        ],
        out_specs=pl.BlockSpec((tm, tn), lambda i, j: (i, j)),
        out_shape=jax.ShapeDtypeStruct((t, n), out_dtype),
        scratch_shapes=[pltpu.VMEM((tm, d), BF16)],
        compiler_params=_params(("parallel", "arbitrary"), est),
        name="norm_matmul",
    )(x, g, w)


IN_F32_SECTIONS = 2
IN_SEC_LOGF, IN_SEC_RX, IN_SEC_Q, IN_SEC_V, IN_SEC_OG, IN_SEC_RY, IN_SEC_GATES = range(7)


def _in_proj_kernel(perm_ref, x_ref, g_ref, w_ref, lb_ref, bg_ref, *rest, tps, n_gate_secs, sub_rows, n_side):
    del perm_ref
    side_in, (of_ref, ob_ref), side_out, (h_ref,) = (
        rest[:n_side], rest[n_side:n_side + 2], rest[n_side + 2:2 * n_side + 2], rest[2 * n_side + 2:])
    _run_side_casts(side_in, side_out)
    j = pl.program_id(1)
    sec = j // tps

    @pl.when(j == 0)
    def _():
        h_ref[...] = _rms(x_ref[...], g_ref[...]).astype(BF16)

    tm = h_ref.shape[0]

    def run(out_ref, act):
        for r in range(tm // sub_rows):
            rs = slice(r * sub_rows, (r + 1) * sub_rows)
            out_ref[rs, :] = act(_dot(h_ref[rs, :], w_ref[...])).astype(out_ref.dtype)

    def log_forget(a):
        lb = lb_ref[...]
        return jnp.log(lb + (1.0 - lb) * jax.nn.sigmoid(a))

    sec_cq = IN_SEC_GATES + n_gate_secs
    pl.when(sec == IN_SEC_LOGF)(lambda: run(of_ref, log_forget))
    pl.when(sec == IN_SEC_RX)(lambda: run(of_ref, lambda a: a))
    pl.when((sec == IN_SEC_Q) | (sec == IN_SEC_OG))(lambda: run(ob_ref, jax.nn.silu))
    pl.when((sec == IN_SEC_V) | (sec == sec_cq))(lambda: run(ob_ref, lambda a: a))
    pl.when(sec == IN_SEC_RY)(lambda: run(ob_ref, jax.nn.gelu))
    pl.when((sec >= IN_SEC_GATES) & (sec < sec_cq))(
        lambda: run(ob_ref, lambda a: jax.nn.sigmoid(a + bg_ref[...])))


def in_proj(x, g, w, lb, bg, *, sec, tm, tn, side=()):
    t, d = x.shape
    n = w.shape[2]
    assert t % tm == 0 and sec % tn == 0 and n % sec == 0
    tps = sec // tn
    n_sec = n // sec
    n_gate_secs = bg.shape[1] // sec
    assert n_sec == IN_SEC_GATES + n_gate_secs + 1
    perm = jnp.asarray([1, 4, 0, 2, 3, 5] + list(range(7, 7 + n_gate_secs)) + [6], jnp.int32)
    nf = IN_F32_SECTIONS * tps
    sub_rows = min(tm, IN_PROJ_EPILOGUE_ROWS)
    grid = (t // tm, n // tn)
    side_in, side_out, side_shapes, side_vmem = _side_cast_specs(side, grid)
    est = (2 * tm * d * 4 + 2 * d * tn * 2 + 2 * tm * tn * 4 + 2 * tm * tn * 2 + tm * d * 2
           + 6 * sub_rows * tn * 4 + side_vmem)
    kern = functools.partial(_in_proj_kernel, tps=tps, n_gate_secs=n_gate_secs, sub_rows=sub_rows,
                             n_side=len(side))
    grid_spec = pltpu.PrefetchScalarGridSpec(
        num_scalar_prefetch=1,
        grid=grid,
        in_specs=[
            pl.BlockSpec((tm, d), lambda i, j, perm: (i, 0)),
            pl.BlockSpec((1, d), lambda i, j, perm: (0, 0)),
            pl.BlockSpec((None, d, tn), lambda i, j, perm: (0, 0, perm[j // tps] * tps + j % tps)),
            pl.BlockSpec((1, tn), lambda i, j, perm: (0, jnp.minimum(j, tps - 1))),
            pl.BlockSpec((1, tn), lambda i, j, perm: (0, jnp.clip(j - IN_SEC_GATES * tps, 0,
                                                                   n_gate_secs * tps - 1))),
            *side_in,
        ],
        out_specs=[
            pl.BlockSpec((tm, tn), lambda i, j, perm: (i, jnp.minimum(j, nf - 1))),
            pl.BlockSpec((tm, tn), lambda i, j, perm: (i, jnp.maximum(j - nf, 0))),
            *side_out,
        ],
        scratch_shapes=[pltpu.VMEM((tm, d), BF16)],
    )
    return pl.pallas_call(
        kern,
        grid_spec=grid_spec,
        out_shape=[
            jax.ShapeDtypeStruct((t, IN_F32_SECTIONS * sec), F32),
            jax.ShapeDtypeStruct((t, n - IN_F32_SECTIONS * sec), BF16),
            *side_shapes,
        ],
        compiler_params=_params(("parallel", "arbitrary"), est),
        name="in_proj",
    )(perm, x, g, w, lb, bg, *[w_ for w_, _ in side])


def _hgrn_consts(chunk, dk):
    t = np.arange(chunk)[:, None]
    s = np.arange(chunk)[None, :]
    cum = np.concatenate([(s <= t).astype(np.float32)] * N_PIECES, axis=1)
    lane_blk = np.arange(HALF * dk)[:, None] // dk
    sel = (lane_blk == (np.arange(chunk)[None, :] % HALF)).astype(np.float32)
    return jnp.asarray(cum, BF16), jnp.asarray(sel, BF16)


def _hgrn_kernel(q_ref, lf_ref, v_ref, og_ref, gn_ref, s0_ref, cum_ref, sel_ref, *rest,
                 chunk, n_chunks, dk, n_caches, n_cache_blocks, n_side):
    n_extra = n_caches + n_side
    extra_in, (o_ref, sfin_ref), extra_out, (st_ref, b_ref, k_ref, u_ref, sb_ref) = (
        rest[:n_extra], rest[n_extra:n_extra + 2], rest[n_extra + 2:2 * n_extra + 2], rest[2 * n_extra + 2:])
    _run_head_merges(extra_in[:n_caches], extra_out[:n_caches], n_cache_blocks, n_axes=3)
    _run_side_casts(extra_in[n_caches:], extra_out[n_caches:])
    l = pl.program_id(2)
    n_sub = chunk // SUB
    width = q_ref.shape[1]
    n_heads = width // dk
    heads = [slice(h * dk, (h + 1) * dk) for h in range(n_heads)]

    @pl.when(l == 0)
    def _():
        for h in range(n_heads):
            st_ref[h] = s0_ref[0, h].T

    q = q_ref[...].astype(F32)
    lf = lf_ref[...]
    kk = 1.0 - jnp.exp(lf)
    vb = v_ref[...]
    k_ref[...] = kk

    p0 = lf.astype(BF16)
    r1 = lf - p0.astype(F32)
    p1 = r1.astype(BF16)
    p2 = (r1 - p1.astype(F32)).astype(BF16)
    cum = cum_ref[...]
    b = jnp.concatenate(
        [_dot(cum, jnp.concatenate([p[c * chunk:(c + 1) * chunk, :] for p in (p0, p1, p2)], axis=0))
         for c in range(n_chunks)], axis=0)
    b_ref[...] = b
    tl = n_chunks * chunk

    def rows_of(ref, group, offset):
        return jnp.concatenate(
            [jnp.broadcast_to(ref[pl.ds(g * group + offset, 1), :], (group, width)) for g in range(tl // group)],
            axis=0)

    b_end = rows_of(b_ref, SUB, SUB - 1)
    b_mid = rows_of(b_ref, SUB, HALF - 1)
    b_last = rows_of(b_ref, chunk, chunk - 1)

    qe = (q * jnp.exp(b)).astype(BF16)
    k_dec = (kk * jnp.exp(b_end - b)).astype(BF16)
    k_end = (kk * jnp.exp(b_last - b)).astype(BF16)
    decay = [jnp.exp(b_ref[pl.ds((c + 1) * chunk - 1, 1), :]) for c in range(n_chunks)]
    q_dec = [(q * jnp.exp(jnp.minimum(b - rows_of(b_ref, chunk, (j + 1) * SUB - 1), 0.0))).astype(BF16)
             for j in range(n_sub - 1)]
    q_mid = (q * jnp.exp(jnp.minimum(b - b_mid, 0.0))).astype(BF16)
    k_mid = (kk * jnp.exp(jnp.minimum(b_mid - b, 0.0))).astype(BF16)

    w = [(q * jnp.exp(jnp.minimum(b - rows_of(b_ref, HALF, u), 0.0)) * rows_of(k_ref, HALF, u)).astype(BF16)
         for u in range(HALF)]
    sel = sel_ref[...]
    diag = [_dot(jnp.concatenate([wu[:, hc] for wu in w], axis=1), sel) for hc in heads]

    row = lax.broadcasted_iota(jnp.int32, (chunk, chunk), 0)
    col = lax.broadcasted_iota(jnp.int32, (chunk, chunk), 1)
    row_blk = row // SUB
    col_blk = col // SUB
    mid_mask = (col_blk == row_blk) & (row % SUB >= HALF) & (col % SUB < HALF)
    diag_mask = (col // HALF == row // HALF) & (col <= row)
    eye = (lax.broadcasted_iota(jnp.int32, (dk, dk), 0)
           == lax.broadcasted_iota(jnp.int32, (dk, dk), 1)).astype(F32).astype(BF16)

    chunks = [slice(c * chunk, (c + 1) * chunk) for c in range(n_chunks)]
    units = [(h, hc, c, rows) for h, hc in enumerate(heads) for c, rows in enumerate(chunks)]
    cross, v_t = [], []
    for h, hc, c, rows in units:
        qd = jnp.concatenate([qj[rows, hc] for qj in q_dec] + [q_mid[rows, hc]], axis=0)
        kd = jnp.concatenate([k_dec[rows, hc], k_mid[rows, hc]], axis=0)
        cross.append(_dot_nt(qd, kd))
        v_t.append(_dot_nt(eye, vb[rows, hc]).astype(BF16))

    intra = []
    for i, (h, hc, c, rows) in enumerate(units):
        m = cross[i]
        attn = jnp.where(mid_mask, m[(n_sub - 1) * chunk:, chunk:], 0.0)
        for j in range(n_sub - 1):
            attn = jnp.where((col_blk == j) & (row_blk > j), m[j * chunk:(j + 1) * chunk, :chunk], attn)
        attn = jnp.where(diag_mask, diag[h][rows, :], attn)
        intra.append(_dot(attn.astype(BF16), vb[rows, hc]))
        u_ref[i] = _dot(v_t[i], k_end[rows, hc])

    st = [st_ref[h] for h in range(n_heads)]
    for i, (h, hc, c, rows) in enumerate(units):
        sb_ref[i] = st[h].astype(BF16)
        st[h] = st[h] * decay[c][:, hc] + u_ref[i]
    for h in range(n_heads):
        st_ref[h] = st[h]

    inter = [_dot_nt(qe[rows, hc], sb_ref[i]) for i, (h, hc, c, rows) in enumerate(units)]
    gn = gn_ref[...]
    for h, hc in enumerate(heads):
        per_chunk = range(h * n_chunks, (h + 1) * n_chunks)
        o = jnp.concatenate([intra[i] + inter[i] for i in per_chunk], axis=0)
        ms = jnp.mean(o * o, axis=-1, keepdims=True)
        o = o * lax.rsqrt(ms + EPS) * gn[:, hc] * og_ref[:, hc].astype(F32)
        o_ref[:, hc] = o.astype(o_ref.dtype)

    @pl.when(l == pl.num_programs(2) - 1)
    def _():
        for h in range(n_heads):
            sfin_ref[0, h] = st[h].T


HGRN_UNITS_PER_STEP = 32


def hgrn2(pf, pb, gn, s0, layer, *, bsz, seq, heads, dk, tl, caches=(), side=()):
    chunk = min(CHUNK, seq)
    assert seq % tl == 0 and tl % chunk == 0 and chunk % SUB == 0
    nl = seq // tl
    d_a = heads * dk
    n_chunks = tl // chunk
    hb = max(1, min(heads, HGRN_UNITS_PER_STEP // n_chunks))
    while heads % hb:
        hb -= 1
    n_groups = heads // hb
    width = hb * dk
    n_units = hb * n_chunks

    def sec(k):
        return pl.BlockSpec((tl, width), lambda b, h, l, k=k: (b * nl + l, k * n_groups + h))

    cum, sel = _hgrn_consts(chunk, dk)
    grid = (bsz, n_groups, nl)
    if caches and _head_merge_blocks(caches[0]) > math.prod(grid):
        o_a, s_fin, *casts = hgrn2(pf, pb, gn, s0, layer, bsz=bsz, seq=seq, heads=heads, dk=dk, tl=tl, side=side)
        return (o_a, s_fin, *[c.reshape(*c.shape[:3], -1) for c in caches], *casts)
    cache_in, cache_out, cache_shapes, cache_vmem = _head_merge_specs(caches, grid)
    side_in, side_out, side_shapes, side_vmem = _side_cast_specs(side, grid, flat=True)
    cache_in, cache_out, cache_shapes = cache_in + side_in, cache_out + side_out, cache_shapes + side_shapes
    est = (2 * tl * width * (4 + 3 * 2) + 2 * tl * width * 2 + 6 * hb * dk * dk * 4 + 2 * tl * width * 4
           + n_units * dk * dk * 6 + 2 * (cum.size + sel.size) * 2 + 24 * tl * width * 4
           + cache_vmem + side_vmem)
    kern = functools.partial(_hgrn_kernel, chunk=chunk, n_chunks=n_chunks, dk=dk, n_caches=len(caches),
                             n_cache_blocks=_head_merge_blocks(caches[0]) if caches else 0, n_side=len(side))
    return pl.pallas_call(
        kern,
        grid=grid,
        in_specs=[
            sec(IN_SEC_Q - IN_F32_SECTIONS), sec(IN_SEC_LOGF),
            sec(IN_SEC_V - IN_F32_SECTIONS), sec(IN_SEC_OG - IN_F32_SECTIONS),
            pl.BlockSpec((1, width), lambda b, h, l: (0, h)),
            pl.BlockSpec((None, 1, hb, dk, dk), lambda b, h, l: (layer, b, h, 0, 0)),
            pl.BlockSpec(cum.shape, lambda b, h, l: (0, 0)),
            pl.BlockSpec(sel.shape, lambda b, h, l: (0, 0)),
            *cache_in,
        ],
        out_specs=[
            pl.BlockSpec((tl, width), lambda b, h, l: (b * nl + l, h)),
            pl.BlockSpec((1, hb, dk, dk), lambda b, h, l: (b, h, 0, 0)),
            *cache_out,
        ],
        out_shape=[
            jax.ShapeDtypeStruct((bsz * seq, d_a), BF16),
            jax.ShapeDtypeStruct((bsz, heads, dk, dk), F32),
            *cache_shapes,
        ],
        scratch_shapes=[
            pltpu.VMEM((hb, dk, dk), F32),
            pltpu.VMEM((tl, width), F32),
            pltpu.VMEM((tl, width), F32),
            pltpu.VMEM((n_units, dk, dk), F32),
            pltpu.VMEM((n_units, dk, dk), BF16),
        ],
        compiler_params=_params(("parallel", "parallel", "arbitrary"), est),
        name="hgrn2",
    )(pb, pf, pb, pb, gn, s0, cum, sel, *caches, *[w_ for w_, _ in side])


def _lru_kernel(rx_ref, gy_ref, cw_ref, cb_ref, wax_ref, ba_ref, bx_ref, lam_ref, h0_ref, buf_ref,
                *rest, tl, n_blocks, bw, n_side):
    side_in, (o_ref, hlast_ref), side_out, (xp_ref, a_ref, u_ref, hs_ref, ps_ref, h_ref) = (
        rest[:n_side], rest[n_side:n_side + 2], rest[n_side + 2:2 * n_side + 2], rest[2 * n_side + 2:])
    _run_side_casts(side_in, side_out)
    l = pl.program_id(1)
    keep = CONV_W - 1

    @pl.when(l == 0)
    def _():
        xp_ref[CONV_PAD - keep:CONV_PAD, :] = buf_ref[0]
        h_ref[...] = h0_ref[0]

    x = rx_ref[...]
    xp_ref[CONV_PAD:CONV_PAD + tl, :] = x
    cw = cw_ref[...]
    xc = xp_ref[CONV_PAD - keep:CONV_PAD - keep + tl, :] * cw[0:1, :]
    for j in range(1, CONV_W):
        xc = xc + xp_ref[CONV_PAD - keep + j:CONV_PAD - keep + j + tl, :] * cw[j:j + 1, :]
    xc = xc + cb_ref[...]
    xp_ref[CONV_PAD - keep:CONV_PAD, :] = xp_ref[CONV_PAD + tl - keep:CONV_PAD + tl, :]

    xcb = xc.astype(BF16)
    pre = [_dot(xcb[:, n * bw:(n + 1) * bw], wax_ref[n]) for n in range(n_blocks)]
    r = jax.nn.sigmoid(jnp.concatenate([pn[:, :bw] for pn in pre], axis=-1) + ba_ref[...])
    ig = jax.nn.sigmoid(jnp.concatenate([pn[:, bw:] for pn in pre], axis=-1) + bx_ref[...])
    lam = lam_ref[...]
    softplus_neg = jnp.maximum(-lam, 0.0) + jnp.log1p(jnp.exp(-jnp.abs(lam)))
    log_a = -LRU_C * r * softplus_neg
    a = jnp.exp(log_a)
    mult = jnp.sqrt(jnp.maximum(-jnp.tanh(log_a) * (a * a + 1.0), 0.0))
    a_ref[...] = a
    u_ref[...] = mult * (ig * xc)

    seg = tl // LRU_SEGMENTS
    one = jnp.ones_like(h_ref[...])

    def step(t, carry):
        hs, ps = carry
        new_h, new_p = [], []
        for s in range(LRU_SEGMENTS):
            r = s * seg + t
            a_t = a_ref[pl.ds(r, 1), :]
            h_s = a_t * hs[s] + u_ref[pl.ds(r, 1), :]
            hs_ref[pl.ds(r, 1), :] = h_s
            new_h.append(h_s)
            if s > 0:
                p_s = a_t * ps[s - 1]
                ps_ref[pl.ds(r, 1), :] = p_s
                new_p.append(p_s)
        return tuple(new_h), tuple(new_p)

    init = ((h_ref[...],) + (jnp.zeros_like(one),) * (LRU_SEGMENTS - 1), (one,) * (LRU_SEGMENTS - 1))
    hs, ps = lax.fori_loop(0, seg, step, init, unroll=min(seg, 4))
    h = hs[0]
    gy = gy_ref[...].astype(F32)
    o_ref[0:seg, :] = (hs_ref[0:seg, :] * gy[0:seg, :]).astype(o_ref.dtype)
    for s in range(1, LRU_SEGMENTS):
        rows = slice(s * seg, (s + 1) * seg)
        o_ref[rows, :] = ((hs_ref[rows, :] + ps_ref[rows, :] * h) * gy[rows, :]).astype(o_ref.dtype)
        h = hs[s] + ps[s - 1] * h
    h_ref[...] = h

    @pl.when(l == pl.num_programs(1) - 1)
    def _():
        hlast_ref[0] = h


def conv_lru(pf, pb, cw, cb, wax, ba, bx, lam, h0, buf, layer, *, bsz, seq, d, tl, side=()):
    assert seq % tl == 0 and tl >= CONV_W - 1 and tl % LRU_SEGMENTS == 0
    nl = seq // tl
    n_blocks, bw = wax.shape[0], wax.shape[1]
    vec = pl.BlockSpec((1, d), lambda b, l: (0, 0))
    side_in, side_out, side_shapes, side_vmem = _side_cast_specs(side, (bsz, nl), flat=True)
    est = (2 * tl * d * (4 + 2) + 2 * tl * d * 2 + (3 * tl + CONV_PAD) * d * 4 + 8 * tl * d * 4
           + side_vmem)
    kern = functools.partial(_lru_kernel, tl=tl, n_blocks=n_blocks, bw=bw, n_side=len(side))
    return pl.pallas_call(
        kern,
        grid=(bsz, nl),
        in_specs=[
            pl.BlockSpec((tl, d), lambda b, l: (b * nl + l, IN_SEC_RX)),
            pl.BlockSpec((tl, d), lambda b, l: (b * nl + l, IN_SEC_RY - IN_F32_SECTIONS)),
            pl.BlockSpec((CONV_W, d), lambda b, l: (0, 0)),
            vec,
            pl.BlockSpec((n_blocks, bw, 2 * bw), lambda b, l: (0, 0, 0)),
            vec, vec, vec,
            pl.BlockSpec((None, 1, 1, d), lambda b, l: (layer, b, 0, 0)),
            pl.BlockSpec((None, 1, CONV_W - 1, d), lambda b, l: (layer, b, 0, 0)),
            *side_in,
        ],
        out_specs=[
            pl.BlockSpec((tl, d), lambda b, l: (b * nl + l, 0)),
            pl.BlockSpec((1, 1, d), lambda b, l: (b, 0, 0)),
            *side_out,
        ],
        out_shape=[
            jax.ShapeDtypeStruct((bsz * seq, d), BF16),
            jax.ShapeDtypeStruct((bsz, 1, d), F32),
            *side_shapes,
        ],
        scratch_shapes=[
            pltpu.VMEM((CONV_PAD + tl, d), F32),
            pltpu.VMEM((tl, d), F32),
            pltpu.VMEM((tl, d), F32),
            pltpu.VMEM((tl, d), F32),
            pltpu.VMEM((tl, d), F32),
            pltpu.VMEM((1, d), F32),
        ],
        compiler_params=_params(("parallel", "arbitrary"), est),
        name="conv_lru",
    )(pf, pb, cw, cb, wax, ba, bx, lam, h0, buf, *[w_ for w_, _ in side])


def _mem_attn_kernel(q_ref, k_ref, v_ref, o_ref, *, scale, heads, hd):
    cols = [slice(h * hd, (h + 1) * hd) for h in range(heads)]
    scores = [_dot_nt(q_ref[:, c], k_ref[0, :, c].astype(BF16)) * scale for c in cols]
    probs = [jnp.exp(s - jnp.max(s, axis=-1, keepdims=True)) for s in scores]
    outs = [_dot(p.astype(BF16), v_ref[0, :, c].astype(BF16)) for p, c in zip(probs, cols)]
    for p, o, c in zip(probs, outs, cols):
        o_ref[:, c] = (o / jnp.sum(p, axis=-1, keepdims=True)).astype(o_ref.dtype)


def mem_attn(pb, mem_k, mem_v, layer, *, bsz, seq, heads, hd, col0, k_col, v_col, tl):
    assert seq % tl == 0
    nl = seq // tl
    n_mem = mem_k.shape[2]
    d_c = heads * hd
    est = 4 * tl * d_c * 2 + 4 * n_mem * d_c * 4 + 6 * tl * n_mem * 4
    kern = functools.partial(_mem_attn_kernel, scale=1.0 / math.sqrt(hd), heads=heads, hd=hd)
    return pl.pallas_call(
        kern,
        grid=(bsz, nl),
        in_specs=[
            pl.BlockSpec((tl, d_c), lambda b, l: (b * nl + l, col0)),
            pl.BlockSpec((None, 1, n_mem, d_c), lambda b, l: (layer, b, 0, k_col)),
            pl.BlockSpec((None, 1, n_mem, d_c), lambda b, l: (layer, b, 0, v_col)),
        ],
        out_specs=pl.BlockSpec((tl, d_c), lambda b, l: (b * nl + l, 0)),
        out_shape=jax.ShapeDtypeStruct((bsz * seq, d_c), BF16),
        compiler_params=_params(("parallel", "parallel"), est),
        name="mem_attn",
    )(pb, mem_k, mem_v)


def _merge_kernel(x_ref, oa_ref, ob_ref, oc_ref, g0_ref, g1_ref, g2_ref,
                  wa_ref, wb_ref, wc_ref, wo_ref, gn_ref, *rest, n_side, attn):
    if attn is not None:
        (k_ref, v_ref), rest = rest[:2], rest[2:]
    side_in, y_ref, side_out = rest[:n_side], rest[n_side], rest[n_side + 1:]
    _run_side_casts(side_in, side_out)
    if attn is not None:
        heads, hd, scale = attn
        cols = [slice(h * hd, (h + 1) * hd) for h in range(heads)]
        scores = [_dot_nt(oc_ref[:, c], k_ref[0, :, c].astype(BF16)) * scale for c in cols]
    branch_a = _dot(oa_ref[...], wa_ref[...])
    branch_b = _dot(ob_ref[...], wb_ref[...])
    if attn is not None:
        probs = [jnp.exp(s - jnp.max(s, axis=-1, keepdims=True)) for s in scores]
        outs = [_dot(p.astype(BF16), v_ref[0, :, c].astype(BF16)) for p, c in zip(probs, cols)]
        o_c = jnp.concatenate([(o / jnp.sum(p, axis=-1, keepdims=True)).astype(BF16)
                               for p, o in zip(probs, outs)], axis=1)
    else:
        o_c = oc_ref[...]
    m = g0_ref[...].astype(F32) * branch_a
    m = m + g1_ref[...].astype(F32) * branch_b
    m = m + g2_ref[...].astype(F32) * _dot(o_c, wc_ref[...])
    z = _dot(m.astype(BF16), wo_ref[...])
    y_ref[...] = x_ref[...] + _rms(z, gn_ref[...])


def merge(x, oa, ob, oc, pb, wa, wb, wc, wo, gn, *, col_gates, tm, side=(), attn=None):
    t, d = x.shape
    db = oa.shape[1]
    assert t % tm == 0
    row = lambda i: (i, 0)
    const = lambda i: (0, 0)
    wspec = lambda rows: pl.BlockSpec((None, rows, d), lambda i: (0, 0, 0), pipeline_mode=pl.Buffered(1))
    gate_specs = [pl.BlockSpec((tm, d), lambda i, k=k: (i, col_gates + k)) for k in range(N_GATES)]
    side_in, side_out, side_shapes, side_vmem = _side_cast_specs(side, (t // tm,))
    est = (4 * tm * d * 4 + 6 * tm * db * 2 + 6 * tm * d * 2
           + 3 * db * d * 2 + d * d * 2 + 6 * tm * d * 4 + side_vmem)
    if attn is None:
        oc_arr, oc_spec, attn_specs, attn_args, attn_static = oc, pl.BlockSpec((tm, db), row), [], [], None
    else:
        mem_k, mem_v, layer, k_col, v_col, heads, hd, q_col, seq = attn
        assert oc is None and seq % tm == 0 and heads * hd == db
        per_b = seq // tm
        n_mem = mem_k.shape[2]
        oc_arr, oc_spec = pb, pl.BlockSpec((tm, db), lambda i: (i, q_col))
        attn_specs = [pl.BlockSpec((None, 1, n_mem, db), lambda i: (layer, i // per_b, 0, k_col)),
                      pl.BlockSpec((None, 1, n_mem, db), lambda i: (layer, i // per_b, 0, v_col))]
        attn_args = [mem_k, mem_v]
        attn_static = (heads, hd, 1.0 / math.sqrt(hd))
        est += 4 * n_mem * db * 4 + 6 * tm * n_mem * 4
    return pl.pallas_call(
        functools.partial(_merge_kernel, n_side=len(side), attn=attn_static),
        grid=(t // tm,),
        in_specs=[
            pl.BlockSpec((tm, d), row),
            pl.BlockSpec((tm, db), row), pl.BlockSpec((tm, db), row), oc_spec,
            *gate_specs,
            wspec(db), wspec(db), wspec(db), wspec(d),
            pl.BlockSpec((1, d), const),
            *attn_specs,
            *side_in,
        ],
        out_specs=[pl.BlockSpec((tm, d), row), *side_out],
        out_shape=[jax.ShapeDtypeStruct((t, d), F32), *side_shapes],
        compiler_params=_params(("parallel",), est),
        name="merge",
    )(x, oa, ob, oc_arr, pb, pb, pb, wa, wb, wc, wo, gn, *attn_args, *[w_ for w_, _ in side])


def _ffn_kernel(x_ref, gpre_ref, wg_ref, wu_ref, wd_ref, gpost_ref, y_ref, h_ref, acc_ref):
    j = pl.program_id(1)

    @pl.when(j == 0)
    def _():
        h_ref[...] = _rms(x_ref[...], gpre_ref[...]).astype(BF16)
        acc_ref[...] = jnp.zeros_like(acc_ref)

    h = h_ref[...]
    gt = _dot(h, wg_ref[...])
    up = _dot(h, wu_ref[...])
    act = (jax.nn.silu(gt) * up).astype(BF16)
    acc_ref[...] += _dot(act, wd_ref[...])

    @pl.when(j == pl.num_programs(1) - 1)
    def _():
        y_ref[...] = x_ref[...] + _rms(acc_ref[...], gpost_ref[...])


def ffn(x, gpre, w_gu, w_down, gpost, *, tm, tf):
    t, d = x.shape
    d_ff = w_down.shape[1]
    assert t % tm == 0 and d_ff % tf == 0
    nf = d_ff // tf
    est = 4 * tm * d * 4 + 2 * 3 * d * tf * 2 + tm * d * 2 + tm * d * 4 + 4 * tm * tf * 4
    return pl.pallas_call(
        _ffn_kernel,
        grid=(t // tm, nf),
        in_specs=[
            pl.BlockSpec((tm, d), lambda i, j: (i, 0)),
            pl.BlockSpec((1, d), lambda i, j: (0, 0)),
            pl.BlockSpec((None, d, tf), lambda i, j: (0, 0, j)),
            pl.BlockSpec((None, d, tf), lambda i, j: (0, 0, nf + j)),
            pl.BlockSpec((None, tf, d), lambda i, j: (0, j, 0)),
            pl.BlockSpec((1, d), lambda i, j: (0, 0)),
        ],
        out_specs=pl.BlockSpec((tm, d), lambda i, j: (i, 0)),
        out_shape=jax.ShapeDtypeStruct((t, d), F32),
        scratch_shapes=[pltpu.VMEM((tm, d), BF16), pltpu.VMEM((tm, d), F32)],
        compiler_params=_params(("parallel", "arbitrary"), est),
        name="ffn",
    )(x, gpre, w_gu, w_gu, w_down, gpost)


def _row_tile(n, target):
    t = min(n, target)
    while n % t:
        t //= 2
    return t


IN_PROJ_ROWS, IN_PROJ_COLS, IN_PROJ_EPILOGUE_ROWS = 1024, 1024, 256
HGRN_ROWS = 2048
LRU_ROWS = 512
MEM_ATTN_ROWS = 1024
MERGE_ROWS = 256
FFN_ROWS, FFN_COLS = 512, 512
MEM_KV_ROWS, MEM_KV_COLS = 512, 1024


NEXT_LAYER_HOSTS = {
    "in_proj": ("w_in",),
    "conv_lru": ("ffn_w_gu", "ffn_w_down"),
    "merge": ("w_branch_a", "w_branch_b", "w_branch_c", "w_out", "mem_w_kv"),
}
SAME_LAYER_HOSTS = {
    "hgrn2": ("w_branch_a", "w_branch_b", "w_branch_c", "w_out", "mem_w_kv"),
    "conv_lru": ("ffn_w_gu", "ffn_w_down"),
}


def _trunk_layer(x2, bsz, seq, mem, state, lb, w, p, jobs=None, caches=()):
    t, d = x2.shape
    s_hg, h_lru, conv_buf, state_layer = state
    heads, dk = s_hg.shape[2], s_hg.shape[3]
    d_a = heads * dk
    d_b = h_lru.shape[-1]
    mem_heads, hd = p["mem_heads"], p["mem_hd"]
    if seq < CONV_W - 1:
        raise NotImplementedError("sequence shorter than the conv history")
    jobs = jobs or {}
    w = dict(w)
    w_next = {}

    def side(host):
        return [(a, layer) for _, a, layer, _ in jobs.get(host, ())]

    def keep(host, outs):
        for (name, _, _, for_next), o in zip(jobs.get(host, ()), outs):
            (w_next if for_next else w)[name] = o

    pf, pb, *extra = in_proj(x2, p["norm_pre_mix"], w["w_in"], lb, p["b_gate"], sec=d_a,
                             tm=_row_tile(t, IN_PROJ_ROWS), tn=IN_PROJ_COLS, side=side("in_proj"))
    keep("in_proj", extra)
    n_bf_sections = pb.shape[1] // d_a

    o_a, s_new, *extra = hgrn2(pf, pb, p["hgrn_out_norm"], s_hg, state_layer, bsz=bsz, seq=seq, heads=heads,
                               dk=dk, tl=_row_tile(seq, HGRN_ROWS), caches=caches, side=side("hgrn2"))
    merged_caches = extra[:len(caches)]
    keep("hgrn2", extra[len(caches):])
    o_b, h_last, *extra = conv_lru(pf, pb, p["conv_w"], p["conv_b"], p["lru_wax"], p["lru_ba"], p["lru_bx"],
                                   p["lru_lambda"], h_lru, conv_buf, state_layer, bsz=bsz, seq=seq, d=d_b,
                                   tl=_row_tile(seq, LRU_ROWS), side=side("conv_lru"))
    keep("conv_lru", extra)
    mem_k, mem_v, mem_layer, k_col, v_col = mem(w)
    merge_rows = _row_tile(t, MERGE_ROWS)
    if seq % merge_rows == 0:
        o_c = None
        attn = (mem_k, mem_v, mem_layer, k_col, v_col, mem_heads, hd, n_bf_sections - 1, seq)
    else:
        attn = None
        o_c = mem_attn(pb, mem_k, mem_v, mem_layer, bsz=bsz, seq=seq, heads=mem_heads, hd=hd,
                       col0=n_bf_sections - 1, k_col=k_col, v_col=v_col, tl=_row_tile(seq, MEM_ATTN_ROWS))

    gate_col = (IN_SEC_GATES - IN_F32_SECTIONS) * d_a
    assert gate_col % d == 0
    x2, *extra = merge(x2, o_a, o_b, o_c, pb, w["w_branch_a"], w["w_branch_b"], w["w_branch_c"], w["w_out"],
                       p["norm_post_mix"], col_gates=gate_col // d, tm=merge_rows,
                       side=side("merge"), attn=attn)
    keep("merge", extra)
    x2 = ffn(x2, p["norm_pre_ffn"], w["ffn_w_gu"], w["ffn_w_down"], p["norm_post_ffn"],
             tm=_row_tile(t, FFN_ROWS), tf=FFN_COLS)

    rx_tail = pf.reshape(bsz, seq, -1)[:, seq - (CONV_W - 1):, IN_SEC_RX * d_a:(IN_SEC_RX + 1) * d_a]
    return x2, s_new, h_last.reshape(bsz, d_b), rx_tail, w, w_next, merged_caches


def kernel(x_prompt, x_sample, state_hgrn, state_lru, state_conv, cache_mem_k, cache_mem_v, mem_prompt, norm_mem, mem_w_kv, hgrn_lower_bound, norm_pre_mix, w_in, b_gate, hgrn_out_norm, conv_w, conv_b, lru_wa, lru_ba, lru_wx, lru_bx, lru_lambda, w_branch_a, w_branch_b, w_branch_c, w_out, norm_post_mix, norm_pre_ffn, ffn_w_gu, ffn_w_down, norm_post_ffn):
    depth = w_in.shape[0]
    bp, sp, d = x_prompt.shape
    bs, ss, _ = x_sample.shape
    _, _, heads, dk, dv = state_hgrn.shape
    d_a = heads * dk
    d_b = state_lru.shape[-1]
    n_mem, mem_heads, hd = cache_mem_k.shape[2:]
    d_c = mem_heads * hd
    assert dk == dv and d_b == d_a and d_c == d_a and d == 2 * d_a
    assert b_gate.shape[1] == N_GATES * d

    sm = jax.nn.softmax(hgrn_lower_bound.astype(F32), axis=0)
    lbs = jnp.cumsum(sm, axis=0) - sm[0:1]

    xp = x_prompt.reshape(bp * sp, d)
    xs = x_sample.reshape(bs * ss, d)
    mem2 = mem_prompt.reshape(bp * n_mem, d)
    zero_state = (jnp.zeros((1, bp, heads, dk, dv), F32), jnp.zeros((1, bp, 1, d_b), F32),
                  jnp.zeros((1, bp, CONV_W - 1, d_b), F32), 0)
    lru_s4 = state_lru.reshape(depth, bs, 1, d_b)

    w_f32 = dict(w_in=w_in, w_branch_a=w_branch_a, w_branch_b=w_branch_b, w_branch_c=w_branch_c, w_out=w_out,
                 ffn_w_gu=ffn_w_gu, ffn_w_down=ffn_w_down, mem_w_kv=mem_w_kv)
    assert set(w_f32) == {name for names in NEXT_LAYER_HOSTS.values() for name in names}
    assert set(w_f32) == {"w_in"} | {name for names in SAME_LAYER_HOSTS.values() for name in names}
    w = {"w_in": cast_bf16(w_in, 0)}

    outs = {k: [] for k in ("hg_p", "lru_p", "conv_p", "mk_p", "mv_p", "hg_s", "lru_s", "conv_s")}
    for l in range(depth):
        row = lambda a: a[l].reshape(1, -1)
        p = dict(
            mem_heads=mem_heads, mem_hd=hd,
            norm_pre_mix=row(norm_pre_mix), b_gate=row(b_gate),
            hgrn_out_norm=row(hgrn_out_norm), conv_w=conv_w[l], conv_b=row(conv_b),
            lru_wax=jnp.concatenate([lru_wa[l], lru_wx[l]], axis=-1).astype(BF16),
            lru_ba=row(lru_ba), lru_bx=row(lru_bx), lru_lambda=row(lru_lambda),
            norm_post_mix=row(norm_post_mix), norm_pre_ffn=row(norm_pre_ffn),
            norm_post_ffn=row(norm_post_ffn),
        )
        lb = lbs[l].reshape(1, -1)

        prompt_kv = []

        def prompt_mem(w_now):
            kv = norm_matmul(mem2, row(norm_mem), w_now["mem_w_kv"], tm=_row_tile(bp * n_mem, MEM_KV_ROWS),
                             tn=MEM_KV_COLS)
            prompt_kv.append(kv.reshape(1, bp, n_mem, 2 * d_c))
            return prompt_kv[0], prompt_kv[0], 0, 0, 1

        jobs = {}
        if l == 0:
            for host, names in SAME_LAYER_HOSTS.items():
                jobs.setdefault(host, []).extend((name, w_f32[name], 0, False) for name in names)
        if l + 1 < depth:
            for host, names in NEXT_LAYER_HOSTS.items():
                jobs.setdefault(host, []).extend((name, w_f32[name], l + 1, True) for name in names)
        caches = (cache_mem_k, cache_mem_v) if l == 0 else ()
        xp, s1, h1, c1, w, w_next, merged = _trunk_layer(xp, bp, sp, prompt_mem, zero_state, lb, w, p,
                                                         jobs, caches)
        if l == 0:
            cache_k, cache_v = merged
        kv4 = prompt_kv[0]
        outs["hg_p"].append(s1); outs["lru_p"].append(h1); outs["conv_p"].append(c1)
        outs["mk_p"].append(kv4[0, :, :, :d_c].reshape(bp, n_mem, mem_heads, hd))
        outs["mv_p"].append(kv4[0, :, :, d_c:].reshape(bp, n_mem, mem_heads, hd))

        xs, s2, h2, c2, _, _, _ = _trunk_layer(xs, bs, ss, lambda _: (cache_k, cache_v, l, 0, 0),
                                               (state_hgrn, lru_s4, state_conv, l), lb, w, p)
        outs["hg_s"].append(s2); outs["lru_s"].append(h2); outs["conv_s"].append(c2)
        w = w_next

    st = {k: jnp.stack(v) for k, v in outs.items()}
    return (xp.reshape(bp, sp, d), xs.reshape(bs, ss, d), st["hg_p"], st["lru_p"], st["conv_p"],
            st["mk_p"], st["mv_p"], st["hg_s"], st["lru_s"], st["conv_s"])
```
